```python
import math
import jax, jax.numpy as jnp
from jax import lax
import numpy as np

D_MODEL = 2048
BATCH = 2
SEQ = 16384
DEPTH = 2

GRID_W = 64
CTX_LEN = 256
N_MIXERS = 2

N_HEADS = 32
N_KV_HEADS = 4
HEAD_DIM = D_MODEL // N_HEADS
GQA_GROUP = N_HEADS // N_KV_HEADS
ATTN_DIM = N_HEADS * HEAD_DIM
KV_DIM = N_KV_HEADS * HEAD_DIM
QKV_DIM = ATTN_DIM + 2 * KV_DIM
WINDOW = 128
BLOCK = 128
ROPE_BASE = 10000.0
AXIS_DIM = HEAD_DIM // 2
NEG_INF = -1e30

HY_EMB = 33
HY_FILTER_WIDTH = 64
HY_FAST_DECAY = 0.3
HY_SLOW_DECAY = 1.5
HY_TARGET = 1e-2
SHORT_CONV = 3

N_EXPERTS = 16
N_GROUPS = 4
EXPERTS_PER_GROUP = N_EXPERTS // N_GROUPS
TOP_K = 2
GROUP_SCORE_K = 2
EXPERT_FF = D_MODEL // 2

LN_EPS = 1e-5
DEEPNORM_ALPHA = (2 * DEPTH) ** 0.25
DEEPNORM_BETA = (8 * DEPTH) ** -0.25
N_ATTN_LAYERS = (DEPTH + N_MIXERS - 1) // N_MIXERS
N_HYENA_LAYERS = DEPTH // N_MIXERS

kernel_name = "hybrid_swa_hyena_grouped_moe_deepnorm"

F32 = jnp.float32


def layer_norm(x, g, b):
    xf = x.astype(F32)
    mu = jnp.mean(xf, axis=-1, keepdims=True)
    var = jnp.mean(jnp.square(xf - mu), axis=-1, keepdims=True)
    out = (xf - mu) * lax.rsqrt(var + LN_EPS) * g.astype(F32) + b.astype(F32)
    return out.astype(x.dtype)


def ada_modulation(cond, w, b):
    m = jax.nn.silu(cond) @ w + b
    return jnp.split(m, 6, axis=-1)


def axial_rope_tables(n_tok):
    rows = n_tok // GRID_W
    row = jnp.repeat(jnp.arange(rows, dtype=F32), GRID_W)
    col = jnp.tile(jnp.arange(GRID_W, dtype=F32), rows)
    inv = ROPE_BASE ** (-jnp.arange(0, AXIS_DIM, 2, dtype=F32) / AXIS_DIM)
    ang_r = row[:, None] * inv
    ang_c = col[:, None] * inv
    return (jnp.cos(ang_r), jnp.sin(ang_r), jnp.cos(ang_c), jnp.sin(ang_c))


def rotate_pairs(xh, cos, sin):
    x1, x2 = jnp.split(xh, 2, axis=-1)
    return jnp.concatenate([x1 * cos - x2 * sin, x2 * cos + x1 * sin], axis=-1)


def apply_axial_rope(x, tables):
    cos_r, sin_r, cos_c, sin_c = tables
    shape = (1, x.shape[1]) + (1,) * (x.ndim - 3) + (AXIS_DIM // 2,)
    rs = lambda t: t.reshape(shape).astype(x.dtype)
    xr = rotate_pairs(x[..., :AXIS_DIM], rs(cos_r), rs(sin_r))
    xc = rotate_pairs(x[..., AXIS_DIM:], rs(cos_c), rs(sin_c))
    return jnp.concatenate([xr, xc], axis=-1)


def banded_window_attention(q, k, v, kc, vc, sink_kg):
    B, S = q.shape[0], q.shape[1]
    C = kc.shape[1]
    nb = S // BLOCK
    kp = jnp.pad(k, ((0, 0), (BLOCK, BLOCK), (0, 0), (0, 0)))
    vp = jnp.pad(v, ((0, 0), (BLOCK, BLOCK), (0, 0), (0, 0)))
    qb = q.reshape(B, nb, BLOCK, N_KV_HEADS, GQA_GROUP, HEAD_DIM).transpose(1, 0, 2, 3, 4, 5)
    rr = jnp.arange(BLOCK)
    jj = jnp.arange(3 * BLOCK)
    band = jnp.abs(jj[None, :] - BLOCK - rr[:, None]) <= WINDOW
    s_sink = jnp.broadcast_to(sink_kg[None, :, :, None, None], (B, N_KV_HEADS, GQA_GROUP, BLOCK, 1))

    def one_block(args):
        n, qblk = args
        start = n * BLOCK
        kb = lax.dynamic_slice_in_dim(kp, start, 3 * BLOCK, axis=1)
        vb = lax.dynamic_slice_in_dim(vp, start, 3 * BLOCK, axis=1)
        kpos = start - BLOCK + jj
        valid = band & ((kpos >= 0) & (kpos < S))[None, :]
        s_loc = jnp.einsum('brkgd,bjkd->bkgrj', qblk, kb).astype(F32)
        s_loc = jnp.where(valid, s_loc, NEG_INF)
        s_ctx = jnp.einsum('brkgd,bckd->bkgrc', qblk, kc).astype(F32)
        probs = jax.nn.softmax(jnp.concatenate([s_loc, s_ctx, s_sink], axis=-1), axis=-1)
        p_loc = probs[..., :3 * BLOCK].astype(v.dtype)
        p_ctx = probs[..., 3 * BLOCK:3 * BLOCK + C].astype(v.dtype)
        return (jnp.einsum('bkgrj,bjkd->brkgd', p_loc, vb)
                + jnp.einsum('bkgrc,bckd->brkgd', p_ctx, vc))

    out = lax.map(one_block, (jnp.arange(nb), qb))
    return out.transpose(1, 0, 2, 3, 4, 5).reshape(B, S, ATTN_DIM)


def context_attention(qc, kc, vc, sink_kg):
    B, C = qc.shape[0], qc.shape[1]
    s = jnp.einsum('bqkgd,bckd->bkgqc', qc, kc).astype(F32)
    s_sink = jnp.broadcast_to(sink_kg[None, :, :, None, None], s.shape[:-1] + (1,))
    pr = jax.nn.softmax(jnp.concatenate([s, s_sink], axis=-1), axis=-1)[..., :C].astype(vc.dtype)
    return jnp.einsum('bkgqc,bckd->bqkgd', pr, vc).reshape(B, C, ATTN_DIM)


def attention_mixer(u, uc, w_in, b_in, sink, w_out, rope, ctx_out):
    B, S = u.shape[0], u.shape[1]
    C = uc.shape[1]
    scale = HEAD_DIM ** -0.5
    p = u @ w_in + b_in
    q = p[..., :ATTN_DIM].reshape(B, S, N_KV_HEADS, GQA_GROUP, HEAD_DIM)
    k = p[..., ATTN_DIM:ATTN_DIM + KV_DIM].reshape(B, S, N_KV_HEADS, HEAD_DIM)
    v = p[..., ATTN_DIM + KV_DIM:].reshape(B, S, N_KV_HEADS, HEAD_DIM)
    q = apply_axial_rope(q, rope) * scale
    k = apply_axial_rope(k, rope)
    if ctx_out:
        pc = uc @ w_in + b_in
        qc = pc[..., :ATTN_DIM].reshape(B, C, N_KV_HEADS, GQA_GROUP, HEAD_DIM) * scale
        kvc = pc[..., ATTN_DIM:]
    else:
        kvc = uc @ w_in[:, ATTN_DIM:] + b_in[ATTN_DIM:]
    kc = kvc[..., :KV_DIM].reshape(B, C, N_KV_HEADS, HEAD_DIM)
    vc = kvc[..., KV_DIM:].reshape(B, C, N_KV_HEADS, HEAD_DIM)
    sink_kg = sink.astype(F32).reshape(N_KV_HEADS, GQA_GROUP)
    y = banded_window_attention(q, k, v, kc, vc, sink_kg) @ w_out
    yc = context_attention(qc, kc, vc, sink_kg) @ w_out if ctx_out else None
    return y, yc


def hyena_pos_features(L):
    t = jnp.linspace(0.0, 1.0, L, dtype=F32)[:, None]
    bands = (HY_EMB - 1) // 2
    w = 2.0 * math.pi * jnp.arange(L, dtype=F32)[:, None] / L
    f = jnp.linspace(1e-4, bands - 1, bands, dtype=F32)[None, :]
    z = jnp.concatenate([t, jnp.cos(f * w), -jnp.sin(f * w)], axis=-1)
    return t, z


def hyena_filter(L, f_w1, f_b1, f_w2, f_b2, f_w3, f_b3, f_freq, f_w4):
    t, z = hyena_pos_features(L)
    freq = f_freq.astype(F32)
    hdn = jnp.sin(freq * (z @ f_w1.astype(F32) + f_b1.astype(F32)))
    hdn = jnp.sin(freq * (hdn @ f_w2.astype(F32) + f_b2.astype(F32)))
    hdn = jnp.sin(freq * (hdn @ f_w3.astype(F32) + f_b3.astype(F32)))
    h = (hdn @ f_w4.astype(F32)).reshape(L, 2, D_MODEL)
    max_decay = math.log(HY_TARGET) / HY_FAST_DECAY
    min_decay = math.log(HY_TARGET) / HY_SLOW_DECAY
    deltas = jnp.abs(jnp.linspace(min_decay, max_decay, D_MODEL, dtype=F32))
    h = h * jnp.exp(-t * deltas)[:, None, :]
    h_fwd, h_bwd = h[:, 0], h[:, 1]
    l1 = jnp.sum(jnp.abs(h_fwd), axis=0) + jnp.sum(jnp.abs(h_bwd[1:]), axis=0)
    h_fwd = h_fwd / l1
    h_bwd = h_bwd / l1
    return jnp.concatenate([h_fwd, jnp.zeros((1, D_MODEL), F32), h_bwd[:0:-1]], axis=0)


def bidirectional_long_conv(z, filt2):
    L = z.shape[1]
    zf = jnp.fft.rfft(z.astype(F32), n=2 * L, axis=1)
    ff = jnp.fft.rfft(filt2, n=2 * L, axis=0)
    y = jnp.fft.irfft(zf * ff[None], n=2 * L, axis=1)[:, :L]
    return y.astype(z.dtype)


def hyena_mixer(u, hp):
    (w_in, b_in, conv_w, conv_b, f_w1, f_b1, f_w2, f_b2, f_w3, f_b3, f_freq, f_w4,
     f_bias, w_out, b_out) = hp
    L = u.shape[1]
    p = u @ w_in + b_in
    pp = jnp.pad(p, ((0, 0), (1, 1), (0, 0)))
    p = conv_w[0] * pp[:, :-2] + conv_w[1] * pp[:, 1:-1] + conv_w[2] * pp[:, 2:] + conv_b
    x0, x1, v = jnp.split(p, 3, axis=-1)
    z = x1 * v
    filt2 = hyena_filter(L, f_w1, f_b1, f_w2, f_b2, f_w3, f_b3, f_freq, f_w4)
    y = x0 * (bidirectional_long_conv(z, filt2) + f_bias * z)
    return y @ w_out + b_out


def grouped_moe(u, router_w, router_b, w1, w3, w2):
    shp = u.shape
    t = u.reshape(-1, D_MODEL)
    T = t.shape[0]
    s = jax.nn.sigmoid((t @ router_w).astype(F32))
    sel = s + router_b.astype(F32)
    grp = sel.reshape(T, N_GROUPS, EXPERTS_PER_GROUP)
    gscore = jnp.sum(lax.top_k(grp, GROUP_SCORE_K)[0], axis=-1)
    gbest = jnp.argmax(gscore, axis=-1)
    in_group = (jnp.arange(N_EXPERTS) // EXPERTS_PER_GROUP)[None, :] == gbest[:, None]
    _, idx = lax.top_k(jnp.where(in_group, sel, -jnp.inf), TOP_K)
    g = jnp.take_along_axis(s, idx, axis=-1)
    g = g / jnp.sum(g, axis=-1, keepdims=True)
    combine = jnp.sum(jax.nn.one_hot(idx, N_EXPERTS, dtype=F32) * g[..., None], axis=1)
    y = jnp.zeros_like(t)
    for e in range(N_EXPERTS):
        hid = jax.nn.silu(t @ w1[e]) * (t @ w3[e])
        y = y + combine[:, e:e + 1].astype(t.dtype) * (hid @ w2[e])
    return y.reshape(shp)


def setup_inputs(seed: int = 0) -> dict:
    key = jax.random.key(seed)
    ks = iter(jax.random.split(key, 40))
    nrm = lambda shape, s: jax.random.normal(next(ks), shape, F32) * s
    D, NA, NH = D_MODEL, N_ATTN_LAYERS, N_HYENA_LAYERS
    return {
        "x": nrm((BATCH, SEQ, D), 1.0),
        "c": nrm((BATCH, D), 1.0),
        "ctx": nrm((BATCH, CTX_LEN, D), 1.0),
        "c_ctx": nrm((D,), 1.0),
        "ada_w": nrm((DEPTH, D, 6 * D), 0.5 * D ** -0.5),
        "ada_b": nrm((DEPTH, 6 * D), 0.02),
        "attn_w_in": nrm((NA, D, QKV_DIM), D ** -0.5),
        "attn_b_in": nrm((NA, QKV_DIM), 0.02),
        "attn_sink": nrm((NA, N_HEADS), 0.5),
        "attn_w_out": nrm((NA, ATTN_DIM, D), DEEPNORM_BETA * ATTN_DIM ** -0.5),
        "hy_w_in": nrm((NH, D, 3 * D), D ** -0.5),
        "hy_b_in": nrm((NH, 3 * D), 0.02),
        "hy_conv_w": nrm((NH, SHORT_CONV, 3 * D), SHORT_CONV ** -0.5),
        "hy_conv_b": nrm((NH, 3 * D), 0.02),
        "hy_f_w1": nrm((NH, HY_EMB, HY_FILTER_WIDTH), HY_EMB ** -0.5),
        "hy_f_b1": nrm((NH, HY_FILTER_WIDTH), 0.1),
        "hy_f_w2": nrm((NH, HY_FILTER_WIDTH, HY_FILTER_WIDTH), HY_FILTER_WIDTH ** -0.5),
        "hy_f_b2": nrm((NH, HY_FILTER_WIDTH), 0.1),
        "hy_f_w3": nrm((NH, HY_FILTER_WIDTH, HY_FILTER_WIDTH), HY_FILTER_WIDTH ** -0.5),
        "hy_f_b3": nrm((NH, HY_FILTER_WIDTH), 0.1),
        "hy_f_freq": 1.0 + nrm((NH, HY_FILTER_WIDTH), 0.02),
        "hy_f_w4": nrm((NH, HY_FILTER_WIDTH, 2 * D), HY_FILTER_WIDTH ** -0.5),
        "hy_f_bias": nrm((NH, D), 1.0),
        "hy_w_out": nrm((NH, D, D), DEEPNORM_BETA * D ** -0.5),
        "hy_b_out": nrm((NH, D), 0.02),
        "ln1_g": 1.0 + nrm((DEPTH, D), 0.02),
        "ln1_b": nrm((DEPTH, D), 0.02),
        "ln2_g": 1.0 + nrm((DEPTH, D), 0.02),
        "ln2_b": nrm((DEPTH, D), 0.02),
        "router_w": nrm((D, N_EXPERTS), D ** -0.5),
        "router_b": nrm((N_EXPERTS,), 0.01),
        "moe_w1": nrm((DEPTH, N_EXPERTS, D, EXPERT_FF), D ** -0.5),
        "moe_w3": nrm((DEPTH, N_EXPERTS, D, EXPERT_FF), D ** -0.5),
        "moe_w2": nrm((DEPTH, N_EXPERTS, EXPERT_FF, D), DEEPNORM_BETA * EXPERT_FF ** -0.5),
    }


def reference(x, c, ctx, c_ctx, ada_w, ada_b, attn_w_in, attn_b_in, attn_sink, attn_w_out,
              hy_w_in, hy_b_in, hy_conv_w, hy_conv_b, hy_f_w1, hy_f_b1, hy_f_w2, hy_f_b2,
              hy_f_w3, hy_f_b3, hy_f_freq, hy_f_w4, hy_f_bias, hy_w_out, hy_b_out,
              ln1_g, ln1_b, ln2_g, ln2_b, router_w, router_b, moe_w1, moe_w3, moe_w2):
    rope = axial_rope_tables(x.shape[1])
    attn_layers = [i for i in range(DEPTH) if i % N_MIXERS == 0]
    last_attn = attn_layers[-1]
    h, hc = x, ctx
    for i in range(DEPTH):
        mixer = i % N_MIXERS
        j = i // N_MIXERS
        ctx_out = i < last_attn
        ctx_used = ctx_out or mixer == 0
        sh1, sc1, g1, sh2, sc2, g2 = [m[:, None, :] for m in ada_modulation(c, ada_w[i], ada_b[i])]
        u = h * (1.0 + sc1) + sh1
        if ctx_used:
            csh1, csc1, cg1, csh2, csc2, cg2 = ada_modulation(c_ctx, ada_w[i], ada_b[i])
            uc = hc * (1.0 + csc1) + csh1
        if mixer == 0:
            y, yc = attention_mixer(u, uc, attn_w_in[j], attn_b_in[j], attn_sink[j], attn_w_out[j],
                                    rope, ctx_out)
        else:
            hp = (hy_w_in[j], hy_b_in[j], hy_conv_w[j], hy_conv_b[j], hy_f_w1[j], hy_f_b1[j],
                  hy_f_w2[j], hy_f_b2[j], hy_f_w3[j], hy_f_b3[j], hy_f_freq[j], hy_f_w4[j],
                  hy_f_bias[j], hy_w_out[j], hy_b_out[j])
            y = hyena_mixer(u, hp)
            yc = hyena_mixer(uc, hp) if ctx_out else None
        h = layer_norm(DEEPNORM_ALPHA * h + g1 * y, ln1_g[i], ln1_b[i])
        f = grouped_moe(h * (1.0 + sc2) + sh2, router_w, router_b, moe_w1[i], moe_w3[i], moe_w2[i])
        h = layer_norm(DEEPNORM_ALPHA * h + g2 * f, ln2_g[i], ln2_b[i])
        if ctx_out:
            hc = layer_norm(DEEPNORM_ALPHA * hc + cg1 * yc, ln1_g[i], ln1_b[i])
            fc = grouped_moe(hc * (1.0 + csc2) + csh2, router_w, router_b, moe_w1[i], moe_w3[i], moe_w2[i])
            hc = layer_norm(DEEPNORM_ALPHA * hc + cg2 * fc, ln2_g[i], ln2_b[i])
    return h
```

```python
import functools
import math

import numpy as np
import jax
import jax.numpy as jnp
from jax import lax
from jax.experimental import pallas as pl
from jax.experimental.pallas import tpu as pltpu

F32 = jnp.float32
BF16 = jnp.bfloat16
I32 = jnp.int32
U32 = jnp.uint32
HIGHEST = lax.Precision.HIGHEST

LANES = 128
V7X_VMEM_LIMIT_BYTES = 56 * 1024 * 1024

GRID_W = 64
BLOCK = 128
ROPE_BASE = 10000.0
NEG_INF = -1e30
HY_BANDS = 16
HY_FAST_DECAY = 0.3
HY_SLOW_DECAY = 1.5
HY_TARGET = 1e-2
N_GROUPS = 4
EXPERTS_PER_GROUP = 4
N_PAIRS = 6
N_BUCKETS = N_GROUPS * N_PAIRS
BUCKET_ROWS = 32
LN_EPS = 1e-5
FFT_N2 = 128

TOKEN_TILE = 512


def _cparams(*sem):
    return pltpu.CompilerParams(dimension_semantics=sem, vmem_limit_bytes=V7X_VMEM_LIMIT_BYTES)


def _resident(block_shape, index_map):
    return pl.BlockSpec(block_shape, index_map, pipeline_mode=pl.Buffered(1))


def _silu(x):
    return x * jax.nn.sigmoid(x)


def _ada_kernel(c_ref, w_ref, b_ref, o_ref):
    c = _silu(c_ref[...])
    o_ref[...] = jnp.dot(c, w_ref[...], preferred_element_type=F32, precision=HIGHEST) + b_ref[...]


def _ada_call(cond, ada_w, ada_b):
    depth, d, n6 = ada_w.shape
    tn = 1024
    rows = cond.shape[0]
    return pl.pallas_call(
        _ada_kernel,
        out_shape=jax.ShapeDtypeStruct((depth, rows, n6), F32),
        grid=(depth, n6 // tn),
        in_specs=[
            pl.BlockSpec((rows, d), lambda l, j: (0, 0)),
            pl.BlockSpec((None, d, tn), lambda l, j: (l, 0, j)),
            pl.BlockSpec((None, 1, tn), lambda l, j: (l, 0, j)),
        ],
        out_specs=pl.BlockSpec((None, rows, tn), lambda l, j: (l, 0, j)),
        compiler_params=_cparams("arbitrary", "arbitrary"),
        name="ada_mod",
    )(cond, ada_w, ada_b.reshape(depth, 1, n6))


def _qkv_kernel(x_ref, sc_ref, sh_ref, w_ref, b_ref, cos_ref, sa_ref, sb_ref, q_ref, k_ref, v_ref, *, scale):
    u = (x_ref[...] * (1.0 + sc_ref[...]) + sh_ref[...]).astype(BF16)
    p = jnp.dot(u, w_ref[...], preferred_element_type=F32) + b_ref[...]
    cos, sa, sb = cos_ref[...], sa_ref[...], sb_ref[...]
    nq = q_ref.shape[1]
    nk = k_ref.shape[1]

    def rope(xc):
        return xc * cos + pltpu.roll(xc, 16, 1) * sa + pltpu.roll(xc, LANES - 16, 1) * sb

    for c in range(nq // LANES):
        q_ref[:, c * LANES:(c + 1) * LANES] = (rope(p[:, c * LANES:(c + 1) * LANES]) * scale).astype(BF16)
    for c in range(nk // LANES):
        o = nq + c * LANES
        k_ref[:, c * LANES:(c + 1) * LANES] = rope(p[:, o:o + LANES]).astype(BF16)
    v_ref[...] = p[:, nq + nk:].astype(BF16)


def _qkv_call(x, sc, sh, w_ext, b_ext, cos_t, sa_t, sb_t, n_q, n_kd, head_dim):
    bsz, s, d = x.shape
    tm = min(TOKEN_TILE, s)
    n_out = w_ext.shape[1]
    kern = functools.partial(_qkv_kernel, scale=head_dim ** -0.5)
    return pl.pallas_call(
        kern,
        out_shape=(jax.ShapeDtypeStruct((bsz, s, n_q), BF16),
                   jax.ShapeDtypeStruct((bsz, s, n_kd), BF16),
                   jax.ShapeDtypeStruct((bsz, s, n_kd), BF16)),
        grid=(bsz, s // tm),
        in_specs=[
            pl.BlockSpec((None, tm, d), lambda b, i: (b, i, 0)),
            pl.BlockSpec((None, 1, d), lambda b, i: (b, 0, 0)),
            pl.BlockSpec((None, 1, d), lambda b, i: (b, 0, 0)),
            _resident((d, n_out), lambda b, i: (0, 0)),
            pl.BlockSpec((1, n_out), lambda b, i: (0, 0)),
            pl.BlockSpec((tm, LANES), lambda b, i: (i, 0)),
            pl.BlockSpec((tm, LANES), lambda b, i: (i, 0)),
            pl.BlockSpec((tm, LANES), lambda b, i: (i, 0)),
        ],
        out_specs=(pl.BlockSpec((None, tm, n_q), lambda b, i: (b, i, 0)),
                   pl.BlockSpec((None, tm, n_kd), lambda b, i: (b, i, 0)),
                   pl.BlockSpec((None, tm, n_kd), lambda b, i: (b, i, 0))),
        compiler_params=_cparams("arbitrary", "arbitrary"),
        name="attn_qkv",
    )(x, sc, sh, w_ext, b_ext, cos_t, sa_t, sb_t)


def _ctx_kv_kernel(x_ref, sc_ref, sh_ref, w_ref, b_ref, k_ref, v_ref):
    u = (x_ref[...] * (1.0 + sc_ref[...]) + sh_ref[...]).astype(BF16)
    p = jnp.dot(u, w_ref[...], preferred_element_type=F32) + b_ref[...]
    nk = k_ref.shape[1]
    k_ref[...] = p[:, :nk].astype(BF16)
    v_ref[...] = p[:, nk:].astype(BF16)


def _ctx_kv_call(ctx, csc, csh, w_kv, b_kv, n_kd):
    bsz, c, d = ctx.shape
    return pl.pallas_call(
        _ctx_kv_kernel,
        out_shape=(jax.ShapeDtypeStruct((bsz, c, n_kd), BF16), jax.ShapeDtypeStruct((bsz, c, n_kd), BF16)),
        grid=(bsz,),
        in_specs=[
            pl.BlockSpec((None, c, d), lambda b: (b, 0, 0)),
            pl.BlockSpec((1, d), lambda b: (0, 0)),
            pl.BlockSpec((1, d), lambda b: (0, 0)),
            pl.BlockSpec((d, 2 * n_kd), lambda b: (0, 0)),
            pl.BlockSpec((1, 2 * n_kd), lambda b: (0, 0)),
        ],
        out_specs=(pl.BlockSpec((None, c, n_kd), lambda b: (b, 0, 0)),
                   pl.BlockSpec((None, c, n_kd), lambda b: (b, 0, 0))),
        compiler_params=_cparams("arbitrary"),
        name="attn_ctx_kv",
    )(ctx, csc, csh, w_kv, b_kv)


def _attn_kernel(sink_ref, q_ref, kp_ref, kc_ref, kn_ref, vp_ref, vc_ref, vn_ref, kx_ref, vx_ref, o_ref,
                 *, n_kv, group, nb):
    n = pl.program_id(1)
    r = lax.broadcasted_iota(I32, (BLOCK, BLOCK), 0)
    j = lax.broadcasted_iota(I32, (BLOCK, BLOCK), 1)
    prev_ok = (j >= r) & (n > 0)
    next_ok = (j <= r) & (n < nb - 1)
    lo = lax.broadcasted_iota(I32, (1, LANES), 1) < (LANES // 2)
    pairs = group // 2
    for kh in range(n_kv):
        sl = slice(kh * LANES, (kh + 1) * LANES)
        kcat = jnp.concatenate([kp_ref[:, sl], kc_ref[:, sl], kn_ref[:, sl], kx_ref[:, sl]], axis=0)
        vcat = jnp.concatenate([vp_ref[:, sl], vc_ref[:, sl], vn_ref[:, sl], vx_ref[:, sl]], axis=0)
        zk = jnp.zeros_like(kcat)
        k_half = (jnp.where(lo, kcat, zk), jnp.where(lo, zk, kcat))
        v_half = (jnp.where(lo, vcat, zk), jnp.where(lo, zk, vcat))
        for pp in range(pairs):
            p = kh * pairs + pp
            q2 = q_ref[:, p * LANES:(p + 1) * LANES]
            acc = None
            for side in range(2):
                sk = sink_ref[2 * p + side]
                s = lax.dot_general(q2, k_half[side], (((1,), (1,)), ((), ())), preferred_element_type=F32)
                s = jnp.concatenate([
                    jnp.where(prev_ok, s[:, :BLOCK], NEG_INF),
                    s[:, BLOCK:2 * BLOCK],
                    jnp.where(next_ok, s[:, 2 * BLOCK:3 * BLOCK], NEG_INF),
                    s[:, 3 * BLOCK:]], axis=1)
                m = jnp.maximum(jnp.max(s, axis=1, keepdims=True), sk)
                e = jnp.exp(s - m)
                den = jnp.sum(e, axis=1, keepdims=True) + jnp.exp(sk - m)
                o = jnp.dot(e.astype(BF16), v_half[side], preferred_element_type=F32) * (1.0 / den)
                acc = o if acc is None else acc + o
            o_ref[:, p * LANES:(p + 1) * LANES] = acc.astype(BF16)


def _attn_call(sink, q, kd, vd, kxd, vxd, n_kv, group):
    bsz, s, n_q = q.shape
    n_kd = kd.shape[2]
    c = kxd.shape[1]
    nb = s // BLOCK
    kern = functools.partial(_attn_kernel, n_kv=n_kv, group=group, nb=nb)
    prev = lambda b, n: (b, jnp.maximum(n - 1, 0), 0)
    cur = lambda b, n: (b, n, 0)
    nxt = lambda b, n: (b, jnp.minimum(n + 1, nb - 1), 0)
    kv = lambda im: pl.BlockSpec((None, BLOCK, n_kd), im)
    return pl.pallas_call(
        kern,
        out_shape=jax.ShapeDtypeStruct((bsz, s, n_q), BF16),
        grid=(bsz, nb),
        in_specs=[
            pl.BlockSpec(memory_space=pltpu.SMEM),
            pl.BlockSpec((None, BLOCK, n_q), cur),
            kv(prev), kv(cur), kv(nxt), kv(prev), kv(cur), kv(nxt),
            pl.BlockSpec((None, c, n_kd), lambda b, n: (b, 0, 0)),
            pl.BlockSpec((None, c, n_kd), lambda b, n: (b, 0, 0)),
        ],
        out_specs=pl.BlockSpec((None, BLOCK, n_q), cur),
        compiler_params=_cparams("arbitrary", "arbitrary"),
        name="window_attn",
    )(sink, q, kd, kd, kd, vd, vd, vd, kxd, vxd)


def _layer_norm(r, g, b):
    mu = jnp.mean(r, axis=-1, keepdims=True)
    xc = r - mu
    var = jnp.mean(xc * xc, axis=-1, keepdims=True)
    return xc * lax.rsqrt(var + LN_EPS) * g + b


def _route(logits_t, rb):
    s = jax.nn.sigmoid(logits_t)
    sel = s + rb
    n_e = N_GROUPS * EXPERTS_PER_GROUP
    sel_r = [sel[e:e + 1, :] for e in range(n_e)]
    s_r = [s[e:e + 1, :] for e in range(n_e)]
    gscore = []
    for g in range(N_GROUPS):
        a, b, c, d = sel_r[4 * g:4 * g + 4]
        m1, n1, m2, n2 = jnp.maximum(a, b), jnp.minimum(a, b), jnp.maximum(c, d), jnp.minimum(c, d)
        gscore.append(jnp.maximum(m1, m2) + jnp.maximum(jnp.minimum(m1, m2), jnp.maximum(n1, n2)))
    best, gi = gscore[0], jnp.zeros_like(gscore[0], dtype=I32)
    for g in range(1, N_GROUPS):
        upd = gscore[g] > best
        gi = jnp.where(upd, g, gi)
        best = jnp.where(upd, gscore[g], best)

    def pick(rows, i):
        out = rows[i]
        for g in range(1, N_GROUPS):
            out = jnp.where(gi == g, rows[4 * g + i], out)
        return out

    v = [pick(sel_r, i) for i in range(EXPERTS_PER_GROUP)]
    sv = [pick(s_r, i) for i in range(EXPERTS_PER_GROUP)]

    def argmax4(vals):
        bv, bi = vals[0], jnp.zeros_like(gi)
        for i in range(1, EXPERTS_PER_GROUP):
            upd = vals[i] > bv
            bi = jnp.where(upd, i, bi)
            bv = jnp.where(upd, vals[i], bv)
        return bi

    def take4(vals, idx):
        out = vals[0]
        for i in range(1, EXPERTS_PER_GROUP):
            out = jnp.where(idx == i, vals[i], out)
        return out

    i1 = argmax4(v)
    i2 = argmax4([jnp.where(i1 == i, -jnp.inf, v[i]) for i in range(EXPERTS_PER_GROUP)])
    s1, s2 = take4(sv, i1), take4(sv, i2)
    tot = s1 + s2
    g1, g2 = s1 / tot, s2 / tot
    first_lo = i1 < i2
    i_lo, i_hi = jnp.minimum(i1, i2), jnp.maximum(i1, i2)
    g_lo, g_hi = jnp.where(first_lo, g1, g2), jnp.where(first_lo, g2, g1)
    pair = jnp.where(i_lo == 0, i_hi - 1, jnp.where(i_lo == 1, i_hi + 1, N_PAIRS - 1))
    bucket = gi * N_PAIRS + pair
    return 4 * gi + i_lo, 4 * gi + i_hi, g_lo, g_hi, bucket


def _proj_ln_kernel(a_ref, w_ref, bias_ref, h_ref, gate_ref, lng_ref, lnb_ref, sc_ref, sh_ref, rw_ref, rb_ref,
                    h1_ref, xp_ref, meta_ref, cnt_ref, run_ref, *, alpha):
    first = (pl.program_id(0) == 0) & (pl.program_id(1) == 0)

    @pl.when(first)
    def _():
        run_ref[...] = jnp.zeros_like(run_ref)

    tm, d = h_ref.shape
    half = d // 2
    y = jnp.dot(a_ref[...], w_ref[...], preferred_element_type=F32) + bias_ref[...]
    h1 = _layer_norm(alpha * h_ref[...] + gate_ref[...] * y, lng_ref[...], lnb_ref[...])
    h1_ref[...] = h1
    tb = (h1 * (1.0 + sc_ref[...]) + sh_ref[...]).astype(BF16)

    bits = lax.bitcast_convert_type(tb.astype(F32), U32)
    xp_ref[:, :half] = (bits[:, :half] & jnp.uint32(0xFFFF0000)) | (bits[:, half:] >> 16)

    logits = jnp.dot(tb, rw_ref[...], preferred_element_type=F32)
    logits_t = jnp.transpose(logits)[:N_GROUPS * EXPERTS_PER_GROUP, :]
    e_lo, e_hi, g_lo, g_hi, bucket = _route(logits_t, rb_ref[...])

    rows = lax.broadcasted_iota(I32, (BUCKET_ROWS, tm), 0)
    onehot = (rows == bucket).astype(F32)
    tri = (lax.broadcasted_iota(I32, (tm, tm), 0) <= lax.broadcasted_iota(I32, (tm, tm), 1)).astype(BF16)
    cum = jnp.dot(onehot.astype(BF16), tri, preferred_element_type=F32)
    run = run_ref[:, 0:1]
    rank = jnp.sum(onehot * (cum - 1.0 + run), axis=0, keepdims=True)
    new_run = run + cum[:, tm - 1:tm]
    run_ref[...] = jnp.broadcast_to(new_run, run_ref.shape)
    cnt_ref[...] = jnp.broadcast_to(new_run, cnt_ref.shape)

    mrow = lax.broadcasted_iota(I32, (8, tm), 0)
    meta = jnp.where(mrow == 0, e_lo.astype(F32), 0.0)
    meta = jnp.where(mrow == 1, e_hi.astype(F32), meta)
    meta = jnp.where(mrow == 2, g_lo, meta)
    meta = jnp.where(mrow == 3, g_hi, meta)
    meta = jnp.where(mrow == 4, bucket.astype(F32), meta)
    meta = jnp.where(mrow == 5, rank, meta)
    meta_ref[...] = meta

    grow = lax.broadcasted_iota(I32, (LANES, tm), 0)
    gates_t = jnp.where(grow == 0, g_lo, jnp.where(grow == 1, g_hi, 0.0))
    xp_ref[:, half:] = lax.bitcast_convert_type(jnp.transpose(gates_t), U32)


def _proj_ln_call(a, w, bias, h, gate, lng, lnb, sc, sh, rw_pad, rb, alpha):
    bsz, s, d = h.shape
    dm = a.shape[2]
    tm = min(TOKEN_TILE, s)
    t = bsz * s
    nt = s // tm
    dp = d // 2 + LANES
    kern = functools.partial(_proj_ln_kernel, alpha=alpha)
    vec = lambda: pl.BlockSpec((1, d), lambda b, i: (0, 0))
    bvec = lambda: pl.BlockSpec((None, 1, d), lambda b, i: (b, 0, 0))
    return pl.pallas_call(
        kern,
        out_shape=(jax.ShapeDtypeStruct((bsz, s, d), F32),
                   jax.ShapeDtypeStruct((t, dp), U32),
                   jax.ShapeDtypeStruct((8, t), F32),
                   jax.ShapeDtypeStruct((BUCKET_ROWS, LANES), F32)),
        grid=(bsz, nt),
        in_specs=[
            pl.BlockSpec((None, tm, dm), lambda b, i: (b, i, 0)),
            _resident((dm, d), lambda b, i: (0, 0)),
            vec(),
            pl.BlockSpec((None, tm, d), lambda b, i: (b, i, 0)),
            bvec(), vec(), vec(), bvec(), bvec(),
            pl.BlockSpec((d, LANES), lambda b, i: (0, 0)),
            pl.BlockSpec((N_GROUPS * EXPERTS_PER_GROUP, 1), lambda b, i: (0, 0)),
        ],
        out_specs=(pl.BlockSpec((None, tm, d), lambda b, i: (b, i, 0)),
                   pl.BlockSpec((tm, dp), lambda b, i: (b * nt + i, 0)),
                   pl.BlockSpec((8, tm), lambda b, i: (0, b * nt + i)),
                   pl.BlockSpec((BUCKET_ROWS, LANES), lambda b, i: (0, 0))),
        scratch_shapes=[pltpu.VMEM((BUCKET_ROWS, LANES), F32)],
        compiler_params=_cparams("arbitrary", "arbitrary"),
        name="proj_ln_route",
    )(a, w, bias, h, gate, lng, lnb, sc, sh, rw_pad, rb)


def _wait_rows(src_row, dst_row, sem, n):
    def wait(r, c):
        pltpu.make_async_copy(src_row, dst_row, sem).wait()
        return c

    lax.fori_loop(0, n, wait, 0, unroll=8)


def _scatter_kernel(dest_ref, xp_ref, init_ref, xs_ref, sem):
    del init_ref
    tm = xp_ref.shape[0]
    base = pl.program_id(0) * tm

    def issue(r, c):
        pltpu.make_async_copy(xp_ref.at[pl.ds(r, 1)], xs_ref.at[pl.ds(dest_ref[base + r], 1)], sem).start()
        return c

    lax.fori_loop(0, tm, issue, 0, unroll=8)
    _wait_rows(xp_ref.at[pl.ds(0, 1)], xs_ref.at[pl.ds(0, 1)], sem, tm)


def _scatter_call(dest, xp, rows_out):
    t, dp = xp.shape
    tm = min(TOKEN_TILE, t)
    init = jnp.zeros((rows_out, dp), U32)
    return pl.pallas_call(
        _scatter_kernel,
        out_shape=jax.ShapeDtypeStruct((rows_out, dp), U32),
        grid_spec=pltpu.PrefetchScalarGridSpec(
            num_scalar_prefetch=1,
            grid=(t // tm,),
            in_specs=[pl.BlockSpec((tm, dp), lambda i, dest: (i, 0)),
                      pl.BlockSpec(memory_space=pl.ANY)],
            out_specs=pl.BlockSpec(memory_space=pl.ANY),
            scratch_shapes=[pltpu.SemaphoreType.DMA],
        ),
        input_output_aliases={2: 0},
        compiler_params=_cparams("arbitrary"),
        name="moe_scatter",
    )(dest, xp, init)


def _moe_kernel(ex_ref, valid_ref, xs_ref, w1_ref, w3_ref, w2_ref, y_ref):
    del ex_ref
    i = pl.program_id(0)
    s = pl.program_id(1)
    half = xs_ref.shape[1] - LANES
    ok = valid_ref[i] > 0

    @pl.when(ok)
    def _():
        w = xs_ref[:, :half]
        hi = lax.bitcast_convert_type(w & jnp.uint32(0xFFFF0000), F32).astype(BF16)
        lo = lax.bitcast_convert_type(w << 16, F32).astype(BF16)
        x = jnp.concatenate([hi, lo], axis=1)
        gates = lax.bitcast_convert_type(xs_ref[:, half:], F32)
        which = (s + i) % 2
        g = jnp.where(which == 0, gates[:, 0:1], gates[:, 1:2])
        a = jnp.dot(x, w1_ref[...], preferred_element_type=F32)
        b = jnp.dot(x, w3_ref[...], preferred_element_type=F32)
        hid = (_silu(a) * b * g).astype(BF16)
        y = jnp.dot(hid, w2_ref[...], preferred_element_type=F32)

        @pl.when(s == 0)
        def _():
            y_ref[...] = y

        @pl.when(s != 0)
        def _():
            y_ref[...] += y

    @pl.when(jnp.logical_not(ok) & (s == 0))
    def _():
        y_ref[...] = jnp.zeros_like(y_ref)


def _moe_call(ex, valid, xs, w1, w3, w2):
    rows, dp = xs.shape
    n_e, d, ff = w1.shape
    tm = min(TOKEN_TILE, rows)
    ntiles = rows // tm
    return pl.pallas_call(
        _moe_kernel,
        out_shape=jax.ShapeDtypeStruct((rows, d), F32),
        grid_spec=pltpu.PrefetchScalarGridSpec(
            num_scalar_prefetch=2,
            grid=(ntiles, 2),
            in_specs=[
                pl.BlockSpec((tm, dp), lambda i, s, ex, va: (i, 0)),
                pl.BlockSpec((None, d, ff), lambda i, s, ex, va: (ex[2 * i + s], 0, 0)),
                pl.BlockSpec((None, d, ff), lambda i, s, ex, va: (ex[2 * i + s], 0, 0)),
                pl.BlockSpec((None, ff, d), lambda i, s, ex, va: (ex[2 * i + s], 0, 0)),
            ],
            out_specs=pl.BlockSpec((tm, d), lambda i, s, ex, va: (i, 0)),
        ),
        compiler_params=_cparams("arbitrary", "arbitrary"),
        name="moe_experts",
    )(ex, valid, xs, w1, w3, w2)


def _gather_ln_kernel(dest_ref, ys_ref, h_ref, gate_ref, lng_ref, lnb_ref, o_ref, buf, sem, *, alpha, nt):
    tm = h_ref.shape[0]
    base = (pl.program_id(0) * nt + pl.program_id(1)) * tm

    def issue(r, c):
        pltpu.make_async_copy(ys_ref.at[pl.ds(dest_ref[base + r], 1)], buf.at[pl.ds(r, 1)], sem).start()
        return c

    lax.fori_loop(0, tm, issue, 0, unroll=8)
    _wait_rows(ys_ref.at[pl.ds(0, 1)], buf.at[pl.ds(0, 1)], sem, tm)
    o_ref[...] = _layer_norm(alpha * h_ref[...] + gate_ref[...] * buf[...], lng_ref[...], lnb_ref[...])


def _gather_ln_call(dest, ys, h, gate, lng, lnb, alpha):
    bsz, s, d = h.shape
    tm = min(TOKEN_TILE, s)
    nt = s // tm
    kern = functools.partial(_gather_ln_kernel, alpha=alpha, nt=nt)
    return pl.pallas_call(
        kern,
        out_shape=jax.ShapeDtypeStruct((bsz, s, d), F32),
        grid_spec=pltpu.PrefetchScalarGridSpec(
            num_scalar_prefetch=1,
            grid=(bsz, nt),
            in_specs=[
                pl.BlockSpec(memory_space=pl.ANY),
                pl.BlockSpec((None, tm, d), lambda b, i, dest: (b, i, 0)),
                pl.BlockSpec((None, 1, d), lambda b, i, dest: (b, 0, 0)),
                pl.BlockSpec((1, d), lambda b, i, dest: (0, 0)),
                pl.BlockSpec((1, d), lambda b, i, dest: (0, 0)),
            ],
            out_specs=pl.BlockSpec((None, tm, d), lambda b, i, dest: (b, i, 0)),
            scratch_shapes=[pltpu.VMEM((tm, d), F32), pltpu.SemaphoreType.DMA],
        ),
        compiler_params=_cparams("arbitrary", "arbitrary"),
        name="moe_gather_ln",
    )(dest, ys, h, gate, lng, lnb)


_PAIR_LO = (0, 0, 0, 1, 1, 2)
_PAIR_HI = (1, 2, 3, 2, 3, 3)


def _moe_layer(xp, meta, cnt, h1, gate2, lng, lnb, w1, w3, w2, alpha):
    t = xp.shape[0]
    tm = min(TOKEN_TILE, t)
    ntiles = t // tm + N_BUCKETS
    counts = cnt[:N_BUCKETS, 0].astype(I32)
    tiles_b = (counts + tm - 1) // tm
    tile_end = jnp.cumsum(tiles_b)
    offs = (tile_end - tiles_b) * tm
    bucket = meta[4].astype(I32)
    dest = offs[bucket] + meta[5].astype(I32)

    tile = jnp.arange(ntiles, dtype=I32)
    valid = (tile < tile_end[-1]).astype(I32)
    tb = jnp.minimum(jnp.sum((tile[:, None] >= tile_end[None, :]).astype(I32), axis=1), N_BUCKETS - 1)
    lo = jnp.asarray(_PAIR_LO, I32)[tb % N_PAIRS] + EXPERTS_PER_GROUP * (tb // N_PAIRS)
    hi = jnp.asarray(_PAIR_HI, I32)[tb % N_PAIRS] + EXPERTS_PER_GROUP * (tb // N_PAIRS)
    odd = (tile % 2) == 1
    ex = jnp.stack([jnp.where(odd, hi, lo), jnp.where(odd, lo, hi)], axis=1).reshape(-1)

    xs = _scatter_call(dest, xp, ntiles * tm)
    ys = _moe_call(ex, valid, xs, w1, w3, w2)
    return _gather_ln_call(dest, ys, h1, gate2, lng, lnb, alpha)


def _hy_in_kernel(xm_ref, xp_ref, xn_ref, sc_ref, sh_ref, w_ref, b_ref, cw_ref, cb_ref, x0_ref, z_ref, *, nt, tn):
    i = pl.program_id(1)
    tm, d = xm_ref.shape
    sc, sh = 1.0 + sc_ref[...], sh_ref[...]
    u = jnp.concatenate([xp_ref[...] * sc + sh, xm_ref[...] * sc + sh, xn_ref[...] * sc + sh], axis=0).astype(BF16)
    rows = tm + 16
    rid = lax.broadcasted_iota(I32, (rows, 1), 0)
    keep = ((rid >= 8) | (i > 0)) & ((rid < tm + 8) | (i < nt - 1))

    def conv(sec, j):
        col = sec * d + j * tn
        p = jnp.dot(u, w_ref[:, col:col + tn], preferred_element_type=F32) + b_ref[:, col:col + tn]
        p = jnp.where(keep, p, 0.0)
        cw = cw_ref[:, col:col + tn]
        out = (cw[0:1] * pltpu.roll(p, 1, 0) + cw[1:2] * p + cw[2:3] * pltpu.roll(p, rows - 1, 0)
               + cb_ref[:, col:col + tn])
        return out[8:8 + tm]

    for j in range(d // tn):
        x0_ref[:, j * tn:(j + 1) * tn] = conv(0, j).astype(BF16)
        z_ref[:, j * tn:(j + 1) * tn] = (conv(1, j) * conv(2, j)).astype(BF16)


def _hy_in_call(h, sc, sh, w, b, cw, cb):
    bsz, s, d = h.shape
    tm = min(TOKEN_TILE, s)
    nt = s // tm
    hb = tm // 8
    tn = 512
    kern = functools.partial(_hy_in_kernel, nt=nt, tn=tn)
    return pl.pallas_call(
        kern,
        out_shape=(jax.ShapeDtypeStruct((bsz, s, d), BF16), jax.ShapeDtypeStruct((bsz, s, d), BF16)),
        grid=(bsz, nt),
        in_specs=[
            pl.BlockSpec((None, tm, d), lambda b, i: (b, i, 0)),
            pl.BlockSpec((None, 8, d), lambda b, i: (b, jnp.maximum(i * hb - 1, 0), 0)),
            pl.BlockSpec((None, 8, d), lambda b, i: (b, jnp.minimum((i + 1) * hb, nt * hb - 1), 0)),
            pl.BlockSpec((None, 1, d), lambda b, i: (b, 0, 0)),
            pl.BlockSpec((None, 1, d), lambda b, i: (b, 0, 0)),
            _resident((d, 3 * d), lambda b, i: (0, 0)),
            pl.BlockSpec((1, 3 * d), lambda b, i: (0, 0)),
            pl.BlockSpec((3, 3 * d), lambda b, i: (0, 0)),
            pl.BlockSpec((1, 3 * d), lambda b, i: (0, 0)),
        ],
        out_specs=(pl.BlockSpec((None, tm, d), lambda b, i: (b, i, 0)),
                   pl.BlockSpec((None, tm, d), lambda b, i: (b, i, 0))),
        compiler_params=_cparams("arbitrary", "arbitrary"),
        name="hyena_in",
    )(h, h, h, sc, sh, w, b, cw, cb)


def _filter_kernel(w1_ref, b1_ref, w2_ref, b2_ref, w3_ref, b3_ref, fr_ref, w4_ref, dl_ref, f_ref, l1_ref,
                   *, seq, tn):
    i = pl.program_id(0)

    @pl.when(i == 0)
    def _():
        l1_ref[...] = jnp.zeros_like(l1_ref)

    n_lane = i * tn + lax.broadcasted_iota(I32, (1, tn), 1)
    m_lane = jnp.where(n_lane < seq, n_lane, 2 * seq - n_lane).astype(F32)
    t_lane = m_lane / (seq - 1.0)
    wl = (2.0 * math.pi) * m_lane / float(seq)
    band = lax.broadcasted_iota(I32, (HY_BANDS, 1), 0).astype(F32)
    fb = 1e-4 + band * ((HY_BANDS - 1 - 1e-4) / (HY_BANDS - 1))
    ang = fb * wl
    z = jnp.concatenate([t_lane, jnp.cos(ang), -jnp.sin(ang),
                         jnp.zeros((7, tn), F32)], axis=0)
    fr = fr_ref[...]

    def layer(w_ref, b_ref, x):
        pre = lax.dot_general(w_ref[...], x, (((0,), (0,)), ((), ())), preferred_element_type=F32,
                              precision=HIGHEST)
        return jnp.sin(fr * (pre + b_ref[...]))

    hdn = layer(w1_ref, b1_ref, z)
    hdn = layer(w2_ref, b2_ref, hdn)
    hdn = layer(w3_ref, b3_ref, hdn)
    h = lax.dot_general(hdn, w4_ref[...], (((0,), (0,)), ((), ())), preferred_element_type=F32,
                        precision=HIGHEST)
    n_col = i * tn + lax.broadcasted_iota(I32, (tn, 1), 0)
    m_col = jnp.where(n_col < seq, n_col, 2 * seq - n_col).astype(F32)
    h = h * jnp.exp(-(m_col / (seq - 1.0)) * dl_ref[...])
    h = jnp.where(n_col == seq, 0.0, h)
    l1_ref[...] += jnp.sum(jnp.abs(h), axis=0, keepdims=True)
    f_ref[...] = h.astype(BF16)


def _filter_call(fw1, fb1, fw2, fb2, fw3, fb3, freq, fw4, seq):
    width = fw2.shape[0]
    d = fw4.shape[1] // 2
    tn = min(512, seq)
    steps = 2 * seq // tn
    max_decay = math.log(HY_TARGET) / HY_FAST_DECAY
    min_decay = math.log(HY_TARGET) / HY_SLOW_DECAY
    deltas = jnp.abs(jnp.linspace(min_decay, max_decay, d, dtype=F32)).reshape(1, d)
    w1p = jnp.concatenate([fw1, jnp.zeros((7, width), F32)], axis=0)
    col = lambda a: a.reshape(width, 1)
    small = lambda shp: pl.BlockSpec(shp, lambda i: (0, 0))
    kern = functools.partial(_filter_kernel, seq=seq, tn=tn)
    return pl.pallas_call(
        kern,
        out_shape=(jax.ShapeDtypeStruct((2 * seq, d), BF16), jax.ShapeDtypeStruct((1, d), F32)),
        grid=(steps,),
        in_specs=[small((40, width)), small((width, 1)), small((width, width)), small((width, 1)),
                  small((width, width)), small((width, 1)), small((width, 1)),
                  pl.BlockSpec((width, d), lambda i: (0, (i * tn) // seq)),
                  small((1, d))],
        out_specs=(pl.BlockSpec((tn, d), lambda i: (i, 0)), small((1, d))),
        compiler_params=_cparams("arbitrary"),
        name="hyena_filter",
    )(w1p, col(fb1), fw2, col(fb2), fw3, col(fb3), col(freq), fw4, deltas)


def _dft_tables(n1, n2):
    n = n1 * n2
    k = np.arange(n1)[:, None]
    m = np.arange(n1)[None, :]
    ang1 = -2.0 * np.pi * ((k * m) % n1) / n1
    f1r, f1i = np.cos(ang1), np.sin(ang1)
    hn = n1 // 2
    a_data = np.block([[f1r[:, :hn], -f1i[:, :hn]], [f1i[:, :hn], f1r[:, :hn]]])
    a_filt = np.concatenate([f1r, f1i], axis=0)
    a_inv = np.block([[f1r.T[:hn], f1i.T[:hn]], [-f1i.T[:hn], f1r.T[:hn]]]) / n
    k2 = np.arange(n2)[:, None]
    m2 = np.arange(n2)[None, :]
    ang2 = -2.0 * np.pi * ((k2 * m2) % n2) / n2
    f2r, f2i = np.cos(ang2), np.sin(ang2)
    b_fwd = np.block([[f2r, -f2i], [f2i, f2r]])
    b_inv = np.block([[f2r, f2i], [-f2i, f2r]])
    angt = -2.0 * np.pi * ((np.arange(n2)[:, None] * np.arange(n1)[None, :]) % n) / n
    tw = (np.cos(angt), np.sin(angt))
    as_bf16 = lambda a: jnp.asarray(a, F32).astype(BF16)
    return dict(a_data=as_bf16(a_data), a_filt=as_bf16(a_filt), a_inv=as_bf16(a_inv),
                b_fwd=as_bf16(b_fwd), b_inv=as_bf16(b_inv),
                tw_n2=tuple(jnp.asarray(t, F32).reshape(n2, n1, 1) for t in tw),
                tw_k1=tuple(jnp.asarray(t.T.copy(), F32).reshape(n1, n2, 1) for t in tw))


def _fft_a_kernel(xa_ref, xb_ref, a_ref, twr_ref, twi_ref, y_ref):
    n1 = a_ref.shape[0] // 2
    rhs = jnp.concatenate([xa_ref[...], xb_ref[...]], axis=0)
    y = jnp.dot(a_ref[...], rhs, preferred_element_type=F32)
    yr, yi = y[:n1], y[n1:]
    tr, ti = twr_ref[...], twi_ref[...]
    y_ref[0] = (yr * tr - yi * ti).astype(BF16)
    y_ref[1] = (yr * ti + yi * tr).astype(BF16)


def _fft_a_call(x4, a_mat, tw_n2, block_a, block_b):
    p, hn, n2, d = x4.shape
    n1 = 2 * hn
    x3 = x4.reshape(p, hn, n2 * d)
    y = pl.pallas_call(
        _fft_a_kernel,
        out_shape=jax.ShapeDtypeStruct((2, n1, n2 * d), BF16),
        grid=(n2,),
        in_specs=[
            pl.BlockSpec((None, hn, d), lambda j: (block_a, 0, j)),
            pl.BlockSpec((None, hn, d), lambda j: (block_b, 0, j)),
            pl.BlockSpec((2 * n1, n1), lambda j: (0, 0)),
            pl.BlockSpec((None, n1, 1), lambda j: (j, 0, 0)),
            pl.BlockSpec((None, n1, 1), lambda j: (j, 0, 0)),
        ],
        out_specs=pl.BlockSpec((2, n1, d), lambda j: (0, 0, j)),
        compiler_params=_cparams("arbitrary"),
        name="fft_stage_a",
    )(x3, x3, a_mat, tw_n2[0], tw_n2[1])
    return y.reshape(2, n1, n2, d)


def _fft_b_kernel(y_ref, b_ref, h_ref):
    n2 = y_ref.shape[1]
    rhs = y_ref[...].reshape(2 * n2, y_ref.shape[2])
    x = jnp.dot(b_ref[...], rhs, preferred_element_type=F32)
    h_ref[0] = x[:n2].astype(BF16)
    h_ref[1] = x[n2:].astype(BF16)


def _fft_b_call(y, b_fwd):
    _, n1, n2, d = y.shape
    return pl.pallas_call(
        _fft_b_kernel,
        out_shape=jax.ShapeDtypeStruct((2, n1, n2, d), BF16),
        grid=(n1,),
        in_specs=[pl.BlockSpec((2, None, n2, d), lambda k: (0, k, 0, 0)),
                  pl.BlockSpec((2 * n2, 2 * n2), lambda k: (0, 0))],
        out_specs=pl.BlockSpec((2, None, n2, d), lambda k: (0, k, 0, 0)),
        compiler_params=_cparams("arbitrary"),
        name="fft_filter_b",
    )(y, b_fwd)


def _fft_bc_kernel(y_ref, h_ref, bf_ref, bi_ref, twr_ref, twi_ref, g_ref):
    n2 = y_ref.shape[1]
    d = y_ref.shape[2]
    x = jnp.dot(bf_ref[...], y_ref[...].reshape(2 * n2, d), preferred_element_type=F32)
    xr, xi = x[:n2], x[n2:]
    hr, hi = h_ref[0].astype(F32), h_ref[1].astype(F32)
    z = jnp.concatenate([xr * hr - xi * hi, xr * hi + xi * hr], axis=0).astype(BF16)
    g = jnp.dot(bi_ref[...], z, preferred_element_type=F32)
    gr, gi = g[:n2], g[n2:]
    tr, ti = twr_ref[...], twi_ref[...]
    g_ref[0] = (gr * tr + gi * ti).astype(BF16)
    g_ref[1] = (gi * tr - gr * ti).astype(BF16)


def _fft_bc_call(y, hspec, b_fwd, b_inv, tw_k1):
    _, n1, n2, d = y.shape
    blk = lambda: pl.BlockSpec((2, None, n2, d), lambda k: (0, k, 0, 0))
    mat = lambda: pl.BlockSpec((2 * n2, 2 * n2), lambda k: (0, 0))
    tw = lambda: pl.BlockSpec((None, n2, 1), lambda k: (k, 0, 0))
    return pl.pallas_call(
        _fft_bc_kernel,
        out_shape=jax.ShapeDtypeStruct((2, n1, n2, d), BF16),
        grid=(n1,),
        in_specs=[blk(), blk(), mat(), mat(), tw(), tw()],
        out_specs=blk(),
        compiler_params=_cparams("arbitrary"),
        name="fft_stage_bc",
    )(y, hspec, b_fwd, b_inv, tw_k1[0], tw_k1[1])


def _fft_d_kernel(g_ref, a_ref, x0_ref, z_ref, l1_ref, fb_ref, o_ref):
    n1 = g_ref.shape[1]
    d = g_ref.shape[2]
    hn = n1 // 2
    y = jnp.dot(a_ref[...], g_ref[...].reshape(2 * n1, d), preferred_element_type=F32)
    inv_l1 = 1.0 / l1_ref[...]
    fb = fb_ref[...]
    for b in range(2):
        conv = y[b * hn:(b + 1) * hn] * inv_l1
        o_ref[b] = (x0_ref[b].astype(F32) * (conv + fb * z_ref[b].astype(F32))).astype(BF16)


def _fft_d_call(g, a_inv, x0_4, z_4, l1, fbias):
    _, n1, n2, d = g.shape
    hn = n1 // 2
    tok = lambda: pl.BlockSpec((2, hn, d), lambda j: (0, 0, j))
    flat = lambda a: a.reshape(a.shape[0], a.shape[1], n2 * d)
    out = pl.pallas_call(
        _fft_d_kernel,
        out_shape=jax.ShapeDtypeStruct((2, hn, n2 * d), BF16),
        grid=(n2,),
        in_specs=[pl.BlockSpec((2, n1, d), lambda j: (0, 0, j)),
                  pl.BlockSpec((n1, 2 * n1), lambda j: (0, 0)),
                  tok(), tok(),
                  pl.BlockSpec((1, d), lambda j: (0, 0)),
                  pl.BlockSpec((1, d), lambda j: (0, 0))],
        out_specs=tok(),
        compiler_params=_cparams("arbitrary"),
        name="fft_stage_d",
    )(flat(g), a_inv, flat(x0_4), flat(z_4), l1, fbias)
    return out.reshape(2, hn, n2, d)


def _hyena_conv(x0, z, filt, l1, fbias):
    bsz, seq, d = z.shape
    assert bsz == 2, "the two batch rows are packed as one complex signal"
    n2 = FFT_N2
    n1 = 2 * seq // n2
    tabs = _dft_tables(n1, n2)
    hn = n1 // 2
    hspec = _fft_b_call(_fft_a_call(filt.reshape(2, hn, n2, d), tabs["a_filt"], tabs["tw_n2"], 0, 1),
                        tabs["b_fwd"])
    y = _fft_a_call(z.reshape(2, hn, n2, d), tabs["a_data"], tabs["tw_n2"], 0, 1)
    g = _fft_bc_call(y, hspec, tabs["b_fwd"], tabs["b_inv"], tabs["tw_k1"])
    out = _fft_d_call(g, tabs["a_inv"], x0.reshape(2, hn, n2, d), z.reshape(2, hn, n2, d), l1, fbias)
    return out.reshape(bsz, seq, d)


def _dup_heads(w, n_kv, head_dim):
    lead = w.shape[:-1]
    w = w.reshape(lead + (n_kv, head_dim))
    return jnp.concatenate([w, w], axis=-1).reshape(lead + (2 * n_kv * head_dim,))


def _rope_tables(seq, head_dim):
    axis_dim = head_dim // 2
    rows = seq // GRID_W
    inv = ROPE_BASE ** (-jnp.arange(0, axis_dim, 2, dtype=F32) / axis_dim)
    row = jnp.repeat(jnp.arange(rows, dtype=F32), GRID_W)[:, None] * inv
    col = jnp.tile(jnp.arange(GRID_W, dtype=F32), rows)[:, None] * inv
    quarter = axis_dim // 2
    cos = jnp.concatenate([jnp.cos(row), jnp.cos(row), jnp.cos(col), jnp.cos(col)], axis=1)
    sin = jnp.concatenate([jnp.sin(row), jnp.sin(row), jnp.sin(col), jnp.sin(col)], axis=1)
    second = (np.arange(head_dim) % axis_dim) >= quarter
    reps = LANES // head_dim
    cos = jnp.tile(cos, (1, reps))
    sin = jnp.tile(sin, (1, reps))
    second = jnp.asarray(np.tile(second, reps))[None, :]
    return cos, jnp.where(second, sin, 0.0), jnp.where(second, 0.0, -sin)


def kernel(x, c, ctx, c_ctx, ada_w, ada_b, attn_w_in, attn_b_in, attn_sink, attn_w_out, hy_w_in, hy_b_in, hy_conv_w, hy_conv_b, hy_f_w1, hy_f_b1, hy_f_w2, hy_f_b2, hy_f_w3, hy_f_b3, hy_f_freq, hy_f_w4, hy_f_bias, hy_w_out, hy_b_out, ln1_g, ln1_b, ln2_g, ln2_b, router_w, router_b, moe_w1, moe_w3, moe_w2):
    bsz, seq, d = x.shape
    depth = ada_w.shape[0]
    assert depth == 2 and attn_w_in.shape[0] == 1 and hy_w_in.shape[0] == 1
    alpha = (2 * depth) ** 0.25
    n_heads = attn_sink.shape[1]
    attn_dim = attn_w_out.shape[1]
    head_dim = attn_dim // n_heads
    kv_dim = (attn_w_in.shape[2] - attn_dim) // 2
    n_kv = kv_dim // head_dim
    group = n_heads // n_kv
    assert head_dim * 2 == LANES and group % 2 == 0
    n_exp = router_w.shape[1]
    assert n_exp == N_GROUPS * EXPERTS_PER_GROUP

    cond = jnp.concatenate([c, c_ctx[None, :], jnp.zeros((8 - bsz - 1, d), F32)], axis=0)
    mods = _ada_call(cond, ada_w, ada_b).reshape(depth, 8, 6, d)
    mod = lambda layer, k: mods[layer, :bsz, k].reshape(bsz, 1, d)
    cmod = lambda layer, k: mods[layer, bsz, k].reshape(1, d)
    row = lambda v: v.reshape(1, -1)

    rw_pad = jnp.concatenate([router_w, jnp.zeros((d, LANES - n_exp), F32)], axis=1).astype(BF16)
    rb = router_b.reshape(n_exp, 1)
    w1b, w3b, w2b = moe_w1.astype(BF16), moe_w3.astype(BF16), moe_w2.astype(BF16)

    w_in, b_in = attn_w_in[0], attn_b_in[0]
    wq, wk, wv = w_in[:, :attn_dim], w_in[:, attn_dim:attn_dim + kv_dim], w_in[:, attn_dim + kv_dim:]
    bq, bk, bv = b_in[:attn_dim], b_in[attn_dim:attn_dim + kv_dim], b_in[attn_dim + kv_dim:]
    dup = lambda a: _dup_heads(a, n_kv, head_dim)
    w_kv = jnp.concatenate([dup(wk), dup(wv)], axis=1).astype(BF16)
    b_kv = row(jnp.concatenate([dup(bk), dup(bv)]))
    w_ext = jnp.concatenate([wq.astype(BF16), w_kv], axis=1)
    b_ext = jnp.concatenate([row(bq), b_kv], axis=1)
    n_kd = 2 * kv_dim
    cos_t, sa_t, sb_t = _rope_tables(seq, head_dim)

    q, kd, vd = _qkv_call(x, mod(0, 1), mod(0, 0), w_ext, b_ext, cos_t, sa_t, sb_t, attn_dim, n_kd, head_dim)
    kxd, vxd = _ctx_kv_call(ctx, cmod(0, 1), cmod(0, 0), w_kv, b_kv, n_kd)
    att = _attn_call(attn_sink[0], q, kd, vd, kxd, vxd, n_kv, group)
    h1, xp, meta, cnt = _proj_ln_call(att, attn_w_out[0].astype(BF16), jnp.zeros((1, d), F32), x, mod(0, 2),
                                      row(ln1_g[0]), row(ln1_b[0]), mod(0, 4), mod(0, 3), rw_pad, rb, alpha)
    h = _moe_layer(xp, meta, cnt, h1, mod(0, 5), row(ln2_g[0]), row(ln2_b[0]), w1b[0], w3b[0], w2b[0], alpha)

    x0, z = _hy_in_call(h, mod(1, 1), mod(1, 0), hy_w_in[0].astype(BF16), row(hy_b_in[0]), hy_conv_w[0],
                        row(hy_conv_b[0]))
    filt, l1 = _filter_call(hy_f_w1[0], hy_f_b1[0], hy_f_w2[0], hy_f_b2[0], hy_f_w3[0], hy_f_b3[0],
                            hy_f_freq[0], hy_f_w4[0], seq)
    yh = _hyena_conv(x0, z, filt, l1, row(hy_f_bias[0]))
    h1, xp, meta, cnt = _proj_ln_call(yh, hy_w_out[0].astype(BF16), row(hy_b_out[0]), h, mod(1, 2),
                                      row(ln1_g[1]), row(ln1_b[1]), mod(1, 4), mod(1, 3), rw_pad, rb, alpha)
    return _moe_layer(xp, meta, cnt, h1, mod(1, 5), row(ln2_g[1]), row(ln2_b[1]), w1b[1], w3b[1], w2b[1], alpha)
```

```python
import functools
import math

import numpy as np
import jax
import jax.numpy as jnp
from jax import lax
from jax.experimental import pallas as pl
from jax.experimental.pallas import tpu as pltpu

F32 = jnp.float32
BF16 = jnp.bfloat16
I32 = jnp.int32
U32 = jnp.uint32
HIGHEST = lax.Precision.HIGHEST

LANES = 128
V7X_VMEM_LIMIT_BYTES = 56 * 1024 * 1024

GRID_W = 64
BLOCK = 128
ROPE_BASE = 10000.0
NEG_INF = -1e30
HY_BANDS = 16
HY_FAST_DECAY = 0.3
HY_SLOW_DECAY = 1.5
HY_TARGET = 1e-2
N_GROUPS = 4
EXPERTS_PER_GROUP = 4
N_PAIRS = 6
N_BUCKETS = N_GROUPS * N_PAIRS
BUCKET_ROWS = 32
LN_EPS = 1e-5
FFT_N2 = 128

TOKEN_TILE = 512


def _cparams(*sem):
    return pltpu.CompilerParams(dimension_semantics=sem, vmem_limit_bytes=V7X_VMEM_LIMIT_BYTES)


def _resident(block_shape, index_map):
    return pl.BlockSpec(block_shape, index_map, pipeline_mode=pl.Buffered(1))


def _silu(x):
    return x * jax.nn.sigmoid(x)


def _ada_kernel(c_ref, w_ref, b_ref, o_ref):
    c = _silu(c_ref[...])
    o_ref[...] = jnp.dot(c, w_ref[...], preferred_element_type=F32, precision=HIGHEST) + b_ref[...]


def _ada_call(cond, ada_w, ada_b):
    depth, d, n6 = ada_w.shape
    tn = 1024
    rows = cond.shape[0]
    return pl.pallas_call(
        _ada_kernel,
        out_shape=jax.ShapeDtypeStruct((depth, rows, n6), F32),
        grid=(depth, n6 // tn),
        in_specs=[
            pl.BlockSpec((rows, d), lambda l, j: (0, 0)),
            pl.BlockSpec((None, d, tn), lambda l, j: (l, 0, j)),
            pl.BlockSpec((None, 1, tn), lambda l, j: (l, 0, j)),
        ],
        out_specs=pl.BlockSpec((None, rows, tn), lambda l, j: (l, 0, j)),
        compiler_params=_cparams("arbitrary", "arbitrary"),
        name="ada_mod",
    )(cond, ada_w, ada_b.reshape(depth, 1, n6))


def _qkv_kernel(x_ref, sc_ref, sh_ref, w_ref, b_ref, cos_ref, sa_ref, sb_ref, q_ref, k_ref, v_ref, *, scale):
    u = (x_ref[...] * (1.0 + sc_ref[...]) + sh_ref[...]).astype(BF16)
    p = jnp.dot(u, w_ref[...], preferred_element_type=F32) + b_ref[...]
    cos, sa, sb = cos_ref[...], sa_ref[...], sb_ref[...]
    nq = q_ref.shape[1]
    nk = k_ref.shape[1]

    def rope(xc):
        return xc * cos + pltpu.roll(xc, 16, 1) * sa + pltpu.roll(xc, LANES - 16, 1) * sb

    for c in range(nq // LANES):
        q_ref[:, c * LANES:(c + 1) * LANES] = (rope(p[:, c * LANES:(c + 1) * LANES]) * scale).astype(BF16)
    for c in range(nk // LANES):
        o = nq + c * LANES
        k_ref[:, c * LANES:(c + 1) * LANES] = rope(p[:, o:o + LANES]).astype(BF16)
    v_ref[...] = p[:, nq + nk:].astype(BF16)


def _qkv_call(x, sc, sh, w_ext, b_ext, cos_t, sa_t, sb_t, n_q, n_kd, head_dim):
    bsz, s, d = x.shape
    tm = min(TOKEN_TILE, s)
    n_out = w_ext.shape[1]
    kern = functools.partial(_qkv_kernel, scale=head_dim ** -0.5)
    return pl.pallas_call(
        kern,
        out_shape=(jax.ShapeDtypeStruct((bsz, s, n_q), BF16),
                   jax.ShapeDtypeStruct((bsz, s, n_kd), BF16),
                   jax.ShapeDtypeStruct((bsz, s, n_kd), BF16)),
        grid=(bsz, s // tm),
        in_specs=[
            pl.BlockSpec((None, tm, d), lambda b, i: (b, i, 0)),
            pl.BlockSpec((None, 1, d), lambda b, i: (b, 0, 0)),
            pl.BlockSpec((None, 1, d), lambda b, i: (b, 0, 0)),
            _resident((d, n_out), lambda b, i: (0, 0)),
            pl.BlockSpec((1, n_out), lambda b, i: (0, 0)),
            pl.BlockSpec((tm, LANES), lambda b, i: (i, 0)),
            pl.BlockSpec((tm, LANES), lambda b, i: (i, 0)),
            pl.BlockSpec((tm, LANES), lambda b, i: (i, 0)),
        ],
        out_specs=(pl.BlockSpec((None, tm, n_q), lambda b, i: (b, i, 0)),
                   pl.BlockSpec((None, tm, n_kd), lambda b, i: (b, i, 0)),
                   pl.BlockSpec((None, tm, n_kd), lambda b, i: (b, i, 0))),
        compiler_params=_cparams("arbitrary", "arbitrary"),
        name="attn_qkv",
    )(x, sc, sh, w_ext, b_ext, cos_t, sa_t, sb_t)


def _ctx_kv_kernel(x_ref, sc_ref, sh_ref, w_ref, b_ref, k_ref, v_ref):
    u = (x_ref[...] * (1.0 + sc_ref[...]) + sh_ref[...]).astype(BF16)
    p = jnp.dot(u, w_ref[...], preferred_element_type=F32) + b_ref[...]
    nk = k_ref.shape[1]
    k_ref[...] = p[:, :nk].astype(BF16)
    v_ref[...] = p[:, nk:].astype(BF16)


def _ctx_kv_call(ctx, csc, csh, w_kv, b_kv, n_kd):
    bsz, c, d = ctx.shape
    return pl.pallas_call(
        _ctx_kv_kernel,
        out_shape=(jax.ShapeDtypeStruct((bsz, c, n_kd), BF16), jax.ShapeDtypeStruct((bsz, c, n_kd), BF16)),
        grid=(bsz,),
        in_specs=[
            pl.BlockSpec((None, c, d), lambda b: (b, 0, 0)),
            pl.BlockSpec((1, d), lambda b: (0, 0)),
            pl.BlockSpec((1, d), lambda b: (0, 0)),
            pl.BlockSpec((d, 2 * n_kd), lambda b: (0, 0)),
            pl.BlockSpec((1, 2 * n_kd), lambda b: (0, 0)),
        ],
        out_specs=(pl.BlockSpec((None, c, n_kd), lambda b: (b, 0, 0)),
                   pl.BlockSpec((None, c, n_kd), lambda b: (b, 0, 0))),
        compiler_params=_cparams("arbitrary"),
        name="attn_ctx_kv",
    )(ctx, csc, csh, w_kv, b_kv)


def _attn_kernel(sink_ref, q_ref, kp_ref, kc_ref, kn_ref, vp_ref, vc_ref, vn_ref, kx_ref, vx_ref, o_ref,
                 *, n_kv, group, nb):
    n = pl.program_id(1)
    r = lax.broadcasted_iota(I32, (1, BLOCK, BLOCK), 1)
    j = lax.broadcasted_iota(I32, (1, BLOCK, BLOCK), 2)
    prev_ok = (j >= r) & (n > 0)
    next_ok = (j <= r) & (n < nb - 1)
    lo = lax.broadcasted_iota(I32, (1, LANES), 1) < (LANES // 2)
    pairs = group // 2
    head = lax.broadcasted_iota(I32, (group, 1, 1), 0)
    for kh in range(n_kv):
        sl = slice(kh * LANES, (kh + 1) * LANES)
        kcat = jnp.concatenate([kp_ref[:, sl], kc_ref[:, sl], kn_ref[:, sl], kx_ref[:, sl]], axis=0)
        vcat = jnp.concatenate([vp_ref[:, sl], vc_ref[:, sl], vn_ref[:, sl], vx_ref[:, sl]], axis=0)
        nkeys = kcat.shape[0]
        parts = []
        for pp in range(pairs):
            q2 = q_ref[:, (kh * pairs + pp) * LANES:(kh * pairs + pp + 1) * LANES]
            zq = jnp.zeros_like(q2)
            parts += [jnp.where(lo, q2, zq), jnp.where(lo, zq, q2)]
        qs = jnp.concatenate(parts, axis=0)
        s = lax.dot_general(qs, kcat, (((1,), (1,)), ((), ())), preferred_element_type=F32)
        s = s.reshape(group, BLOCK, nkeys)
        s = jnp.concatenate([
            jnp.where(prev_ok, s[:, :, :BLOCK], NEG_INF),
            s[:, :, BLOCK:2 * BLOCK],
            jnp.where(next_ok, s[:, :, 2 * BLOCK:3 * BLOCK], NEG_INF),
            s[:, :, 3 * BLOCK:]], axis=2)
        sk = jnp.zeros((group, 1, 1), F32)
        for g in range(group):
            sk = jnp.where(head == g, sink_ref[kh * group + g], sk)
        m = jnp.maximum(jnp.max(s, axis=2, keepdims=True), sk)
        e = jnp.exp(s - m)
        rden = 1.0 / (jnp.sum(e, axis=2, keepdims=True) + jnp.exp(sk - m))
        o = jnp.dot(e.reshape(group * BLOCK, nkeys).astype(BF16), vcat, preferred_element_type=F32)
        o = o.reshape(group, BLOCK, LANES) * rden
        for pp in range(pairs):
            p = kh * pairs + pp
            o_ref[:, p * LANES:(p + 1) * LANES] = jnp.where(lo, o[2 * pp], o[2 * pp + 1]).astype(BF16)


def _attn_call(sink, q, kd, vd, kxd, vxd, n_kv, group):
    bsz, s, n_q = q.shape
    n_kd = kd.shape[2]
    c = kxd.shape[1]
    nb = s // BLOCK
    kern = functools.partial(_attn_kernel, n_kv=n_kv, group=group, nb=nb)
    prev = lambda b, n: (b, jnp.maximum(n - 1, 0), 0)
    cur = lambda b, n: (b, n, 0)
    nxt = lambda b, n: (b, jnp.minimum(n + 1, nb - 1), 0)
    kv = lambda im: pl.BlockSpec((None, BLOCK, n_kd), im)
    return pl.pallas_call(
        kern,
        out_shape=jax.ShapeDtypeStruct((bsz, s, n_q), BF16),
        grid=(bsz, nb),
        in_specs=[
            pl.BlockSpec(memory_space=pltpu.SMEM),
            pl.BlockSpec((None, BLOCK, n_q), cur),
            kv(prev), kv(cur), kv(nxt), kv(prev), kv(cur), kv(nxt),
            pl.BlockSpec((None, c, n_kd), lambda b, n: (b, 0, 0)),
            pl.BlockSpec((None, c, n_kd), lambda b, n: (b, 0, 0)),
        ],
        out_specs=pl.BlockSpec((None, BLOCK, n_q), cur),
        compiler_params=_cparams("arbitrary", "arbitrary"),
        name="window_attn",
    )(sink, q, kd, kd, kd, vd, vd, vd, kxd, vxd)


def _layer_norm(r, g, b):
    mu = jnp.mean(r, axis=-1, keepdims=True)
    xc = r - mu
    var = jnp.mean(xc * xc, axis=-1, keepdims=True)
    return xc * lax.rsqrt(var + LN_EPS) * g + b


def _route(logits_t, rb):
    s = jax.nn.sigmoid(logits_t)
    sel = s + rb
    n_e = N_GROUPS * EXPERTS_PER_GROUP
    sel_r = [sel[e:e + 1, :] for e in range(n_e)]
    s_r = [s[e:e + 1, :] for e in range(n_e)]
    gscore = []
    for g in range(N_GROUPS):
        a, b, c, d = sel_r[4 * g:4 * g + 4]
        m1, n1, m2, n2 = jnp.maximum(a, b), jnp.minimum(a, b), jnp.maximum(c, d), jnp.minimum(c, d)
        gscore.append(jnp.maximum(m1, m2) + jnp.maximum(jnp.minimum(m1, m2), jnp.maximum(n1, n2)))
    best, gi = gscore[0], jnp.zeros_like(gscore[0], dtype=I32)
    for g in range(1, N_GROUPS):
        upd = gscore[g] > best
        gi = jnp.where(upd, g, gi)
        best = jnp.where(upd, gscore[g], best)

    def pick(rows, i):
        out = rows[i]
        for g in range(1, N_GROUPS):
            out = jnp.where(gi == g, rows[4 * g + i], out)
        return out

    v = [pick(sel_r, i) for i in range(EXPERTS_PER_GROUP)]
    sv = [pick(s_r, i) for i in range(EXPERTS_PER_GROUP)]

    def argmax4(vals):
        bv, bi = vals[0], jnp.zeros_like(gi)
        for i in range(1, EXPERTS_PER_GROUP):
            upd = vals[i] > bv
            bi = jnp.where(upd, i, bi)
            bv = jnp.where(upd, vals[i], bv)
        return bi

    def take4(vals, idx):
        out = vals[0]
        for i in range(1, EXPERTS_PER_GROUP):
            out = jnp.where(idx == i, vals[i], out)
        return out

    i1 = argmax4(v)
    i2 = argmax4([jnp.where(i1 == i, -jnp.inf, v[i]) for i in range(EXPERTS_PER_GROUP)])
    s1, s2 = take4(sv, i1), take4(sv, i2)
    tot = s1 + s2
    g1, g2 = s1 / tot, s2 / tot
    first_lo = i1 < i2
    i_lo, i_hi = jnp.minimum(i1, i2), jnp.maximum(i1, i2)
    g_lo, g_hi = jnp.where(first_lo, g1, g2), jnp.where(first_lo, g2, g1)
    pair = jnp.where(i_lo == 0, i_hi - 1, jnp.where(i_lo == 1, i_hi + 1, N_PAIRS - 1))
    bucket = gi * N_PAIRS + pair
    return 4 * gi + i_lo, 4 * gi + i_hi, g_lo, g_hi, bucket


def _proj_ln_kernel(a_ref, w_ref, bias_ref, h_ref, gate_ref, lng_ref, lnb_ref, sc_ref, sh_ref, rw_ref, rb_ref,
                    h1_ref, xp_ref, meta_ref, cnt_ref, run_ref, *, alpha):
    first = (pl.program_id(0) == 0) & (pl.program_id(1) == 0)

    @pl.when(first)
    def _():
        run_ref[...] = jnp.zeros_like(run_ref)

    tm, d = h_ref.shape
    half = d // 2
    y = jnp.dot(a_ref[...], w_ref[...], preferred_element_type=F32) + bias_ref[...]
    h1 = _layer_norm(alpha * h_ref[...] + gate_ref[...] * y, lng_ref[...], lnb_ref[...])
    h1_ref[...] = h1
    tb = (h1 * (1.0 + sc_ref[...]) + sh_ref[...]).astype(BF16)

    bits = lax.bitcast_convert_type(tb.astype(F32), U32)
    xp_ref[:, :half] = (bits[:, :half] & jnp.uint32(0xFFFF0000)) | (bits[:, half:] >> 16)

    logits = jnp.dot(tb, rw_ref[...], preferred_element_type=F32)
    logits_t = jnp.transpose(logits)[:N_GROUPS * EXPERTS_PER_GROUP, :]
    e_lo, e_hi, g_lo, g_hi, bucket = _route(logits_t, rb_ref[...])

    rows = lax.broadcasted_iota(I32, (BUCKET_ROWS, tm), 0)
    onehot = (rows == bucket).astype(F32)
    tri = (lax.broadcasted_iota(I32, (tm, tm), 0) <= lax.broadcasted_iota(I32, (tm, tm), 1)).astype(BF16)
    cum = jnp.dot(onehot.astype(BF16), tri, preferred_element_type=F32)
    run = run_ref[:, 0:1]
    rank = jnp.sum(onehot * (cum - 1.0 + run), axis=0, keepdims=True)
    new_run = run + cum[:, tm - 1:tm]
    run_ref[...] = jnp.broadcast_to(new_run, run_ref.shape)
    cnt_ref[...] = jnp.broadcast_to(new_run, cnt_ref.shape)

    mrow = lax.broadcasted_iota(I32, (8, tm), 0)
    meta = jnp.where(mrow == 0, e_lo.astype(F32), 0.0)
    meta = jnp.where(mrow == 1, e_hi.astype(F32), meta)
    meta = jnp.where(mrow == 2, g_lo, meta)
    meta = jnp.where(mrow == 3, g_hi, meta)
    meta = jnp.where(mrow == 4, bucket.astype(F32), meta)
    meta = jnp.where(mrow == 5, rank, meta)
    meta_ref[...] = meta

    grow = lax.broadcasted_iota(I32, (LANES, tm), 0)
    gates_t = jnp.where(grow == 0, g_lo, jnp.where(grow == 1, g_hi, 0.0))
    xp_ref[:, half:] = lax.bitcast_convert_type(jnp.transpose(gates_t), U32)


def _proj_ln_call(a, w, bias, h, gate, lng, lnb, sc, sh, rw_pad, rb, alpha):
    bsz, s, d = h.shape
    dm = a.shape[2]
    tm = min(TOKEN_TILE, s)
    t = bsz * s
    nt = s // tm
    dp = d // 2 + LANES
    kern = functools.partial(_proj_ln_kernel, alpha=alpha)
    vec = lambda: pl.BlockSpec((1, d), lambda b, i: (0, 0))
    bvec = lambda: pl.BlockSpec((None, 1, d), lambda b, i: (b, 0, 0))
    return pl.pallas_call(
        kern,
        out_shape=(jax.ShapeDtypeStruct((bsz, s, d), F32),
                   jax.ShapeDtypeStruct((t, dp), U32),
                   jax.ShapeDtypeStruct((8, t), F32),
                   jax.ShapeDtypeStruct((BUCKET_ROWS, LANES), F32)),
        grid=(bsz, nt),
        in_specs=[
            pl.BlockSpec((None, tm, dm), lambda b, i: (b, i, 0)),
            _resident((dm, d), lambda b, i: (0, 0)),
            vec(),
            pl.BlockSpec((None, tm, d), lambda b, i: (b, i, 0)),
            bvec(), vec(), vec(), bvec(), bvec(),
            pl.BlockSpec((d, LANES), lambda b, i: (0, 0)),
            pl.BlockSpec((N_GROUPS * EXPERTS_PER_GROUP, 1), lambda b, i: (0, 0)),
        ],
        out_specs=(pl.BlockSpec((None, tm, d), lambda b, i: (b, i, 0)),
                   pl.BlockSpec((tm, dp), lambda b, i: (b * nt + i, 0)),
                   pl.BlockSpec((8, tm), lambda b, i: (0, b * nt + i)),
                   pl.BlockSpec((BUCKET_ROWS, LANES), lambda b, i: (0, 0))),
        scratch_shapes=[pltpu.VMEM((BUCKET_ROWS, LANES), F32)],
        compiler_params=_cparams("arbitrary", "arbitrary"),
        name="proj_ln_route",
    )(a, w, bias, h, gate, lng, lnb, sc, sh, rw_pad, rb)


def _wait_rows(src_row, dst_row, sem, n):
    def wait(r, c):
        pltpu.make_async_copy(src_row, dst_row, sem).wait()
        return c

    lax.fori_loop(0, n, wait, 0, unroll=8)


def _scatter_kernel(dest_ref, xp_ref, init_ref, xs_ref, sem):
    del init_ref
    tm = xp_ref.shape[0]
    base = pl.program_id(0) * tm

    def issue(r, c):
        pltpu.make_async_copy(xp_ref.at[pl.ds(r, 1)], xs_ref.at[pl.ds(dest_ref[base + r], 1)], sem).start()
        return c

    lax.fori_loop(0, tm, issue, 0, unroll=8)
    _wait_rows(xp_ref.at[pl.ds(0, 1)], xs_ref.at[pl.ds(0, 1)], sem, tm)


def _scatter_call(dest, xp, rows_out):
    t, dp = xp.shape
    tm = min(TOKEN_TILE, t)
    init = jnp.zeros((rows_out, dp), U32)
    return pl.pallas_call(
        _scatter_kernel,
        out_shape=jax.ShapeDtypeStruct((rows_out, dp), U32),
        grid_spec=pltpu.PrefetchScalarGridSpec(
            num_scalar_prefetch=1,
            grid=(t // tm,),
            in_specs=[pl.BlockSpec((tm, dp), lambda i, dest: (i, 0)),
                      pl.BlockSpec(memory_space=pl.ANY)],
            out_specs=pl.BlockSpec(memory_space=pl.ANY),
            scratch_shapes=[pltpu.SemaphoreType.DMA],
        ),
        input_output_aliases={2: 0},
        compiler_params=_cparams("arbitrary"),
        name="moe_scatter",
    )(dest, xp, init)


def _moe_kernel(ex_ref, valid_ref, xs_ref, w1_ref, w3_ref, w2_ref, y_ref):
    del ex_ref
    i = pl.program_id(0)
    s = pl.program_id(1)
    half = xs_ref.shape[1] - LANES
    ok = valid_ref[i] > 0

    @pl.when(ok)
    def _():
        w = xs_ref[:, :half]
        hi = lax.bitcast_convert_type(w & jnp.uint32(0xFFFF0000), F32).astype(BF16)
        lo = lax.bitcast_convert_type(w << 16, F32).astype(BF16)
        x = jnp.concatenate([hi, lo], axis=1)
        gates = lax.bitcast_convert_type(xs_ref[:, half:], F32)
        which = (s + i) % 2
        g = jnp.where(which == 0, gates[:, 0:1], gates[:, 1:2])
        a = jnp.dot(x, w1_ref[...], preferred_element_type=F32)
        b = jnp.dot(x, w3_ref[...], preferred_element_type=F32)
        hid = (_silu(a) * b * g).astype(BF16)
        y = jnp.dot(hid, w2_ref[...], preferred_element_type=F32)

        @pl.when(s == 0)
        def _():
            y_ref[...] = y

        @pl.when(s != 0)
        def _():
            y_ref[...] += y

    @pl.when(jnp.logical_not(ok) & (s == 0))
    def _():
        y_ref[...] = jnp.zeros_like(y_ref)


def _moe_call(ex, valid, xs, w1, w3, w2):
    rows, dp = xs.shape
    n_e, d, ff = w1.shape
    tm = min(TOKEN_TILE, rows)
    ntiles = rows // tm
    return pl.pallas_call(
        _moe_kernel,
        out_shape=jax.ShapeDtypeStruct((rows, d), F32),
        grid_spec=pltpu.PrefetchScalarGridSpec(
            num_scalar_prefetch=2,
            grid=(ntiles, 2),
            in_specs=[
                pl.BlockSpec((tm, dp), lambda i, s, ex, va: (i, 0)),
                pl.BlockSpec((None, d, ff), lambda i, s, ex, va: (ex[2 * i + s], 0, 0)),
                pl.BlockSpec((None, d, ff), lambda i, s, ex, va: (ex[2 * i + s], 0, 0)),
                pl.BlockSpec((None, ff, d), lambda i, s, ex, va: (ex[2 * i + s], 0, 0)),
            ],
            out_specs=pl.BlockSpec((tm, d), lambda i, s, ex, va: (i, 0)),
        ),
        compiler_params=_cparams("arbitrary", "arbitrary"),
        name="moe_experts",
    )(ex, valid, xs, w1, w3, w2)


def _gather_ln_kernel(dest_ref, ys_ref, h_ref, gate_ref, lng_ref, lnb_ref, o_ref, buf, sem, *, alpha, nt):
    tm = h_ref.shape[0]
    base = (pl.program_id(0) * nt + pl.program_id(1)) * tm

    def issue(r, c):
        pltpu.make_async_copy(ys_ref.at[pl.ds(dest_ref[base + r], 1)], buf.at[pl.ds(r, 1)], sem).start()
        return c

    lax.fori_loop(0, tm, issue, 0, unroll=8)
    _wait_rows(ys_ref.at[pl.ds(0, 1)], buf.at[pl.ds(0, 1)], sem, tm)
    o_ref[...] = _layer_norm(alpha * h_ref[...] + gate_ref[...] * buf[...], lng_ref[...], lnb_ref[...])


def _gather_ln_call(dest, ys, h, gate, lng, lnb, alpha):
    bsz, s, d = h.shape
    tm = min(TOKEN_TILE, s)
    nt = s // tm
    kern = functools.partial(_gather_ln_kernel, alpha=alpha, nt=nt)
    return pl.pallas_call(
        kern,
        out_shape=jax.ShapeDtypeStruct((bsz, s, d), F32),
        grid_spec=pltpu.PrefetchScalarGridSpec(
            num_scalar_prefetch=1,
            grid=(bsz, nt),
            in_specs=[
                pl.BlockSpec(memory_space=pl.ANY),
                pl.BlockSpec((None, tm, d), lambda b, i, dest: (b, i, 0)),
                pl.BlockSpec((None, 1, d), lambda b, i, dest: (b, 0, 0)),
                pl.BlockSpec((1, d), lambda b, i, dest: (0, 0)),
                pl.BlockSpec((1, d), lambda b, i, dest: (0, 0)),
            ],
            out_specs=pl.BlockSpec((None, tm, d), lambda b, i, dest: (b, i, 0)),
            scratch_shapes=[pltpu.VMEM((tm, d), F32), pltpu.SemaphoreType.DMA],
        ),
        compiler_params=_cparams("arbitrary", "arbitrary"),
        name="moe_gather_ln",
    )(dest, ys, h, gate, lng, lnb)


_PAIR_LO = (0, 0, 0, 1, 1, 2)
_PAIR_HI = (1, 2, 3, 2, 3, 3)


def _moe_layer(xp, meta, cnt, h1, gate2, lng, lnb, w1, w3, w2, alpha):
    t = xp.shape[0]
    tm = min(TOKEN_TILE, t)
    ntiles = t // tm + N_BUCKETS
    counts = cnt[:N_BUCKETS, 0].astype(I32)
    tiles_b = (counts + tm - 1) // tm
    tile_end = jnp.cumsum(tiles_b)
    offs = (tile_end - tiles_b) * tm
    bucket = meta[4].astype(I32)
    dest = offs[bucket] + meta[5].astype(I32)

    tile = jnp.arange(ntiles, dtype=I32)
    valid = (tile < tile_end[-1]).astype(I32)
    tb = jnp.minimum(jnp.sum((tile[:, None] >= tile_end[None, :]).astype(I32), axis=1), N_BUCKETS - 1)
    lo = jnp.asarray(_PAIR_LO, I32)[tb % N_PAIRS] + EXPERTS_PER_GROUP * (tb // N_PAIRS)
    hi = jnp.asarray(_PAIR_HI, I32)[tb % N_PAIRS] + EXPERTS_PER_GROUP * (tb // N_PAIRS)
    odd = (tile % 2) == 1
    ex = jnp.stack([jnp.where(odd, hi, lo), jnp.where(odd, lo, hi)], axis=1).reshape(-1)

    xs = _scatter_call(dest, xp, ntiles * tm)
    ys = _moe_call(ex, valid, xs, w1, w3, w2)
    return _gather_ln_call(dest, ys, h1, gate2, lng, lnb, alpha)


def _hy_in_kernel(xm_ref, xp_ref, xn_ref, sc_ref, sh_ref, w_ref, b_ref, cw_ref, cb_ref, x0_ref, z_ref, *, nt, tn):
    i = pl.program_id(1)
    tm, d = xm_ref.shape
    sc, sh = 1.0 + sc_ref[...], sh_ref[...]
    u = jnp.concatenate([xp_ref[...] * sc + sh, xm_ref[...] * sc + sh, xn_ref[...] * sc + sh], axis=0).astype(BF16)
    rows = tm + 16
    rid = lax.broadcasted_iota(I32, (rows, 1), 0)
    keep = ((rid >= 8) | (i > 0)) & ((rid < tm + 8) | (i < nt - 1))

    def conv(sec, j):
        col = sec * d + j * tn
        p = jnp.dot(u, w_ref[:, col:col + tn], preferred_element_type=F32) + b_ref[:, col:col + tn]
        p = jnp.where(keep, p, 0.0)
        cw = cw_ref[:, col:col + tn]
        out = (cw[0:1] * pltpu.roll(p, 1, 0) + cw[1:2] * p + cw[2:3] * pltpu.roll(p, rows - 1, 0)
               + cb_ref[:, col:col + tn])
        return out[8:8 + tm]

    for j in range(d // tn):
        x0_ref[:, j * tn:(j + 1) * tn] = conv(0, j).astype(BF16)
        z_ref[:, j * tn:(j + 1) * tn] = (conv(1, j) * conv(2, j)).astype(BF16)


def _hy_in_call(h, sc, sh, w, b, cw, cb):
    bsz, s, d = h.shape
    tm = min(TOKEN_TILE, s)
    nt = s // tm
    hb = tm // 8
    tn = 512
    kern = functools.partial(_hy_in_kernel, nt=nt, tn=tn)
    return pl.pallas_call(
        kern,
        out_shape=(jax.ShapeDtypeStruct((bsz, s, d), BF16), jax.ShapeDtypeStruct((bsz, s, d), BF16)),
        grid=(bsz, nt),
        in_specs=[
            pl.BlockSpec((None, tm, d), lambda b, i: (b, i, 0)),
            pl.BlockSpec((None, 8, d), lambda b, i: (b, jnp.maximum(i * hb - 1, 0), 0)),
            pl.BlockSpec((None, 8, d), lambda b, i: (b, jnp.minimum((i + 1) * hb, nt * hb - 1), 0)),
            pl.BlockSpec((None, 1, d), lambda b, i: (b, 0, 0)),
            pl.BlockSpec((None, 1, d), lambda b, i: (b, 0, 0)),
            _resident((d, 3 * d), lambda b, i: (0, 0)),
            pl.BlockSpec((1, 3 * d), lambda b, i: (0, 0)),
            pl.BlockSpec((3, 3 * d), lambda b, i: (0, 0)),
            pl.BlockSpec((1, 3 * d), lambda b, i: (0, 0)),
        ],
        out_specs=(pl.BlockSpec((None, tm, d), lambda b, i: (b, i, 0)),
                   pl.BlockSpec((None, tm, d), lambda b, i: (b, i, 0))),
        compiler_params=_cparams("arbitrary", "arbitrary"),
        name="hyena_in",
    )(h, h, h, sc, sh, w, b, cw, cb)


def _filter_kernel(w1_ref, b1_ref, w2_ref, b2_ref, w3_ref, b3_ref, fr_ref, w4_ref, dl_ref, f_ref, l1_ref,
                   *, seq, tn):
    i = pl.program_id(0)

    @pl.when(i == 0)
    def _():
        l1_ref[...] = jnp.zeros_like(l1_ref)

    n_lane = i * tn + lax.broadcasted_iota(I32, (1, tn), 1)
    m_lane = jnp.where(n_lane < seq, n_lane, 2 * seq - n_lane).astype(F32)
    t_lane = m_lane / (seq - 1.0)
    wl = (2.0 * math.pi) * m_lane / float(seq)
    band = lax.broadcasted_iota(I32, (HY_BANDS, 1), 0).astype(F32)
    fb = 1e-4 + band * ((HY_BANDS - 1 - 1e-4) / (HY_BANDS - 1))
    ang = fb * wl
    z = jnp.concatenate([t_lane, jnp.cos(ang), -jnp.sin(ang),
                         jnp.zeros((7, tn), F32)], axis=0)
    fr = fr_ref[...]

    def layer(w_ref, b_ref, x):
        pre = lax.dot_general(w_ref[...], x, (((0,), (0,)), ((), ())), preferred_element_type=F32,
                              precision=HIGHEST)
        return jnp.sin(fr * (pre + b_ref[...]))

    hdn = layer(w1_ref, b1_ref, z)
    hdn = layer(w2_ref, b2_ref, hdn)
    hdn = layer(w3_ref, b3_ref, hdn)
    h = lax.dot_general(hdn, w4_ref[...], (((0,), (0,)), ((), ())), preferred_element_type=F32,
                        precision=HIGHEST)
    n_col = i * tn + lax.broadcasted_iota(I32, (tn, 1), 0)
    m_col = jnp.where(n_col < seq, n_col, 2 * seq - n_col).astype(F32)
    h = h * jnp.exp(-(m_col / (seq - 1.0)) * dl_ref[...])
    h = jnp.where(n_col == seq, 0.0, h)
    l1_ref[...] += jnp.sum(jnp.abs(h), axis=0, keepdims=True)
    f_ref[...] = h.astype(BF16)


def _filter_call(fw1, fb1, fw2, fb2, fw3, fb3, freq, fw4, seq):
    width = fw2.shape[0]
    d = fw4.shape[1] // 2
    tn = min(512, seq)
    steps = 2 * seq // tn
    max_decay = math.log(HY_TARGET) / HY_FAST_DECAY
    min_decay = math.log(HY_TARGET) / HY_SLOW_DECAY
    deltas = jnp.abs(jnp.linspace(min_decay, max_decay, d, dtype=F32)).reshape(1, d)
    w1p = jnp.concatenate([fw1, jnp.zeros((7, width), F32)], axis=0)
    col = lambda a: a.reshape(width, 1)
    small = lambda shp: pl.BlockSpec(shp, lambda i: (0, 0))
    kern = functools.partial(_filter_kernel, seq=seq, tn=tn)
    return pl.pallas_call(
        kern,
        out_shape=(jax.ShapeDtypeStruct((2 * seq, d), BF16), jax.ShapeDtypeStruct((1, d), F32)),
        grid=(steps,),
        in_specs=[small((40, width)), small((width, 1)), small((width, width)), small((width, 1)),
                  small((width, width)), small((width, 1)), small((width, 1)),
                  pl.BlockSpec((width, d), lambda i: (0, (i * tn) // seq)),
                  small((1, d))],
        out_specs=(pl.BlockSpec((tn, d), lambda i: (i, 0)), small((1, d))),
        compiler_params=_cparams("arbitrary"),
        name="hyena_filter",
    )(w1p, col(fb1), fw2, col(fb2), fw3, col(fb3), col(freq), fw4, deltas)


def _dft_tables(n1, n2):
    n = n1 * n2
    k = np.arange(n1)[:, None]
    m = np.arange(n1)[None, :]
    ang1 = -2.0 * np.pi * ((k * m) % n1) / n1
    f1r, f1i = np.cos(ang1), np.sin(ang1)
    hn = n1 // 2
    a_data = np.block([[f1r[:, :hn], -f1i[:, :hn]], [f1i[:, :hn], f1r[:, :hn]]])
    a_filt = np.concatenate([f1r, f1i], axis=0)
    a_inv = np.block([[f1r.T[:hn], f1i.T[:hn]], [-f1i.T[:hn], f1r.T[:hn]]]) / n
    k2 = np.arange(n2)[:, None]
    m2 = np.arange(n2)[None, :]
    ang2 = -2.0 * np.pi * ((k2 * m2) % n2) / n2
    f2r, f2i = np.cos(ang2), np.sin(ang2)
    b_fwd = np.block([[f2r, -f2i], [f2i, f2r]])
    b_inv = np.block([[f2r, f2i], [-f2i, f2r]])
    angt = -2.0 * np.pi * ((np.arange(n2)[:, None] * np.arange(n1)[None, :]) % n) / n
    tw = (np.cos(angt), np.sin(angt))
    as_bf16 = lambda a: jnp.asarray(a, F32).astype(BF16)
    return dict(a_data=as_bf16(a_data), a_filt=as_bf16(a_filt), a_inv=as_bf16(a_inv),
                b_fwd=as_bf16(b_fwd), b_inv=as_bf16(b_inv),
                tw_n2=tuple(jnp.asarray(t, F32).reshape(n2, n1, 1) for t in tw),
                tw_k1=tuple(jnp.asarray(t.T.copy(), F32).reshape(n1, n2, 1) for t in tw))


FFT_K1_BLOCK = 8


def _pack_pair(hi, lo):
    hb = lax.bitcast_convert_type(hi.astype(BF16).astype(F32), U32)
    lb = lax.bitcast_convert_type(lo.astype(BF16).astype(F32), U32)
    return (hb & jnp.uint32(0xFFFF0000)) | (lb >> 16)


def _unpack_pair(w):
    hi = lax.bitcast_convert_type(w & jnp.uint32(0xFFFF0000), F32)
    lo = lax.bitcast_convert_type(w << 16, F32)
    return hi, lo


def _fft_a_kernel(xa_ref, xb_ref, a_ref, twr_ref, twi_ref, y_ref):
    n1 = a_ref.shape[0] // 2
    rhs = jnp.concatenate([xa_ref[...], xb_ref[...]], axis=0)
    y = jnp.dot(a_ref[...], rhs, preferred_element_type=F32)
    yr, yi = y[:n1], y[n1:]
    tr, ti = twr_ref[...], twi_ref[...]
    y_ref[...] = _pack_pair(yr * tr - yi * ti, yr * ti + yi * tr)


def _fft_a_call(x4, a_mat, tw_n2, block_a, block_b):
    p, hn, n2, d = x4.shape
    n1 = 2 * hn
    x3 = x4.reshape(p, hn, n2 * d)
    return pl.pallas_call(
        _fft_a_kernel,
        out_shape=jax.ShapeDtypeStruct((n2, n1, d), U32),
        grid=(n2,),
        in_specs=[
            pl.BlockSpec((None, hn, d), lambda j: (block_a, 0, j)),
            pl.BlockSpec((None, hn, d), lambda j: (block_b, 0, j)),
            pl.BlockSpec((2 * n1, n1), lambda j: (0, 0)),
            pl.BlockSpec((None, n1, 1), lambda j: (j, 0, 0)),
            pl.BlockSpec((None, n1, 1), lambda j: (j, 0, 0)),
        ],
        out_specs=pl.BlockSpec((None, n1, d), lambda j: (j, 0, 0)),
        compiler_params=_cparams("arbitrary"),
        name="fft_stage_a",
    )(x3, x3, a_mat, tw_n2[0], tw_n2[1])


def _load_k1(y_ref, kk):
    yr, yi = _unpack_pair(y_ref[:, kk, :])
    return jnp.concatenate([yr, yi], axis=0).astype(BF16)


def _fft_b_kernel(y_ref, b_ref, h_ref):
    n2 = y_ref.shape[0]
    for kk in range(FFT_K1_BLOCK):
        x = jnp.dot(b_ref[...], _load_k1(y_ref, kk), preferred_element_type=F32)
        h_ref[kk] = _pack_pair(x[:n2], x[n2:])


def _fft_td(d):
    return min(d, 1024)


def _fft_b_call(y, b_fwd):
    n2, n1, d = y.shape
    td = _fft_td(d)
    return pl.pallas_call(
        _fft_b_kernel,
        out_shape=jax.ShapeDtypeStruct((n1, n2, d), U32),
        grid=(n1 // FFT_K1_BLOCK, d // td),
        in_specs=[pl.BlockSpec((n2, FFT_K1_BLOCK, td), lambda k, c: (0, k, c)),
                  pl.BlockSpec((2 * n2, 2 * n2), lambda k, c: (0, 0))],
        out_specs=pl.BlockSpec((FFT_K1_BLOCK, n2, td), lambda k, c: (k, 0, c)),
        compiler_params=_cparams("arbitrary", "arbitrary"),
        name="fft_filter_b",
    )(y, b_fwd)


def _fft_bc_kernel(y_ref, h_ref, bf_ref, bi_ref, twr_ref, twi_ref, g_ref):
    n2 = y_ref.shape[0]
    for kk in range(FFT_K1_BLOCK):
        x = jnp.dot(bf_ref[...], _load_k1(y_ref, kk), preferred_element_type=F32)
        xr, xi = x[:n2], x[n2:]
        hr, hi = _unpack_pair(h_ref[kk])
        z = jnp.concatenate([xr * hr - xi * hi, xr * hi + xi * hr], axis=0).astype(BF16)
        g = jnp.dot(bi_ref[...], z, preferred_element_type=F32)
        gr, gi = g[:n2], g[n2:]
        tr, ti = twr_ref[kk], twi_ref[kk]
        g_ref[:, kk, :] = _pack_pair(gr * tr + gi * ti, gi * tr - gr * ti)


def _fft_bc_call(y, hspec, b_fwd, b_inv, tw_k1):
    n2, n1, d = y.shape
    td = _fft_td(d)
    blk = lambda: pl.BlockSpec((n2, FFT_K1_BLOCK, td), lambda k, c: (0, k, c))
    mat = lambda: pl.BlockSpec((2 * n2, 2 * n2), lambda k, c: (0, 0))
    tw = lambda: pl.BlockSpec((FFT_K1_BLOCK, n2, 1), lambda k, c: (k, 0, 0))
    return pl.pallas_call(
        _fft_bc_kernel,
        out_shape=jax.ShapeDtypeStruct((n2, n1, d), U32),
        grid=(n1 // FFT_K1_BLOCK, d // td),
        in_specs=[blk(), pl.BlockSpec((FFT_K1_BLOCK, n2, td), lambda k, c: (k, 0, c)), mat(), mat(), tw(), tw()],
        out_specs=blk(),
        compiler_params=_cparams("arbitrary", "arbitrary"),
        name="fft_stage_bc",
    )(y, hspec, b_fwd, b_inv, tw_k1[0], tw_k1[1])


def _fft_d_kernel(g_ref, a_ref, x0_ref, z_ref, l1_ref, fb_ref, o_ref):
    hn = g_ref.shape[0] // 2
    gr, gi = _unpack_pair(g_ref[...])
    rhs = jnp.concatenate([gr, gi], axis=0).astype(BF16)
    y = jnp.dot(a_ref[...], rhs, preferred_element_type=F32)
    inv_l1 = 1.0 / l1_ref[...]
    fb = fb_ref[...]
    for b in range(2):
        conv = y[b * hn:(b + 1) * hn] * inv_l1
        o_ref[b] = (x0_ref[b].astype(F32) * (conv + fb * z_ref[b].astype(F32))).astype(BF16)


def _fft_d_call(g, a_inv, x0_4, z_4, l1, fbias):
    n2, n1, d = g.shape
    hn = n1 // 2
    tok = lambda: pl.BlockSpec((2, hn, d), lambda j: (0, 0, j))
    flat = lambda a: a.reshape(a.shape[0], a.shape[1], n2 * d)
    out = pl.pallas_call(
        _fft_d_kernel,
        out_shape=jax.ShapeDtypeStruct((2, hn, n2 * d), BF16),
        grid=(n2,),
        in_specs=[pl.BlockSpec((None, n1, d), lambda j: (j, 0, 0)),
                  pl.BlockSpec((n1, 2 * n1), lambda j: (0, 0)),
                  tok(), tok(),
                  pl.BlockSpec((1, d), lambda j: (0, 0)),
                  pl.BlockSpec((1, d), lambda j: (0, 0))],
        out_specs=tok(),
        compiler_params=_cparams("arbitrary"),
        name="fft_stage_d",
    )(g, a_inv, flat(x0_4), flat(z_4), l1, fbias)
    return out.reshape(2, hn, n2, d)


def _hyena_conv(x0, z, filt, l1, fbias):
    bsz, seq, d = z.shape
    assert bsz == 2, "the two batch rows are packed as one complex signal"
    n2 = FFT_N2
    n1 = 2 * seq // n2
    tabs = _dft_tables(n1, n2)
    hn = n1 // 2
    hspec = _fft_b_call(_fft_a_call(filt.reshape(2, hn, n2, d), tabs["a_filt"], tabs["tw_n2"], 0, 1),
                        tabs["b_fwd"])
    y = _fft_a_call(z.reshape(2, hn, n2, d), tabs["a_data"], tabs["tw_n2"], 0, 1)
    g = _fft_bc_call(y, hspec, tabs["b_fwd"], tabs["b_inv"], tabs["tw_k1"])
    out = _fft_d_call(g, tabs["a_inv"], x0.reshape(2, hn, n2, d), z.reshape(2, hn, n2, d), l1, fbias)
    return out.reshape(bsz, seq, d)


def _dup_heads(w, n_kv, head_dim):
    lead = w.shape[:-1]
    w = w.reshape(lead + (n_kv, head_dim))
    return jnp.concatenate([w, w], axis=-1).reshape(lead + (2 * n_kv * head_dim,))


def _rope_tables(seq, head_dim):
    axis_dim = head_dim // 2
    rows = seq // GRID_W
    inv = ROPE_BASE ** (-jnp.arange(0, axis_dim, 2, dtype=F32) / axis_dim)
    row = jnp.repeat(jnp.arange(rows, dtype=F32), GRID_W)[:, None] * inv
    col = jnp.tile(jnp.arange(GRID_W, dtype=F32), rows)[:, None] * inv
    quarter = axis_dim // 2
    cos = jnp.concatenate([jnp.cos(row), jnp.cos(row), jnp.cos(col), jnp.cos(col)], axis=1)
    sin = jnp.concatenate([jnp.sin(row), jnp.sin(row), jnp.sin(col), jnp.sin(col)], axis=1)
    second = (np.arange(head_dim) % axis_dim) >= quarter
    reps = LANES // head_dim
    cos = jnp.tile(cos, (1, reps))
    sin = jnp.tile(sin, (1, reps))
    second = jnp.asarray(np.tile(second, reps))[None, :]
    return cos, jnp.where(second, sin, 0.0), jnp.where(second, 0.0, -sin)


def kernel(x, c, ctx, c_ctx, ada_w, ada_b, attn_w_in, attn_b_in, attn_sink, attn_w_out, hy_w_in, hy_b_in, hy_conv_w, hy_conv_b, hy_f_w1, hy_f_b1, hy_f_w2, hy_f_b2, hy_f_w3, hy_f_b3, hy_f_freq, hy_f_w4, hy_f_bias, hy_w_out, hy_b_out, ln1_g, ln1_b, ln2_g, ln2_b, router_w, router_b, moe_w1, moe_w3, moe_w2):
    bsz, seq, d = x.shape
    depth = ada_w.shape[0]
    assert depth == 2 and attn_w_in.shape[0] == 1 and hy_w_in.shape[0] == 1
    alpha = (2 * depth) ** 0.25
    n_heads = attn_sink.shape[1]
    attn_dim = attn_w_out.shape[1]
    head_dim = attn_dim // n_heads
    kv_dim = (attn_w_in.shape[2] - attn_dim) // 2
    n_kv = kv_dim // head_dim
    group = n_heads // n_kv
    assert head_dim * 2 == LANES and group % 2 == 0
    n_exp = router_w.shape[1]
    assert n_exp == N_GROUPS * EXPERTS_PER_GROUP

    cond = jnp.concatenate([c, c_ctx[None, :], jnp.zeros((8 - bsz - 1, d), F32)], axis=0)
    mods = _ada_call(cond, ada_w, ada_b).reshape(depth, 8, 6, d)
    mod = lambda layer, k: mods[layer, :bsz, k].reshape(bsz, 1, d)
    cmod = lambda layer, k: mods[layer, bsz, k].reshape(1, d)
    row = lambda v: v.reshape(1, -1)

    rw_pad = jnp.concatenate([router_w, jnp.zeros((d, LANES - n_exp), F32)], axis=1).astype(BF16)
    rb = router_b.reshape(n_exp, 1)
    w1b, w3b, w2b = moe_w1.astype(BF16), moe_w3.astype(BF16), moe_w2.astype(BF16)

    w_in, b_in = attn_w_in[0], attn_b_in[0]
    wq, wk, wv = w_in[:, :attn_dim], w_in[:, attn_dim:attn_dim + kv_dim], w_in[:, attn_dim + kv_dim:]
    bq, bk, bv = b_in[:attn_dim], b_in[attn_dim:attn_dim + kv_dim], b_in[attn_dim + kv_dim:]
    dup = lambda a: _dup_heads(a, n_kv, head_dim)
    w_kv = jnp.concatenate([dup(wk), dup(wv)], axis=1).astype(BF16)
    b_kv = row(jnp.concatenate([dup(bk), dup(bv)]))
    w_ext = jnp.concatenate([wq.astype(BF16), w_kv], axis=1)
    b_ext = jnp.concatenate([row(bq), b_kv], axis=1)
    n_kd = 2 * kv_dim
    cos_t, sa_t, sb_t = _rope_tables(seq, head_dim)

    q, kd, vd = _qkv_call(x, mod(0, 1), mod(0, 0), w_ext, b_ext, cos_t, sa_t, sb_t, attn_dim, n_kd, head_dim)
    kxd, vxd = _ctx_kv_call(ctx, cmod(0, 1), cmod(0, 0), w_kv, b_kv, n_kd)
    att = _attn_call(attn_sink[0], q, kd, vd, kxd, vxd, n_kv, group)
    h1, xp, meta, cnt = _proj_ln_call(att, attn_w_out[0].astype(BF16), jnp.zeros((1, d), F32), x, mod(0, 2),
                                      row(ln1_g[0]), row(ln1_b[0]), mod(0, 4), mod(0, 3), rw_pad, rb, alpha)
    h = _moe_layer(xp, meta, cnt, h1, mod(0, 5), row(ln2_g[0]), row(ln2_b[0]), w1b[0], w3b[0], w2b[0], alpha)

    x0, z = _hy_in_call(h, mod(1, 1), mod(1, 0), hy_w_in[0].astype(BF16), row(hy_b_in[0]), hy_conv_w[0],
                        row(hy_conv_b[0]))
    filt, l1 = _filter_call(hy_f_w1[0], hy_f_b1[0], hy_f_w2[0], hy_f_b2[0], hy_f_w3[0], hy_f_b3[0],
                            hy_f_freq[0], hy_f_w4[0], seq)
    yh = _hyena_conv(x0, z, filt, l1, row(hy_f_bias[0]))
    h1, xp, meta, cnt = _proj_ln_call(yh, hy_w_out[0].astype(BF16), row(hy_b_out[0]), h, mod(1, 2),
                                      row(ln1_g[1]), row(ln1_b[1]), mod(1, 4), mod(1, 3), rw_pad, rb, alpha)
    return _moe_layer(xp, meta, cnt, h1, mod(1, 5), row(ln2_g[1]), row(ln2_b[1]), w1b[1], w3b[1], w2b[1], alpha)
```

```python
import functools
import math

import numpy as np
import jax
import jax.numpy as jnp
from jax import lax
from jax.experimental import pallas as pl
from jax.experimental.pallas import tpu as pltpu

F32 = jnp.float32
BF16 = jnp.bfloat16
I32 = jnp.int32
U32 = jnp.uint32
HIGHEST = lax.Precision.HIGHEST

LANES = 128
V7X_VMEM_LIMIT_BYTES = 56 * 1024 * 1024

GRID_W = 64
BLOCK = 128
ROPE_BASE = 10000.0
NEG_INF = -1e30
HY_BANDS = 16
HY_FAST_DECAY = 0.3
HY_SLOW_DECAY = 1.5
HY_TARGET = 1e-2
N_GROUPS = 4
EXPERTS_PER_GROUP = 4
N_PAIRS = 6
N_BUCKETS = N_GROUPS * N_PAIRS
BUCKET_ROWS = 32
LN_EPS = 1e-5
FFT_N2 = 128

TOKEN_TILE = 512


def _cparams(*sem):
    return pltpu.CompilerParams(dimension_semantics=sem, vmem_limit_bytes=V7X_VMEM_LIMIT_BYTES)


def _resident(block_shape, index_map):
    return pl.BlockSpec(block_shape, index_map, pipeline_mode=pl.Buffered(1))


def _silu(x):
    return x * jax.nn.sigmoid(x)


def _ada_kernel(c_ref, w_ref, b_ref, o_ref):
    c = _silu(c_ref[...])
    o_ref[...] = jnp.dot(c, w_ref[...], preferred_element_type=F32, precision=HIGHEST) + b_ref[...]


def _ada_call(cond, ada_w, ada_b):
    depth, d, n6 = ada_w.shape
    tn = 1024
    rows = cond.shape[0]
    return pl.pallas_call(
        _ada_kernel,
        out_shape=jax.ShapeDtypeStruct((depth, rows, n6), F32),
        grid=(depth, n6 // tn),
        in_specs=[
            pl.BlockSpec((rows, d), lambda l, j: (0, 0)),
            pl.BlockSpec((None, d, tn), lambda l, j: (l, 0, j)),
            pl.BlockSpec((None, 1, tn), lambda l, j: (l, 0, j)),
        ],
        out_specs=pl.BlockSpec((None, rows, tn), lambda l, j: (l, 0, j)),
        compiler_params=_cparams("arbitrary", "arbitrary"),
        name="ada_mod",
    )(cond, ada_w, ada_b.reshape(depth, 1, n6))


def _qkv_kernel(x_ref, sc_ref, sh_ref, w_ref, b_ref, cos_ref, sa_ref, sb_ref, q_ref, k_ref, v_ref, *, scale):
    u = (x_ref[...] * (1.0 + sc_ref[...]) + sh_ref[...]).astype(BF16)
    p = jnp.dot(u, w_ref[...], preferred_element_type=F32) + b_ref[...]
    cos, sa, sb = cos_ref[...], sa_ref[...], sb_ref[...]
    nq = q_ref.shape[1]
    nk = k_ref.shape[1]

    def rope(xc):
        return xc * cos + pltpu.roll(xc, 16, 1) * sa + pltpu.roll(xc, LANES - 16, 1) * sb

    for c in range(nq // LANES):
        q_ref[:, c * LANES:(c + 1) * LANES] = (rope(p[:, c * LANES:(c + 1) * LANES]) * scale).astype(BF16)
    for c in range(nk // LANES):
        o = nq + c * LANES
        k_ref[:, c * LANES:(c + 1) * LANES] = rope(p[:, o:o + LANES]).astype(BF16)
    v_ref[...] = p[:, nq + nk:].astype(BF16)


def _qkv_call(x, sc, sh, w_ext, b_ext, cos_t, sa_t, sb_t, n_q, n_kd, head_dim):
    bsz, s, d = x.shape
    tm = min(TOKEN_TILE, s)
    n_out = w_ext.shape[1]
    kern = functools.partial(_qkv_kernel, scale=head_dim ** -0.5)
    return pl.pallas_call(
        kern,
        out_shape=(jax.ShapeDtypeStruct((bsz, s, n_q), BF16),
                   jax.ShapeDtypeStruct((bsz, s, n_kd), BF16),
                   jax.ShapeDtypeStruct((bsz, s, n_kd), BF16)),
        grid=(bsz, s // tm),
        in_specs=[
            pl.BlockSpec((None, tm, d), lambda b, i: (b, i, 0)),
            pl.BlockSpec((None, 1, d), lambda b, i: (b, 0, 0)),
            pl.BlockSpec((None, 1, d), lambda b, i: (b, 0, 0)),
            _resident((d, n_out), lambda b, i: (0, 0)),
            pl.BlockSpec((1, n_out), lambda b, i: (0, 0)),
            pl.BlockSpec((tm, LANES), lambda b, i: (i, 0)),
            pl.BlockSpec((tm, LANES), lambda b, i: (i, 0)),
            pl.BlockSpec((tm, LANES), lambda b, i: (i, 0)),
        ],
        out_specs=(pl.BlockSpec((None, tm, n_q), lambda b, i: (b, i, 0)),
                   pl.BlockSpec((None, tm, n_kd), lambda b, i: (b, i, 0)),
                   pl.BlockSpec((None, tm, n_kd), lambda b, i: (b, i, 0))),
        compiler_params=_cparams("arbitrary", "arbitrary"),
        name="attn_qkv",
    )(x, sc, sh, w_ext, b_ext, cos_t, sa_t, sb_t)


def _ctx_kv_kernel(x_ref, sc_ref, sh_ref, w_ref, b_ref, k_ref, v_ref):
    u = (x_ref[...] * (1.0 + sc_ref[...]) + sh_ref[...]).astype(BF16)
    p = jnp.dot(u, w_ref[...], preferred_element_type=F32) + b_ref[...]
    nk = k_ref.shape[1]
    k_ref[...] = p[:, :nk].astype(BF16)
    v_ref[...] = p[:, nk:].astype(BF16)


def _ctx_kv_call(ctx, csc, csh, w_kv, b_kv, n_kd):
    bsz, c, d = ctx.shape
    return pl.pallas_call(
        _ctx_kv_kernel,
        out_shape=(jax.ShapeDtypeStruct((bsz, c, n_kd), BF16), jax.ShapeDtypeStruct((bsz, c, n_kd), BF16)),
        grid=(bsz,),
        in_specs=[
            pl.BlockSpec((None, c, d), lambda b: (b, 0, 0)),
            pl.BlockSpec((1, d), lambda b: (0, 0)),
            pl.BlockSpec((1, d), lambda b: (0, 0)),
            pl.BlockSpec((d, 2 * n_kd), lambda b: (0, 0)),
            pl.BlockSpec((1, 2 * n_kd), lambda b: (0, 0)),
        ],
        out_specs=(pl.BlockSpec((None, c, n_kd), lambda b: (b, 0, 0)),
                   pl.BlockSpec((None, c, n_kd), lambda b: (b, 0, 0))),
        compiler_params=_cparams("arbitrary"),
        name="attn_ctx_kv",
    )(ctx, csc, csh, w_kv, b_kv)


def _attn_kernel(sink_ref, q_ref, kp_ref, kc_ref, kn_ref, vp_ref, vc_ref, vn_ref, kx_ref, vx_ref, o_ref,
                 *, n_kv, group, nb):
    n = pl.program_id(1)
    r = lax.broadcasted_iota(I32, (1, BLOCK, BLOCK), 1)
    j = lax.broadcasted_iota(I32, (1, BLOCK, BLOCK), 2)
    prev_ok = (j >= r) & (n > 0)
    next_ok = (j <= r) & (n < nb - 1)
    lo = lax.broadcasted_iota(I32, (1, LANES), 1) < (LANES // 2)
    pairs = group // 2
    head = lax.broadcasted_iota(I32, (group, 1, 1), 0)
    for kh in range(n_kv):
        sl = slice(kh * LANES, (kh + 1) * LANES)
        kcat = jnp.concatenate([kp_ref[:, sl], kc_ref[:, sl], kn_ref[:, sl], kx_ref[:, sl]], axis=0)
        vcat = jnp.concatenate([vp_ref[:, sl], vc_ref[:, sl], vn_ref[:, sl], vx_ref[:, sl]], axis=0)
        nkeys = kcat.shape[0]
        parts = []
        for pp in range(pairs):
            q2 = q_ref[:, (kh * pairs + pp) * LANES:(kh * pairs + pp + 1) * LANES]
            zq = jnp.zeros_like(q2)
            parts += [jnp.where(lo, q2, zq), jnp.where(lo, zq, q2)]
        qs = jnp.concatenate(parts, axis=0)
        s = lax.dot_general(qs, kcat, (((1,), (1,)), ((), ())), preferred_element_type=F32)
        s = s.reshape(group, BLOCK, nkeys)
        s = jnp.concatenate([
            jnp.where(prev_ok, s[:, :, :BLOCK], NEG_INF),
            s[:, :, BLOCK:2 * BLOCK],
            jnp.where(next_ok, s[:, :, 2 * BLOCK:3 * BLOCK], NEG_INF),
            s[:, :, 3 * BLOCK:]], axis=2)
        sk = jnp.zeros((group, 1, 1), F32)
        for g in range(group):
            sk = jnp.where(head == g, sink_ref[kh * group + g], sk)
        m = jnp.maximum(jnp.max(s, axis=2, keepdims=True), sk)
        e = jnp.exp(s - m)
        rden = 1.0 / (jnp.sum(e, axis=2, keepdims=True) + jnp.exp(sk - m))
        o = jnp.dot(e.reshape(group * BLOCK, nkeys).astype(BF16), vcat, preferred_element_type=F32)
        o = o.reshape(group, BLOCK, LANES) * rden
        for pp in range(pairs):
            p = kh * pairs + pp
            o_ref[:, p * LANES:(p + 1) * LANES] = jnp.where(lo, o[2 * pp], o[2 * pp + 1]).astype(BF16)


def _attn_call(sink, q, kd, vd, kxd, vxd, n_kv, group):
    bsz, s, n_q = q.shape
    n_kd = kd.shape[2]
    c = kxd.shape[1]
    nb = s // BLOCK
    kern = functools.partial(_attn_kernel, n_kv=n_kv, group=group, nb=nb)
    prev = lambda b, n: (b, jnp.maximum(n - 1, 0), 0)
    cur = lambda b, n: (b, n, 0)
    nxt = lambda b, n: (b, jnp.minimum(n + 1, nb - 1), 0)
    kv = lambda im: pl.BlockSpec((None, BLOCK, n_kd), im)
    return pl.pallas_call(
        kern,
        out_shape=jax.ShapeDtypeStruct((bsz, s, n_q), BF16),
        grid=(bsz, nb),
        in_specs=[
            pl.BlockSpec(memory_space=pltpu.SMEM),
            pl.BlockSpec((None, BLOCK, n_q), cur),
            kv(prev), kv(cur), kv(nxt), kv(prev), kv(cur), kv(nxt),
            pl.BlockSpec((None, c, n_kd), lambda b, n: (b, 0, 0)),
            pl.BlockSpec((None, c, n_kd), lambda b, n: (b, 0, 0)),
        ],
        out_specs=pl.BlockSpec((None, BLOCK, n_q), cur),
        compiler_params=_cparams("arbitrary", "arbitrary"),
        name="window_attn",
    )(sink, q, kd, kd, kd, vd, vd, vd, kxd, vxd)


def _layer_norm(r, g, b):
    mu = jnp.mean(r, axis=-1, keepdims=True)
    xc = r - mu
    var = jnp.mean(xc * xc, axis=-1, keepdims=True)
    return xc * lax.rsqrt(var + LN_EPS) * g + b


def _route(logits_t, rb):
    s = jax.nn.sigmoid(logits_t)
    sel = s + rb
    n_e = N_GROUPS * EXPERTS_PER_GROUP
    sel_r = [sel[e:e + 1, :] for e in range(n_e)]
    s_r = [s[e:e + 1, :] for e in range(n_e)]
    gscore = []
    for g in range(N_GROUPS):
        a, b, c, d = sel_r[4 * g:4 * g + 4]
        m1, n1, m2, n2 = jnp.maximum(a, b), jnp.minimum(a, b), jnp.maximum(c, d), jnp.minimum(c, d)
        gscore.append(jnp.maximum(m1, m2) + jnp.maximum(jnp.minimum(m1, m2), jnp.maximum(n1, n2)))
    best, gi = gscore[0], jnp.zeros_like(gscore[0], dtype=I32)
    for g in range(1, N_GROUPS):
        upd = gscore[g] > best
        gi = jnp.where(upd, g, gi)
        best = jnp.where(upd, gscore[g], best)

    def pick(rows, i):
        out = rows[i]
        for g in range(1, N_GROUPS):
            out = jnp.where(gi == g, rows[4 * g + i], out)
        return out

    v = [pick(sel_r, i) for i in range(EXPERTS_PER_GROUP)]
    sv = [pick(s_r, i) for i in range(EXPERTS_PER_GROUP)]

    def argmax4(vals):
        bv, bi = vals[0], jnp.zeros_like(gi)
        for i in range(1, EXPERTS_PER_GROUP):
            upd = vals[i] > bv
            bi = jnp.where(upd, i, bi)
            bv = jnp.where(upd, vals[i], bv)
        return bi

    def take4(vals, idx):
        out = vals[0]
        for i in range(1, EXPERTS_PER_GROUP):
            out = jnp.where(idx == i, vals[i], out)
        return out

    i1 = argmax4(v)
    i2 = argmax4([jnp.where(i1 == i, -jnp.inf, v[i]) for i in range(EXPERTS_PER_GROUP)])
    s1, s2 = take4(sv, i1), take4(sv, i2)
    tot = s1 + s2
    g1, g2 = s1 / tot, s2 / tot
    first_lo = i1 < i2
    i_lo, i_hi = jnp.minimum(i1, i2), jnp.maximum(i1, i2)
    g_lo, g_hi = jnp.where(first_lo, g1, g2), jnp.where(first_lo, g2, g1)
    pair = jnp.where(i_lo == 0, i_hi - 1, jnp.where(i_lo == 1, i_hi + 1, N_PAIRS - 1))
    bucket = gi * N_PAIRS + pair
    return 4 * gi + i_lo, 4 * gi + i_hi, g_lo, g_hi, bucket


def _proj_ln_kernel(a_ref, w_ref, bias_ref, h_ref, gate_ref, lng_ref, lnb_ref, sc_ref, sh_ref, rw_ref, rb_ref,
                    h1_ref, xp_ref, meta_ref, cnt_ref, run_ref, *, alpha, a_packed):
    first = (pl.program_id(0) == 0) & (pl.program_id(1) == 0)

    @pl.when(first)
    def _():
        run_ref[...] = jnp.zeros_like(run_ref)

    tm, d = h_ref.shape
    half = d // 2
    a = _unpack_cols(a_ref[...]).astype(BF16) if a_packed else a_ref[...]
    y = jnp.dot(a, w_ref[...], preferred_element_type=F32) + bias_ref[...]
    h1 = _layer_norm(alpha * h_ref[...] + gate_ref[...] * y, lng_ref[...], lnb_ref[...])
    h1_ref[...] = h1
    tb = (h1 * (1.0 + sc_ref[...]) + sh_ref[...]).astype(BF16)

    bits = lax.bitcast_convert_type(tb.astype(F32), U32)
    xp_ref[:, :half] = (bits[:, :half] & jnp.uint32(0xFFFF0000)) | (bits[:, half:] >> 16)

    logits = jnp.dot(tb, rw_ref[...], preferred_element_type=F32)
    logits_t = jnp.transpose(logits)[:N_GROUPS * EXPERTS_PER_GROUP, :]
    e_lo, e_hi, g_lo, g_hi, bucket = _route(logits_t, rb_ref[...])

    rows = lax.broadcasted_iota(I32, (BUCKET_ROWS, tm), 0)
    onehot = (rows == bucket).astype(F32)
    tri = (lax.broadcasted_iota(I32, (tm, tm), 0) <= lax.broadcasted_iota(I32, (tm, tm), 1)).astype(BF16)
    cum = jnp.dot(onehot.astype(BF16), tri, preferred_element_type=F32)
    run = run_ref[:, 0:1]
    rank = jnp.sum(onehot * (cum - 1.0 + run), axis=0, keepdims=True)
    new_run = run + cum[:, tm - 1:tm]
    run_ref[...] = jnp.broadcast_to(new_run, run_ref.shape)
    cnt_ref[...] = jnp.broadcast_to(new_run, cnt_ref.shape)

    mrow = lax.broadcasted_iota(I32, (8, tm), 0)
    meta = jnp.where(mrow == 0, e_lo.astype(F32), 0.0)
    meta = jnp.where(mrow == 1, e_hi.astype(F32), meta)
    meta = jnp.where(mrow == 2, g_lo, meta)
    meta = jnp.where(mrow == 3, g_hi, meta)
    meta = jnp.where(mrow == 4, bucket.astype(F32), meta)
    meta = jnp.where(mrow == 5, rank, meta)
    meta_ref[...] = meta

    grow = lax.broadcasted_iota(I32, (LANES, tm), 0)
    gates_t = jnp.where(grow == 0, g_lo, jnp.where(grow == 1, g_hi, 0.0))
    xp_ref[:, half:] = lax.bitcast_convert_type(jnp.transpose(gates_t), U32)


def _proj_ln_call(a, w, bias, h, gate, lng, lnb, sc, sh, rw_pad, rb, alpha):
    bsz, s, d = h.shape
    a_packed = a.dtype == U32
    da = a.shape[2]
    dm = w.shape[0]
    tm = min(TOKEN_TILE, s)
    t = bsz * s
    nt = s // tm
    dp = d // 2 + LANES
    kern = functools.partial(_proj_ln_kernel, alpha=alpha, a_packed=a_packed)
    vec = lambda: pl.BlockSpec((1, d), lambda b, i: (0, 0))
    bvec = lambda: pl.BlockSpec((None, 1, d), lambda b, i: (b, 0, 0))
    return pl.pallas_call(
        kern,
        out_shape=(jax.ShapeDtypeStruct((bsz, s, d), F32),
                   jax.ShapeDtypeStruct((t, dp), U32),
                   jax.ShapeDtypeStruct((8, t), F32),
                   jax.ShapeDtypeStruct((BUCKET_ROWS, LANES), F32)),
        grid=(bsz, nt),
        in_specs=[
            pl.BlockSpec((None, tm, da), lambda b, i: (b, i, 0)),
            _resident((dm, d), lambda b, i: (0, 0)),
            vec(),
            pl.BlockSpec((None, tm, d), lambda b, i: (b, i, 0)),
            bvec(), vec(), vec(), bvec(), bvec(),
            pl.BlockSpec((d, LANES), lambda b, i: (0, 0)),
            pl.BlockSpec((N_GROUPS * EXPERTS_PER_GROUP, 1), lambda b, i: (0, 0)),
        ],
        out_specs=(pl.BlockSpec((None, tm, d), lambda b, i: (b, i, 0)),
                   pl.BlockSpec((tm, dp), lambda b, i: (b * nt + i, 0)),
                   pl.BlockSpec((8, tm), lambda b, i: (0, b * nt + i)),
                   pl.BlockSpec((BUCKET_ROWS, LANES), lambda b, i: (0, 0))),
        scratch_shapes=[pltpu.VMEM((BUCKET_ROWS, LANES), F32)],
        compiler_params=_cparams("arbitrary", "arbitrary"),
        name="proj_ln_route",
    )(a, w, bias, h, gate, lng, lnb, sc, sh, rw_pad, rb)


def _wait_rows(src_row, dst_row, sem, n):
    def wait(r, c):
        pltpu.make_async_copy(src_row, dst_row, sem).wait()
        return c

    lax.fori_loop(0, n, wait, 0, unroll=8)


def _scatter_kernel(dest_ref, xp_ref, init_ref, xs_ref, sem):
    del init_ref
    tm = xp_ref.shape[0]
    base = pl.program_id(0) * tm

    def issue(r, c):
        pltpu.make_async_copy(xp_ref.at[pl.ds(r, 1)], xs_ref.at[pl.ds(dest_ref[base + r], 1)], sem).start()
        return c

    lax.fori_loop(0, tm, issue, 0, unroll=8)
    _wait_rows(xp_ref.at[pl.ds(0, 1)], xs_ref.at[pl.ds(0, 1)], sem, tm)


def _scatter_call(dest, xp, rows_out):
    t, dp = xp.shape
    tm = min(TOKEN_TILE, t)
    init = jnp.zeros((rows_out, dp), U32)
    return pl.pallas_call(
        _scatter_kernel,
        out_shape=jax.ShapeDtypeStruct((rows_out, dp), U32),
        grid_spec=pltpu.PrefetchScalarGridSpec(
            num_scalar_prefetch=1,
            grid=(t // tm,),
            in_specs=[pl.BlockSpec((tm, dp), lambda i, dest: (i, 0)),
                      pl.BlockSpec(memory_space=pl.ANY)],
            out_specs=pl.BlockSpec(memory_space=pl.ANY),
            scratch_shapes=[pltpu.SemaphoreType.DMA],
        ),
        input_output_aliases={2: 0},
        compiler_params=_cparams("arbitrary"),
        name="moe_scatter",
    )(dest, xp, init)


def _moe_kernel(ex_ref, valid_ref, xs_ref, w1_ref, w3_ref, w2_ref, y_ref):
    del ex_ref
    i = pl.program_id(0)
    s = pl.program_id(1)
    half = xs_ref.shape[1] - LANES
    ok = valid_ref[i] > 0

    @pl.when(ok)
    def _():
        w = xs_ref[:, :half]
        hi = lax.bitcast_convert_type(w & jnp.uint32(0xFFFF0000), F32).astype(BF16)
        lo = lax.bitcast_convert_type(w << 16, F32).astype(BF16)
        x = jnp.concatenate([hi, lo], axis=1)
        gates = lax.bitcast_convert_type(xs_ref[:, half:], F32)
        which = (s + i) % 2
        g = jnp.where(which == 0, gates[:, 0:1], gates[:, 1:2])
        a = jnp.dot(x, w1_ref[...], preferred_element_type=F32)
        b = jnp.dot(x, w3_ref[...], preferred_element_type=F32)
        hid = (_silu(a) * b * g).astype(BF16)
        y = jnp.dot(hid, w2_ref[...], preferred_element_type=F32)

        @pl.when(s == 0)
        def _():
            y_ref[...] = y

        @pl.when(s != 0)
        def _():
            y_ref[...] += y

    @pl.when(jnp.logical_not(ok) & (s == 0))
    def _():
        y_ref[...] = jnp.zeros_like(y_ref)


def _moe_call(ex, valid, xs, w1, w3, w2):
    rows, dp = xs.shape
    n_e, d, ff = w1.shape
    tm = min(TOKEN_TILE, rows)
    ntiles = rows // tm
    return pl.pallas_call(
        _moe_kernel,
        out_shape=jax.ShapeDtypeStruct((rows, d), F32),
        grid_spec=pltpu.PrefetchScalarGridSpec(
            num_scalar_prefetch=2,
            grid=(ntiles, 2),
            in_specs=[
                pl.BlockSpec((tm, dp), lambda i, s, ex, va: (i, 0)),
                pl.BlockSpec((None, d, ff), lambda i, s, ex, va: (ex[2 * i + s], 0, 0)),
                pl.BlockSpec((None, d, ff), lambda i, s, ex, va: (ex[2 * i + s], 0, 0)),
                pl.BlockSpec((None, ff, d), lambda i, s, ex, va: (ex[2 * i + s], 0, 0)),
            ],
            out_specs=pl.BlockSpec((tm, d), lambda i, s, ex, va: (i, 0)),
        ),
        compiler_params=_cparams("arbitrary", "arbitrary"),
        name="moe_experts",
    )(ex, valid, xs, w1, w3, w2)


def _gather_ln_kernel(dest_ref, ys_ref, h_ref, gate_ref, lng_ref, lnb_ref, o_ref, buf, sem, *, alpha, nt):
    tm = h_ref.shape[0]
    base = (pl.program_id(0) * nt + pl.program_id(1)) * tm

    def issue(r, c):
        pltpu.make_async_copy(ys_ref.at[pl.ds(dest_ref[base + r], 1)], buf.at[pl.ds(r, 1)], sem).start()
        return c

    lax.fori_loop(0, tm, issue, 0, unroll=8)
    _wait_rows(ys_ref.at[pl.ds(0, 1)], buf.at[pl.ds(0, 1)], sem, tm)
    o_ref[...] = _layer_norm(alpha * h_ref[...] + gate_ref[...] * buf[...], lng_ref[...], lnb_ref[...])


def _gather_ln_call(dest, ys, h, gate, lng, lnb, alpha):
    bsz, s, d = h.shape
    tm = min(TOKEN_TILE, s)
    nt = s // tm
    kern = functools.partial(_gather_ln_kernel, alpha=alpha, nt=nt)
    return pl.pallas_call(
        kern,
        out_shape=jax.ShapeDtypeStruct((bsz, s, d), F32),
        grid_spec=pltpu.PrefetchScalarGridSpec(
            num_scalar_prefetch=1,
            grid=(bsz, nt),
            in_specs=[
                pl.BlockSpec(memory_space=pl.ANY),
                pl.BlockSpec((None, tm, d), lambda b, i, dest: (b, i, 0)),
                pl.BlockSpec((None, 1, d), lambda b, i, dest: (b, 0, 0)),
                pl.BlockSpec((1, d), lambda b, i, dest: (0, 0)),
                pl.BlockSpec((1, d), lambda b, i, dest: (0, 0)),
            ],
            out_specs=pl.BlockSpec((None, tm, d), lambda b, i, dest: (b, i, 0)),
            scratch_shapes=[pltpu.VMEM((tm, d), F32), pltpu.SemaphoreType.DMA],
        ),
        compiler_params=_cparams("arbitrary", "arbitrary"),
        name="moe_gather_ln",
    )(dest, ys, h, gate, lng, lnb)


_PAIR_LO = (0, 0, 0, 1, 1, 2)
_PAIR_HI = (1, 2, 3, 2, 3, 3)


def _moe_layer(xp, meta, cnt, h1, gate2, lng, lnb, w1, w3, w2, alpha):
    t = xp.shape[0]
    tm = min(TOKEN_TILE, t)
    ntiles = t // tm + N_BUCKETS
    counts = cnt[:N_BUCKETS, 0].astype(I32)
    tiles_b = (counts + tm - 1) // tm
    tile_end = jnp.cumsum(tiles_b)
    offs = (tile_end - tiles_b) * tm
    bucket = meta[4].astype(I32)
    dest = offs[bucket] + meta[5].astype(I32)

    tile = jnp.arange(ntiles, dtype=I32)
    valid = (tile < tile_end[-1]).astype(I32)
    tb = jnp.minimum(jnp.sum((tile[:, None] >= tile_end[None, :]).astype(I32), axis=1), N_BUCKETS - 1)
    lo = jnp.asarray(_PAIR_LO, I32)[tb % N_PAIRS] + EXPERTS_PER_GROUP * (tb // N_PAIRS)
    hi = jnp.asarray(_PAIR_HI, I32)[tb % N_PAIRS] + EXPERTS_PER_GROUP * (tb // N_PAIRS)
    odd = (tile % 2) == 1
    ex = jnp.stack([jnp.where(odd, hi, lo), jnp.where(odd, lo, hi)], axis=1).reshape(-1)

    xs = _scatter_call(dest, xp, ntiles * tm)
    ys = _moe_call(ex, valid, xs, w1, w3, w2)
    return _gather_ln_call(dest, ys, h1, gate2, lng, lnb, alpha)


def _hy_in_kernel(xm_ref, xp_ref, xn_ref, sc_ref, sh_ref, w_ref, b_ref, cw_ref, cb_ref, x0_ref, z_ref, *, nt, tn):
    i = pl.program_id(1)
    tm, d = xm_ref.shape
    sc, sh = 1.0 + sc_ref[...], sh_ref[...]
    u = jnp.concatenate([xp_ref[...] * sc + sh, xm_ref[...] * sc + sh, xn_ref[...] * sc + sh], axis=0).astype(BF16)
    rows = tm + 16
    rid = lax.broadcasted_iota(I32, (rows, 1), 0)
    keep = ((rid >= 8) | (i > 0)) & ((rid < tm + 8) | (i < nt - 1))

    def conv(sec, j):
        col = sec * d + j * tn
        p = jnp.dot(u, w_ref[:, col:col + tn], preferred_element_type=F32) + b_ref[:, col:col + tn]
        p = jnp.where(keep, p, 0.0)
        cw = cw_ref[:, col:col + tn]
        out = (cw[0:1] * pltpu.roll(p, 1, 0) + cw[1:2] * p + cw[2:3] * pltpu.roll(p, rows - 1, 0)
               + cb_ref[:, col:col + tn])
        return out[8:8 + tm]

    nch = d // tn
    for j in range(nch // 2):
        x0_ref[:, j * tn:(j + 1) * tn] = _pack_pair(conv(0, j), conv(0, j + nch // 2))
        z_ref[:, j * tn:(j + 1) * tn] = _pack_pair(conv(1, j) * conv(2, j),
                                                   conv(1, j + nch // 2) * conv(2, j + nch // 2))


def _hy_in_call(h, sc, sh, w, b, cw, cb):
    bsz, s, d = h.shape
    tm = min(TOKEN_TILE, s)
    nt = s // tm
    hb = tm // 8
    tn = 512
    kern = functools.partial(_hy_in_kernel, nt=nt, tn=tn)
    return pl.pallas_call(
        kern,
        out_shape=(jax.ShapeDtypeStruct((bsz, s, d // 2), U32), jax.ShapeDtypeStruct((bsz, s, d // 2), U32)),
        grid=(bsz, nt),
        in_specs=[
            pl.BlockSpec((None, tm, d), lambda b, i: (b, i, 0)),
            pl.BlockSpec((None, 8, d), lambda b, i: (b, jnp.maximum(i * hb - 1, 0), 0)),
            pl.BlockSpec((None, 8, d), lambda b, i: (b, jnp.minimum((i + 1) * hb, nt * hb - 1), 0)),
            pl.BlockSpec((None, 1, d), lambda b, i: (b, 0, 0)),
            pl.BlockSpec((None, 1, d), lambda b, i: (b, 0, 0)),
            _resident((d, 3 * d), lambda b, i: (0, 0)),
            pl.BlockSpec((1, 3 * d), lambda b, i: (0, 0)),
            pl.BlockSpec((3, 3 * d), lambda b, i: (0, 0)),
            pl.BlockSpec((1, 3 * d), lambda b, i: (0, 0)),
        ],
        out_specs=(pl.BlockSpec((None, tm, d // 2), lambda b, i: (b, i, 0)),
                   pl.BlockSpec((None, tm, d // 2), lambda b, i: (b, i, 0))),
        compiler_params=_cparams("arbitrary", "arbitrary"),
        name="hyena_in",
    )(h, h, h, sc, sh, w, b, cw, cb)


def _filter_kernel(w1_ref, b1_ref, w2_ref, b2_ref, w3_ref, b3_ref, fr_ref, w4_ref, dl_ref, f_ref, l1_ref,
                   *, seq, tn):
    i = pl.program_id(0)

    @pl.when(i == 0)
    def _():
        l1_ref[...] = jnp.zeros_like(l1_ref)

    n_lane = i * tn + lax.broadcasted_iota(I32, (1, tn), 1)
    m_lane = jnp.where(n_lane < seq, n_lane, 2 * seq - n_lane).astype(F32)
    t_lane = m_lane / (seq - 1.0)
    wl = (2.0 * math.pi) * m_lane / float(seq)
    band = lax.broadcasted_iota(I32, (HY_BANDS, 1), 0).astype(F32)
    fb = 1e-4 + band * ((HY_BANDS - 1 - 1e-4) / (HY_BANDS - 1))
    ang = fb * wl
    z = jnp.concatenate([t_lane, jnp.cos(ang), -jnp.sin(ang),
                         jnp.zeros((7, tn), F32)], axis=0)
    fr = fr_ref[...]

    def layer(w_ref, b_ref, x):
        pre = lax.dot_general(w_ref[...], x, (((0,), (0,)), ((), ())), preferred_element_type=F32,
                              precision=HIGHEST)
        return jnp.sin(fr * (pre + b_ref[...]))

    hdn = layer(w1_ref, b1_ref, z)
    hdn = layer(w2_ref, b2_ref, hdn)
    hdn = layer(w3_ref, b3_ref, hdn)
    h = lax.dot_general(hdn, w4_ref[...], (((0,), (0,)), ((), ())), preferred_element_type=F32,
                        precision=HIGHEST)
    n_col = i * tn + lax.broadcasted_iota(I32, (tn, 1), 0)
    m_col = jnp.where(n_col < seq, n_col, 2 * seq - n_col).astype(F32)
    h = h * jnp.exp(-(m_col / (seq - 1.0)) * dl_ref[...])
    h = jnp.where(n_col == seq, 0.0, h)
    l1_ref[...] += jnp.sum(jnp.abs(h), axis=0, keepdims=True)
    f_ref[...] = _pack_cols(h)


def _filter_call(fw1, fb1, fw2, fb2, fw3, fb3, freq, fw4, seq):
    width = fw2.shape[0]
    d = fw4.shape[1] // 2
    tn = min(512, seq)
    steps = 2 * seq // tn
    max_decay = math.log(HY_TARGET) / HY_FAST_DECAY
    min_decay = math.log(HY_TARGET) / HY_SLOW_DECAY
    deltas = jnp.abs(jnp.linspace(min_decay, max_decay, d, dtype=F32)).reshape(1, d)
    w1p = jnp.concatenate([fw1, jnp.zeros((7, width), F32)], axis=0)
    col = lambda a: a.reshape(width, 1)
    small = lambda shp: pl.BlockSpec(shp, lambda i: (0, 0))
    kern = functools.partial(_filter_kernel, seq=seq, tn=tn)
    return pl.pallas_call(
        kern,
        out_shape=(jax.ShapeDtypeStruct((2 * seq, d // 2), U32), jax.ShapeDtypeStruct((1, d), F32)),
        grid=(steps,),
        in_specs=[small((40, width)), small((width, 1)), small((width, width)), small((width, 1)),
                  small((width, width)), small((width, 1)), small((width, 1)),
                  pl.BlockSpec((width, d), lambda i: (0, (i * tn) // seq)),
                  small((1, d))],
        out_specs=(pl.BlockSpec((tn, d // 2), lambda i: (i, 0)), small((1, d))),
        compiler_params=_cparams("arbitrary"),
        name="hyena_filter",
    )(w1p, col(fb1), fw2, col(fb2), fw3, col(fb3), col(freq), fw4, deltas)


def _dft_tables(n1, n2):
    n = n1 * n2
    k = np.arange(n1)[:, None]
    m = np.arange(n1)[None, :]
    ang1 = -2.0 * np.pi * ((k * m) % n1) / n1
    f1r, f1i = np.cos(ang1), np.sin(ang1)
    hn = n1 // 2
    a_data = np.block([[f1r[:, :hn], -f1i[:, :hn]], [f1i[:, :hn], f1r[:, :hn]]])
    a_filt = np.concatenate([f1r, f1i], axis=0)
    a_inv = np.block([[f1r.T[:hn], f1i.T[:hn]], [-f1i.T[:hn], f1r.T[:hn]]]) / n
    k2 = np.arange(n2)[:, None]
    m2 = np.arange(n2)[None, :]
    ang2 = -2.0 * np.pi * ((k2 * m2) % n2) / n2
    f2r, f2i = np.cos(ang2), np.sin(ang2)
    b_fwd = np.block([[f2r, -f2i], [f2i, f2r]])
    b_inv = np.block([[f2r, f2i], [-f2i, f2r]])
    angt = -2.0 * np.pi * ((np.arange(n2)[:, None] * np.arange(n1)[None, :]) % n) / n
    tw = (np.cos(angt), np.sin(angt))
    perm = (np.arange(n1 // FFT_K1_GROUP)[None, :] * FFT_K1_GROUP + np.arange(FFT_K1_GROUP)[:, None]).reshape(-1)
    perm2 = np.concatenate([perm, perm + n1])
    as_bf16 = lambda a: jnp.asarray(a, F32).astype(BF16)
    return dict(a_data=as_bf16(a_data[perm2]), a_filt=as_bf16(a_filt[perm2]), a_inv=as_bf16(a_inv[:, perm2]),
                b_fwd=as_bf16(b_fwd), b_inv=as_bf16(b_inv),
                tw_n2=tuple(jnp.asarray(t[:, perm], F32).reshape(n2, n1, 1) for t in tw),
                tw_k1=tuple(jnp.asarray(t.T.copy(), F32).reshape(n1, n2, 1) for t in tw))


FFT_K1_GROUP = 4


def _pack_pair(hi, lo):
    hb = lax.bitcast_convert_type(hi.astype(BF16).astype(F32), U32)
    lb = lax.bitcast_convert_type(lo.astype(BF16).astype(F32), U32)
    return (hb & jnp.uint32(0xFFFF0000)) | (lb >> 16)


def _unpack_pair(w):
    hi = lax.bitcast_convert_type(w & jnp.uint32(0xFFFF0000), F32)
    lo = lax.bitcast_convert_type(w << 16, F32)
    return hi, lo


def _fetch_step(view, buf, sem):
    k = pl.program_id(0)
    slot = k % 2

    def copy(step, sl):
        return pltpu.make_async_copy(view(step), buf.at[sl], sem.at[sl])

    @pl.when(k == 0)
    def _():
        copy(0, 0).start()

    @pl.when(k + 1 < pl.num_programs(0))
    def _():
        copy(k + 1, 1 - slot).start()

    copy(k, slot).wait()
    return buf.at[slot]


def _store_step(view, buf, sem, fill):
    k = pl.program_id(0)
    slot = k % 2

    def copy(step, sl):
        return pltpu.make_async_copy(buf.at[sl], view(step), sem.at[sl])

    @pl.when(k >= 2)
    def _():
        copy(k - 2, slot).wait()

    fill(buf.at[slot])
    copy(k, slot).start()

    @pl.when(k == pl.num_programs(0) - 1)
    def _():
        copy(k, slot).wait()

        @pl.when(k >= 1)
        def _():
            copy(k - 1, 1 - slot).wait()


def _unpack_cols(w):
    hi, lo = _unpack_pair(w)
    return jnp.concatenate([hi, lo], axis=1)


def _pack_cols(x):
    half = x.shape[1] // 2
    return _pack_pair(x[:, :half], x[:, half:])


def _fft_a_kernel(x_hbm, a_ref, twr_ref, twi_ref, y_ref, xbuf, sem):
    n1 = a_ref.shape[0] // 2
    x_ref = _fetch_step(lambda j: x_hbm.at[:, :, j, :], xbuf, sem)
    rhs = jnp.concatenate([_unpack_cols(x_ref[0]), _unpack_cols(x_ref[1])], axis=0).astype(BF16)
    y = jnp.dot(a_ref[...], rhs, preferred_element_type=F32)
    yr, yi = y[:n1], y[n1:]
    tr, ti = twr_ref[...], twi_ref[...]
    w = _pack_pair(yr * tr - yi * ti, yr * ti + yi * tr)
    q = n1 // FFT_K1_GROUP
    d = w.shape[1]
    for kk in range(FFT_K1_GROUP):
        y_ref[:, kk * d:(kk + 1) * d] = w[kk * q:(kk + 1) * q]


def _fft_a_call(x4, a_mat, tw_n2):
    _, hn, n2, half = x4.shape
    d = 2 * half
    n1 = 2 * hn
    q = n1 // FFT_K1_GROUP
    return pl.pallas_call(
        _fft_a_kernel,
        out_shape=jax.ShapeDtypeStruct((n2, q, FFT_K1_GROUP * d), U32),
        grid=(n2,),
        in_specs=[
            pl.BlockSpec(memory_space=pl.ANY),
            pl.BlockSpec((2 * n1, n1), lambda j: (0, 0)),
            pl.BlockSpec((None, n1, 1), lambda j: (j, 0, 0)),
            pl.BlockSpec((None, n1, 1), lambda j: (j, 0, 0)),
        ],
        out_specs=pl.BlockSpec((None, q, FFT_K1_GROUP * d), lambda j: (j, 0, 0)),
        scratch_shapes=[pltpu.VMEM((2, 2, hn, half), U32), pltpu.SemaphoreType.DMA((2,))],
        compiler_params=_cparams("arbitrary"),
        name="fft_stage_a",
    )(x4, a_mat, tw_n2[0], tw_n2[1])


def _load_k1(y_ref, kk):
    d = y_ref.shape[1] // FFT_K1_GROUP
    yr, yi = _unpack_pair(y_ref[:, kk * d:(kk + 1) * d])
    return jnp.concatenate([yr, yi], axis=0).astype(BF16)


def _fft_b_kernel(y_hbm, b_ref, h_ref, ybuf, sem):
    y_ref = _fetch_step(lambda kb: y_hbm.at[:, kb, :], ybuf, sem)
    n2 = y_ref.shape[0]
    for kk in range(FFT_K1_GROUP):
        x = jnp.dot(b_ref[...], _load_k1(y_ref, kk), preferred_element_type=F32)
        h_ref[kk] = _pack_pair(x[:n2], x[n2:])


def _fft_b_call(y, b_fwd):
    n2, q, gd = y.shape
    d = gd // FFT_K1_GROUP
    return pl.pallas_call(
        _fft_b_kernel,
        out_shape=jax.ShapeDtypeStruct((q * FFT_K1_GROUP, n2, d), U32),
        grid=(q,),
        in_specs=[pl.BlockSpec(memory_space=pl.ANY),
                  pl.BlockSpec((2 * n2, 2 * n2), lambda k: (0, 0))],
        out_specs=pl.BlockSpec((FFT_K1_GROUP, n2, d), lambda k: (k, 0, 0)),
        scratch_shapes=[pltpu.VMEM((2, n2, gd), U32), pltpu.SemaphoreType.DMA((2,))],
        compiler_params=_cparams("arbitrary"),
        name="fft_filter_b",
    )(y, b_fwd)


def _fft_bc_kernel(y_hbm, h_ref, bf_ref, bi_ref, twr_ref, twi_ref, g_hbm, ybuf, gbuf, sem_in, sem_out):
    y_ref = _fetch_step(lambda kb: y_hbm.at[:, kb, :], ybuf, sem_in)
    n2 = y_ref.shape[0]
    d = y_ref.shape[1] // FFT_K1_GROUP

    def fill(g_ref):
        for kk in range(FFT_K1_GROUP):
            x = jnp.dot(bf_ref[...], _load_k1(y_ref, kk), preferred_element_type=F32)
            xr, xi = x[:n2], x[n2:]
            hr, hi = _unpack_pair(h_ref[kk])
            z = jnp.concatenate([xr * hr - xi * hi, xr * hi + xi * hr], axis=0).astype(BF16)
            g = jnp.dot(bi_ref[...], z, preferred_element_type=F32)
            gr, gi = g[:n2], g[n2:]
            tr, ti = twr_ref[kk], twi_ref[kk]
            g_ref[:, kk * d:(kk + 1) * d] = _pack_pair(gr * tr + gi * ti, gi * tr - gr * ti)

    _store_step(lambda kb: g_hbm.at[:, kb, :], gbuf, sem_out, fill)


def _fft_bc_call(y, hspec, b_fwd, b_inv, tw_k1):
    n2, q, gd = y.shape
    d = gd // FFT_K1_GROUP
    mat = lambda: pl.BlockSpec((2 * n2, 2 * n2), lambda k: (0, 0))
    tw = lambda: pl.BlockSpec((FFT_K1_GROUP, n2, 1), lambda k: (k, 0, 0))
    return pl.pallas_call(
        _fft_bc_kernel,
        out_shape=jax.ShapeDtypeStruct((n2, q, gd), U32),
        grid=(q,),
        in_specs=[pl.BlockSpec(memory_space=pl.ANY),
                  pl.BlockSpec((FFT_K1_GROUP, n2, d), lambda k: (k, 0, 0)), mat(), mat(), tw(), tw()],
        out_specs=pl.BlockSpec(memory_space=pl.ANY),
        scratch_shapes=[pltpu.VMEM((2, n2, gd), U32), pltpu.VMEM((2, n2, gd), U32),
                        pltpu.SemaphoreType.DMA((2,)), pltpu.SemaphoreType.DMA((2,))],
        compiler_params=_cparams("arbitrary"),
        name="fft_stage_bc",
    )(y, hspec, b_fwd, b_inv, tw_k1[0], tw_k1[1])


def _fft_d_kernel(g_ref, a_ref, x0_hbm, z_hbm, l1_ref, fb_ref, o_hbm, x0buf, zbuf, obuf, sem_x, sem_z, sem_o):
    hn = a_ref.shape[0] // 2
    d = g_ref.shape[1] // FFT_K1_GROUP
    x0_ref = _fetch_step(lambda j: x0_hbm.at[:, :, j, :], x0buf, sem_x)
    z_ref = _fetch_step(lambda j: z_hbm.at[:, :, j, :], zbuf, sem_z)
    parts = [_unpack_pair(g_ref[:, kk * d:(kk + 1) * d]) for kk in range(FFT_K1_GROUP)]
    rhs = jnp.concatenate([p[0] for p in parts] + [p[1] for p in parts], axis=0).astype(BF16)
    y = jnp.dot(a_ref[...], rhs, preferred_element_type=F32)
    inv_l1 = 1.0 / l1_ref[...]
    fb = fb_ref[...]

    def fill(o_ref):
        for b in range(2):
            conv = y[b * hn:(b + 1) * hn] * inv_l1
            o_ref[b] = _pack_cols(_unpack_cols(x0_ref[b]) * (conv + fb * _unpack_cols(z_ref[b])))

    _store_step(lambda j: o_hbm.at[:, :, j, :], obuf, sem_o, fill)


def _fft_d_call(g, a_inv, x0_4, z_4, l1, fbias):
    n2, q, gd = g.shape
    d = gd // FFT_K1_GROUP
    n1 = q * FFT_K1_GROUP
    hn = n1 // 2
    half = d // 2
    tok_buf = lambda: pltpu.VMEM((2, 2, hn, half), U32)
    return pl.pallas_call(
        _fft_d_kernel,
        out_shape=jax.ShapeDtypeStruct((2, hn, n2, half), U32),
        grid=(n2,),
        in_specs=[pl.BlockSpec((None, q, gd), lambda j: (j, 0, 0)),
                  pl.BlockSpec((n1, 2 * n1), lambda j: (0, 0)),
                  pl.BlockSpec(memory_space=pl.ANY), pl.BlockSpec(memory_space=pl.ANY),
                  pl.BlockSpec((1, d), lambda j: (0, 0)),
                  pl.BlockSpec((1, d), lambda j: (0, 0))],
        out_specs=pl.BlockSpec(memory_space=pl.ANY),
        scratch_shapes=[tok_buf(), tok_buf(), tok_buf(),
                        pltpu.SemaphoreType.DMA((2,)), pltpu.SemaphoreType.DMA((2,)), pltpu.SemaphoreType.DMA((2,))],
        compiler_params=_cparams("arbitrary"),
        name="fft_stage_d",
    )(g, a_inv, x0_4, z_4, l1, fbias)


def _hyena_conv(x0p, zp, filtp, l1, fbias):
    bsz, seq, half = zp.shape
    assert bsz == 2, "the two batch rows are packed as one complex signal"
    n2 = FFT_N2
    n1 = 2 * seq // n2
    tabs = _dft_tables(n1, n2)
    hn = n1 // 2
    view = lambda a: a.reshape(2, hn, n2, half)
    hspec = _fft_b_call(_fft_a_call(view(filtp), tabs["a_filt"], tabs["tw_n2"]), tabs["b_fwd"])
    y = _fft_a_call(view(zp), tabs["a_data"], tabs["tw_n2"])
    g = _fft_bc_call(y, hspec, tabs["b_fwd"], tabs["b_inv"], tabs["tw_k1"])
    out = _fft_d_call(g, tabs["a_inv"], view(x0p), view(zp), l1, fbias)
    return out.reshape(bsz, seq, half)


def _dup_heads(w, n_kv, head_dim):
    lead = w.shape[:-1]
    w = w.reshape(lead + (n_kv, head_dim))
    return jnp.concatenate([w, w], axis=-1).reshape(lead + (2 * n_kv * head_dim,))


def _rope_tables(seq, head_dim):
    axis_dim = head_dim // 2
    rows = seq // GRID_W
    inv = ROPE_BASE ** (-jnp.arange(0, axis_dim, 2, dtype=F32) / axis_dim)
    row = jnp.repeat(jnp.arange(rows, dtype=F32), GRID_W)[:, None] * inv
    col = jnp.tile(jnp.arange(GRID_W, dtype=F32), rows)[:, None] * inv
    quarter = axis_dim // 2
    cos = jnp.concatenate([jnp.cos(row), jnp.cos(row), jnp.cos(col), jnp.cos(col)], axis=1)
    sin = jnp.concatenate([jnp.sin(row), jnp.sin(row), jnp.sin(col), jnp.sin(col)], axis=1)
    second = (np.arange(head_dim) % axis_dim) >= quarter
    reps = LANES // head_dim
    cos = jnp.tile(cos, (1, reps))
    sin = jnp.tile(sin, (1, reps))
    second = jnp.asarray(np.tile(second, reps))[None, :]
    return cos, jnp.where(second, sin, 0.0), jnp.where(second, 0.0, -sin)


def kernel(x, c, ctx, c_ctx, ada_w, ada_b, attn_w_in, attn_b_in, attn_sink, attn_w_out, hy_w_in, hy_b_in, hy_conv_w, hy_conv_b, hy_f_w1, hy_f_b1, hy_f_w2, hy_f_b2, hy_f_w3, hy_f_b3, hy_f_freq, hy_f_w4, hy_f_bias, hy_w_out, hy_b_out, ln1_g, ln1_b, ln2_g, ln2_b, router_w, router_b, moe_w1, moe_w3, moe_w2):
    bsz, seq, d = x.shape
    depth = ada_w.shape[0]
    assert depth == 2 and attn_w_in.shape[0] == 1 and hy_w_in.shape[0] == 1
    alpha = (2 * depth) ** 0.25
    n_heads = attn_sink.shape[1]
    attn_dim = attn_w_out.shape[1]
    head_dim = attn_dim // n_heads
    kv_dim = (attn_w_in.shape[2] - attn_dim) // 2
    n_kv = kv_dim // head_dim
    group = n_heads // n_kv
    assert head_dim * 2 == LANES and group % 2 == 0
    n_exp = router_w.shape[1]
    assert n_exp == N_GROUPS * EXPERTS_PER_GROUP

    cond = jnp.concatenate([c, c_ctx[None, :], jnp.zeros((8 - bsz - 1, d), F32)], axis=0)
    mods = _ada_call(cond, ada_w, ada_b).reshape(depth, 8, 6, d)
    mod = lambda layer, k: mods[layer, :bsz, k].reshape(bsz, 1, d)
    cmod = lambda layer, k: mods[layer, bsz, k].reshape(1, d)
    row = lambda v: v.reshape(1, -1)

    rw_pad = jnp.concatenate([router_w, jnp.zeros((d, LANES - n_exp), F32)], axis=1).astype(BF16)
    rb = router_b.reshape(n_exp, 1)
    w1b, w3b, w2b = moe_w1.astype(BF16), moe_w3.astype(BF16), moe_w2.astype(BF16)

    w_in, b_in = attn_w_in[0], attn_b_in[0]
    wq, wk, wv = w_in[:, :attn_dim], w_in[:, attn_dim:attn_dim + kv_dim], w_in[:, attn_dim + kv_dim:]
    bq, bk, bv = b_in[:attn_dim], b_in[attn_dim:attn_dim + kv_dim], b_in[attn_dim + kv_dim:]
    dup = lambda a: _dup_heads(a, n_kv, head_dim)
    w_kv = jnp.concatenate([dup(wk), dup(wv)], axis=1).astype(BF16)
    b_kv = row(jnp.concatenate([dup(bk), dup(bv)]))
    w_ext = jnp.concatenate([wq.astype(BF16), w_kv], axis=1)
    b_ext = jnp.concatenate([row(bq), b_kv], axis=1)
    n_kd = 2 * kv_dim
    cos_t, sa_t, sb_t = _rope_tables(seq, head_dim)

    q, kd, vd = _qkv_call(x, mod(0, 1), mod(0, 0), w_ext, b_ext, cos_t, sa_t, sb_t, attn_dim, n_kd, head_dim)
    kxd, vxd = _ctx_kv_call(ctx, cmod(0, 1), cmod(0, 0), w_kv, b_kv, n_kd)
    att = _attn_call(attn_sink[0], q, kd, vd, kxd, vxd, n_kv, group)
    h1, xp, meta, cnt = _proj_ln_call(att, attn_w_out[0].astype(BF16), jnp.zeros((1, d), F32), x, mod(0, 2),
                                      row(ln1_g[0]), row(ln1_b[0]), mod(0, 4), mod(0, 3), rw_pad, rb, alpha)
    h = _moe_layer(xp, meta, cnt, h1, mod(0, 5), row(ln2_g[0]), row(ln2_b[0]), w1b[0], w3b[0], w2b[0], alpha)

    x0, z = _hy_in_call(h, mod(1, 1), mod(1, 0), hy_w_in[0].astype(BF16), row(hy_b_in[0]), hy_conv_w[0],
                        row(hy_conv_b[0]))
    filt, l1 = _filter_call(hy_f_w1[0], hy_f_b1[0], hy_f_w2[0], hy_f_b2[0], hy_f_w3[0], hy_f_b3[0],
                            hy_f_freq[0], hy_f_w4[0], seq)
    yh = _hyena_conv(x0, z, filt, l1, row(hy_f_bias[0]))
    h1, xp, meta, cnt = _proj_ln_call(yh, hy_w_out[0].astype(BF16), row(hy_b_out[0]), h, mod(1, 2),
                                      row(ln1_g[1]), row(ln1_b[1]), mod(1, 4), mod(1, 3), rw_pad, rb, alpha)
    return _moe_layer(xp, meta, cnt, h1, mod(1, 5), row(ln2_g[1]), row(ln2_b[1]), w1b[1], w3b[1], w2b[1], alpha)
```

```python
import functools
import math

import numpy as np
import jax
import jax.numpy as jnp
from jax import lax
from jax.experimental import pallas as pl
from jax.experimental.pallas import tpu as pltpu

F32 = jnp.float32
BF16 = jnp.bfloat16
I32 = jnp.int32
U32 = jnp.uint32
HIGHEST = lax.Precision.HIGHEST

LANES = 128
V7X_VMEM_LIMIT_BYTES = 56 * 1024 * 1024

GRID_W = 64
BLOCK = 128
ROPE_BASE = 10000.0
NEG_INF = -1e30
HY_BANDS = 16
HY_FAST_DECAY = 0.3
HY_SLOW_DECAY = 1.5
HY_TARGET = 1e-2
N_GROUPS = 4
EXPERTS_PER_GROUP = 4
N_PAIRS = 6
N_BUCKETS = N_GROUPS * N_PAIRS
BUCKET_ROWS = 32
LN_EPS = 1e-5
FFT_N2 = 128

TOKEN_TILE = 512


def _cparams(*sem):
    return pltpu.CompilerParams(dimension_semantics=sem, vmem_limit_bytes=V7X_VMEM_LIMIT_BYTES)


def _resident(block_shape, index_map):
    return pl.BlockSpec(block_shape, index_map, pipeline_mode=pl.Buffered(1))


def _silu(x):
    return x * jax.nn.sigmoid(x)


def _ada_kernel(c_ref, w_ref, b_ref, o_ref):
    c = _silu(c_ref[...])
    o_ref[...] = jnp.dot(c, w_ref[...], preferred_element_type=F32, precision=HIGHEST) + b_ref[...]


def _ada_call(cond, ada_w, ada_b):
    depth, d, n6 = ada_w.shape
    tn = 1024
    rows = cond.shape[0]
    return pl.pallas_call(
        _ada_kernel,
        out_shape=jax.ShapeDtypeStruct((depth, rows, n6), F32),
        grid=(depth, n6 // tn),
        in_specs=[
            pl.BlockSpec((rows, d), lambda l, j: (0, 0)),
            pl.BlockSpec((None, d, tn), lambda l, j: (l, 0, j)),
            pl.BlockSpec((None, 1, tn), lambda l, j: (l, 0, j)),
        ],
        out_specs=pl.BlockSpec((None, rows, tn), lambda l, j: (l, 0, j)),
        compiler_params=_cparams("arbitrary", "arbitrary"),
        name="ada_mod",
    )(cond, ada_w, ada_b.reshape(depth, 1, n6))


def _qkv_kernel(x_ref, sc_ref, sh_ref, w_ref, b_ref, cos_ref, sa_ref, sb_ref, q_ref, k_ref, v_ref, *, scale):
    u = (x_ref[...] * (1.0 + sc_ref[...]) + sh_ref[...]).astype(BF16)
    p = jnp.dot(u, w_ref[...], preferred_element_type=F32) + b_ref[...]
    cos, sa, sb = cos_ref[...], sa_ref[...], sb_ref[...]
    nq = q_ref.shape[1]
    nk = k_ref.shape[1]

    def rope(xc):
        return xc * cos + pltpu.roll(xc, 16, 1) * sa + pltpu.roll(xc, LANES - 16, 1) * sb

    for c in range(nq // LANES):
        q_ref[:, c * LANES:(c + 1) * LANES] = (rope(p[:, c * LANES:(c + 1) * LANES]) * scale).astype(BF16)
    nkv = nk // 2
    for c in range(nkv // LANES):
        _store_dup_heads(k_ref, c, rope(p[:, nq + c * LANES:nq + (c + 1) * LANES]))
        _store_dup_heads(v_ref, c, p[:, nq + nkv + c * LANES:nq + nkv + (c + 1) * LANES])


def _store_dup_heads(ref, c, pair):
    lo = lax.broadcasted_iota(I32, (1, LANES), 1) < (LANES // 2)
    swapped = pltpu.roll(pair, LANES // 2, 1)
    ref[:, (2 * c) * LANES:(2 * c + 1) * LANES] = jnp.where(lo, pair, swapped).astype(ref.dtype)
    ref[:, (2 * c + 1) * LANES:(2 * c + 2) * LANES] = jnp.where(lo, swapped, pair).astype(ref.dtype)


def _qkv_call(x, sc, sh, w_ext, b_ext, cos_t, sa_t, sb_t, n_q, n_kd, head_dim):
    bsz, s, d = x.shape
    tm = min(TOKEN_TILE, s)
    n_out = w_ext.shape[1]
    kern = functools.partial(_qkv_kernel, scale=head_dim ** -0.5)
    return pl.pallas_call(
        kern,
        out_shape=(jax.ShapeDtypeStruct((bsz, s, n_q), BF16),
                   jax.ShapeDtypeStruct((bsz, s, n_kd), BF16),
                   jax.ShapeDtypeStruct((bsz, s, n_kd), BF16)),
        grid=(bsz, s // tm),
        in_specs=[
            pl.BlockSpec((None, tm, d), lambda b, i: (b, i, 0)),
            pl.BlockSpec((None, 1, d), lambda b, i: (b, 0, 0)),
            pl.BlockSpec((None, 1, d), lambda b, i: (b, 0, 0)),
            _resident((d, n_out), lambda b, i: (0, 0)),
            pl.BlockSpec((1, n_out), lambda b, i: (0, 0)),
            pl.BlockSpec((tm, LANES), lambda b, i: (i, 0)),
            pl.BlockSpec((tm, LANES), lambda b, i: (i, 0)),
            pl.BlockSpec((tm, LANES), lambda b, i: (i, 0)),
        ],
        out_specs=(pl.BlockSpec((None, tm, n_q), lambda b, i: (b, i, 0)),
                   pl.BlockSpec((None, tm, n_kd), lambda b, i: (b, i, 0)),
                   pl.BlockSpec((None, tm, n_kd), lambda b, i: (b, i, 0))),
        compiler_params=_cparams("arbitrary", "arbitrary"),
        name="attn_qkv",
    )(x, sc, sh, w_ext, b_ext, cos_t, sa_t, sb_t)


def _ctx_kv_kernel(x_ref, sc_ref, sh_ref, w_ref, b_ref, k_ref, v_ref):
    u = (x_ref[...] * (1.0 + sc_ref[...]) + sh_ref[...]).astype(BF16)
    p = jnp.dot(u, w_ref[...], preferred_element_type=F32) + b_ref[...]
    nkv = k_ref.shape[1] // 2
    for c in range(nkv // LANES):
        _store_dup_heads(k_ref, c, p[:, c * LANES:(c + 1) * LANES])
        _store_dup_heads(v_ref, c, p[:, nkv + c * LANES:nkv + (c + 1) * LANES])


def _ctx_kv_call(ctx, csc, csh, w_kv, b_kv, n_kd):
    bsz, c, d = ctx.shape
    return pl.pallas_call(
        _ctx_kv_kernel,
        out_shape=(jax.ShapeDtypeStruct((bsz, c, n_kd), BF16), jax.ShapeDtypeStruct((bsz, c, n_kd), BF16)),
        grid=(bsz,),
        in_specs=[
            pl.BlockSpec((None, c, d), lambda b: (b, 0, 0)),
            pl.BlockSpec((1, d), lambda b: (0, 0)),
            pl.BlockSpec((1, d), lambda b: (0, 0)),
            pl.BlockSpec((d, n_kd), lambda b: (0, 0)),
            pl.BlockSpec((1, n_kd), lambda b: (0, 0)),
        ],
        out_specs=(pl.BlockSpec((None, c, n_kd), lambda b: (b, 0, 0)),
                   pl.BlockSpec((None, c, n_kd), lambda b: (b, 0, 0))),
        compiler_params=_cparams("arbitrary"),
        name="attn_ctx_kv",
    )(ctx, csc, csh, w_kv, b_kv)


def _attn_kernel(sink_ref, q_ref, kp_ref, kc_ref, kn_ref, vp_ref, vc_ref, vn_ref, kx_ref, vx_ref, o_ref,
                 *, n_kv, group, nb):
    n = pl.program_id(1)
    r = lax.broadcasted_iota(I32, (1, BLOCK, BLOCK), 1)
    j = lax.broadcasted_iota(I32, (1, BLOCK, BLOCK), 2)
    prev_ok = (j >= r) & (n > 0)
    next_ok = (j <= r) & (n < nb - 1)
    lo = lax.broadcasted_iota(I32, (1, LANES), 1) < (LANES // 2)
    pairs = group // 2
    head = lax.broadcasted_iota(I32, (group, 1, 1), 0)
    for kh in range(n_kv):
        sl = slice(kh * LANES, (kh + 1) * LANES)
        kcat = jnp.concatenate([kp_ref[:, sl], kc_ref[:, sl], kn_ref[:, sl], kx_ref[:, sl]], axis=0)
        vcat = jnp.concatenate([vp_ref[:, sl], vc_ref[:, sl], vn_ref[:, sl], vx_ref[:, sl]], axis=0)
        nkeys = kcat.shape[0]
        parts = []
        for pp in range(pairs):
            q2 = q_ref[:, (kh * pairs + pp) * LANES:(kh * pairs + pp + 1) * LANES]
            zq = jnp.zeros_like(q2)
            parts += [jnp.where(lo, q2, zq), jnp.where(lo, zq, q2)]
        qs = jnp.concatenate(parts, axis=0)
        s = lax.dot_general(qs, kcat, (((1,), (1,)), ((), ())), preferred_element_type=F32)
        s = s.reshape(group, BLOCK, nkeys)
        s = jnp.concatenate([
            jnp.where(prev_ok, s[:, :, :BLOCK], NEG_INF),
            s[:, :, BLOCK:2 * BLOCK],
            jnp.where(next_ok, s[:, :, 2 * BLOCK:3 * BLOCK], NEG_INF),
            s[:, :, 3 * BLOCK:]], axis=2)
        sk = jnp.zeros((group, 1, 1), F32)
        for g in range(group):
            sk = jnp.where(head == g, sink_ref[kh * group + g], sk)
        m = jnp.maximum(jnp.max(s, axis=2, keepdims=True), sk)
        e = jnp.exp(s - m)
        rden = 1.0 / (jnp.sum(e, axis=2, keepdims=True) + jnp.exp(sk - m))
        o = jnp.dot(e.reshape(group * BLOCK, nkeys).astype(BF16), vcat, preferred_element_type=F32)
        o = o.reshape(group, BLOCK, LANES) * rden
        for pp in range(pairs):
            p = kh * pairs + pp
            o_ref[:, p * LANES:(p + 1) * LANES] = jnp.where(lo, o[2 * pp], o[2 * pp + 1]).astype(BF16)


def _attn_call(sink, q, kd, vd, kxd, vxd, n_kv, group):
    bsz, s, n_q = q.shape
    n_kd = kd.shape[2]
    c = kxd.shape[1]
    nb = s // BLOCK
    kern = functools.partial(_attn_kernel, n_kv=n_kv, group=group, nb=nb)
    prev = lambda b, n: (b, jnp.maximum(n - 1, 0), 0)
    cur = lambda b, n: (b, n, 0)
    nxt = lambda b, n: (b, jnp.minimum(n + 1, nb - 1), 0)
    kv = lambda im: pl.BlockSpec((None, BLOCK, n_kd), im)
    return pl.pallas_call(
        kern,
        out_shape=jax.ShapeDtypeStruct((bsz, s, n_q), BF16),
        grid=(bsz, nb),
        in_specs=[
            pl.BlockSpec(memory_space=pltpu.SMEM),
            pl.BlockSpec((None, BLOCK, n_q), cur),
            kv(prev), kv(cur), kv(nxt), kv(prev), kv(cur), kv(nxt),
            pl.BlockSpec((None, c, n_kd), lambda b, n: (b, 0, 0)),
            pl.BlockSpec((None, c, n_kd), lambda b, n: (b, 0, 0)),
        ],
        out_specs=pl.BlockSpec((None, BLOCK, n_q), cur),
        compiler_params=_cparams("arbitrary", "arbitrary"),
        name="window_attn",
    )(sink, q, kd, kd, kd, vd, vd, vd, kxd, vxd)


def _layer_norm(r, g, b):
    mu = jnp.mean(r, axis=-1, keepdims=True)
    xc = r - mu
    var = jnp.mean(xc * xc, axis=-1, keepdims=True)
    return xc * lax.rsqrt(var + LN_EPS) * g + b


def _route(logits_t, rb):
    s = jax.nn.sigmoid(logits_t)
    sel = s + rb
    n_e = N_GROUPS * EXPERTS_PER_GROUP
    sel_r = [sel[e:e + 1, :] for e in range(n_e)]
    s_r = [s[e:e + 1, :] for e in range(n_e)]
    gscore = []
    for g in range(N_GROUPS):
        a, b, c, d = sel_r[4 * g:4 * g + 4]
        m1, n1, m2, n2 = jnp.maximum(a, b), jnp.minimum(a, b), jnp.maximum(c, d), jnp.minimum(c, d)
        gscore.append(jnp.maximum(m1, m2) + jnp.maximum(jnp.minimum(m1, m2), jnp.maximum(n1, n2)))
    best, gi = gscore[0], jnp.zeros_like(gscore[0], dtype=I32)
    for g in range(1, N_GROUPS):
        upd = gscore[g] > best
        gi = jnp.where(upd, g, gi)
        best = jnp.where(upd, gscore[g], best)

    def pick(rows, i):
        out = rows[i]
        for g in range(1, N_GROUPS):
            out = jnp.where(gi == g, rows[4 * g + i], out)
        return out

    v = [pick(sel_r, i) for i in range(EXPERTS_PER_GROUP)]
    sv = [pick(s_r, i) for i in range(EXPERTS_PER_GROUP)]

    def argmax4(vals):
        bv, bi = vals[0], jnp.zeros_like(gi)
        for i in range(1, EXPERTS_PER_GROUP):
            upd = vals[i] > bv
            bi = jnp.where(upd, i, bi)
            bv = jnp.where(upd, vals[i], bv)
        return bi

    def take4(vals, idx):
        out = vals[0]
        for i in range(1, EXPERTS_PER_GROUP):
            out = jnp.where(idx == i, vals[i], out)
        return out

    i1 = argmax4(v)
    i2 = argmax4([jnp.where(i1 == i, -jnp.inf, v[i]) for i in range(EXPERTS_PER_GROUP)])
    s1, s2 = take4(sv, i1), take4(sv, i2)
    tot = s1 + s2
    g1, g2 = s1 / tot, s2 / tot
    first_lo = i1 < i2
    i_lo, i_hi = jnp.minimum(i1, i2), jnp.maximum(i1, i2)
    g_lo, g_hi = jnp.where(first_lo, g1, g2), jnp.where(first_lo, g2, g1)
    pair = jnp.where(i_lo == 0, i_hi - 1, jnp.where(i_lo == 1, i_hi + 1, N_PAIRS - 1))
    bucket = gi * N_PAIRS + pair
    return 4 * gi + i_lo, 4 * gi + i_hi, g_lo, g_hi, bucket


def _proj_ln_kernel(a_ref, w_ref, bias_ref, h_ref, gate_ref, lng_ref, lnb_ref, sc_ref, sh_ref, rw_ref, rb_ref,
                    h1_ref, xp_ref, meta_ref, cnt_ref, run_ref, *, alpha, a_packed):
    first = (pl.program_id(0) == 0) & (pl.program_id(1) == 0)

    @pl.when(first)
    def _():
        run_ref[...] = jnp.zeros_like(run_ref)

    tm, d = h_ref.shape
    half = d // 2
    a = _unpack_cols(a_ref[...]).astype(BF16) if a_packed else a_ref[...]
    y = jnp.dot(a, w_ref[...], preferred_element_type=F32) + bias_ref[...]
    h1 = _layer_norm(alpha * h_ref[...] + gate_ref[...] * y, lng_ref[...], lnb_ref[...])
    h1_ref[...] = h1
    tb = (h1 * (1.0 + sc_ref[...]) + sh_ref[...]).astype(BF16)

    bits = lax.bitcast_convert_type(tb.astype(F32), U32)
    xp_ref[:, :half] = (bits[:, :half] & jnp.uint32(0xFFFF0000)) | (bits[:, half:] >> 16)

    logits = jnp.dot(tb, rw_ref[...], preferred_element_type=F32)
    logits_t = jnp.transpose(logits)[:N_GROUPS * EXPERTS_PER_GROUP, :]
    e_lo, e_hi, g_lo, g_hi, bucket = _route(logits_t, rb_ref[...])

    rows = lax.broadcasted_iota(I32, (BUCKET_ROWS, tm), 0)
    onehot = (rows == bucket).astype(F32)
    tri = (lax.broadcasted_iota(I32, (tm, tm), 0) <= lax.broadcasted_iota(I32, (tm, tm), 1)).astype(BF16)
    cum = jnp.dot(onehot.astype(BF16), tri, preferred_element_type=F32)
    run = run_ref[:, 0:1]
    rank = jnp.sum(onehot * (cum - 1.0 + run), axis=0, keepdims=True)
    new_run = run + cum[:, tm - 1:tm]
    run_ref[...] = jnp.broadcast_to(new_run, run_ref.shape)
    cnt_ref[...] = jnp.broadcast_to(new_run, cnt_ref.shape)

    mrow = lax.broadcasted_iota(I32, (8, tm), 0)
    meta = jnp.where(mrow == 0, e_lo.astype(F32), 0.0)
    meta = jnp.where(mrow == 1, e_hi.astype(F32), meta)
    meta = jnp.where(mrow == 2, g_lo, meta)
    meta = jnp.where(mrow == 3, g_hi, meta)
    meta = jnp.where(mrow == 4, bucket.astype(F32), meta)
    meta = jnp.where(mrow == 5, rank, meta)
    meta_ref[...] = meta

    grow = lax.broadcasted_iota(I32, (LANES, tm), 0)
    gates_t = jnp.where(grow == 0, g_lo, jnp.where(grow == 1, g_hi, 0.0))
    xp_ref[:, half:] = lax.bitcast_convert_type(jnp.transpose(gates_t), U32)


def _proj_ln_call(a, w, bias, h, gate, lng, lnb, sc, sh, rw_pad, rb, alpha):
    bsz, s, d = h.shape
    a_packed = a.dtype == U32
    da = a.shape[2]
    dm = w.shape[0]
    tm = min(TOKEN_TILE, s)
    t = bsz * s
    nt = s // tm
    dp = d // 2 + LANES
    kern = functools.partial(_proj_ln_kernel, alpha=alpha, a_packed=a_packed)
    vec = lambda: pl.BlockSpec((1, d), lambda b, i: (0, 0))
    bvec = lambda: pl.BlockSpec((None, 1, d), lambda b, i: (b, 0, 0))
    return pl.pallas_call(
        kern,
        out_shape=(jax.ShapeDtypeStruct((bsz, s, d), F32),
                   jax.ShapeDtypeStruct((t, dp), U32),
                   jax.ShapeDtypeStruct((8, t), F32),
                   jax.ShapeDtypeStruct((BUCKET_ROWS, LANES), F32)),
        grid=(bsz, nt),
        in_specs=[
            pl.BlockSpec((None, tm, da), lambda b, i: (b, i, 0)),
            _resident((dm, d), lambda b, i: (0, 0)),
            vec(),
            pl.BlockSpec((None, tm, d), lambda b, i: (b, i, 0)),
            bvec(), vec(), vec(), bvec(), bvec(),
            pl.BlockSpec((d, LANES), lambda b, i: (0, 0)),
            pl.BlockSpec((N_GROUPS * EXPERTS_PER_GROUP, 1), lambda b, i: (0, 0)),
        ],
        out_specs=(pl.BlockSpec((None, tm, d), lambda b, i: (b, i, 0)),
                   pl.BlockSpec((tm, dp), lambda b, i: (b * nt + i, 0)),
                   pl.BlockSpec((8, tm), lambda b, i: (0, b * nt + i)),
                   pl.BlockSpec((BUCKET_ROWS, LANES), lambda b, i: (0, 0))),
        scratch_shapes=[pltpu.VMEM((BUCKET_ROWS, LANES), F32)],
        compiler_params=_cparams("arbitrary", "arbitrary"),
        name="proj_ln_route",
    )(a, w, bias, h, gate, lng, lnb, sc, sh, rw_pad, rb)


def _wait_rows(src_row, dst_row, sem, n):
    def wait(r, c):
        pltpu.make_async_copy(src_row, dst_row, sem).wait()
        return c

    lax.fori_loop(0, n, wait, 0, unroll=8)


def _scatter_kernel(dest_ref, xp_ref, init_ref, xs_ref, sem):
    del init_ref
    tm = xp_ref.shape[0]
    base = pl.program_id(0) * tm

    def issue(r, c):
        pltpu.make_async_copy(xp_ref.at[pl.ds(r, 1)], xs_ref.at[pl.ds(dest_ref[base + r], 1)], sem).start()
        return c

    lax.fori_loop(0, tm, issue, 0, unroll=8)
    _wait_rows(xp_ref.at[pl.ds(0, 1)], xs_ref.at[pl.ds(0, 1)], sem, tm)


def _scatter_call(dest, xp, rows_out):
    t, dp = xp.shape
    tm = min(TOKEN_TILE, t)
    init = jnp.zeros((rows_out, dp), U32)
    return pl.pallas_call(
        _scatter_kernel,
        out_shape=jax.ShapeDtypeStruct((rows_out, dp), U32),
        grid_spec=pltpu.PrefetchScalarGridSpec(
            num_scalar_prefetch=1,
            grid=(t // tm,),
            in_specs=[pl.BlockSpec((tm, dp), lambda i, dest: (i, 0)),
                      pl.BlockSpec(memory_space=pl.ANY)],
            out_specs=pl.BlockSpec(memory_space=pl.ANY),
            scratch_shapes=[pltpu.SemaphoreType.DMA],
        ),
        input_output_aliases={2: 0},
        compiler_params=_cparams("arbitrary"),
        name="moe_scatter",
    )(dest, xp, init)


def _moe_kernel(ex_ref, valid_ref, xs_ref, w1_ref, w3_ref, w2_ref, y_ref, acc_ref):
    del ex_ref
    i = pl.program_id(0)
    s = pl.program_id(1)
    half = xs_ref.shape[1] - LANES
    ok = valid_ref[i] > 0

    @pl.when(ok)
    def _():
        w = xs_ref[:, :half]
        hi = lax.bitcast_convert_type(w & jnp.uint32(0xFFFF0000), F32).astype(BF16)
        lo = lax.bitcast_convert_type(w << 16, F32).astype(BF16)
        x = jnp.concatenate([hi, lo], axis=1)
        gates = lax.bitcast_convert_type(xs_ref[:, half:], F32)
        which = (s + i) % 2
        g = jnp.where(which == 0, gates[:, 0:1], gates[:, 1:2])
        a = jnp.dot(x, w1_ref[...], preferred_element_type=F32)
        b = jnp.dot(x, w3_ref[...], preferred_element_type=F32)
        hid = (_silu(a) * b * g).astype(BF16)
        y = jnp.dot(hid, w2_ref[...], preferred_element_type=F32)

        @pl.when(s == 0)
        def _():
            acc_ref[...] = y

        @pl.when(s != 0)
        def _():
            y_ref[...] = _pack_cols(acc_ref[...] + y)

    @pl.when(jnp.logical_not(ok) & (s == 0))
    def _():
        y_ref[...] = jnp.zeros_like(y_ref)


def _moe_call(ex, valid, xs, w1, w3, w2, layer):
    rows, dp = xs.shape
    _, n_e, d, ff = w1.shape
    tm = min(TOKEN_TILE, rows)
    ntiles = rows // tm
    return pl.pallas_call(
        _moe_kernel,
        out_shape=jax.ShapeDtypeStruct((rows, d // 2), U32),
        grid_spec=pltpu.PrefetchScalarGridSpec(
            num_scalar_prefetch=2,
            grid=(ntiles, 2),
            in_specs=[
                pl.BlockSpec((tm, dp), lambda i, s, ex, va: (i, 0)),
                pl.BlockSpec((None, None, d, ff), lambda i, s, ex, va: (layer, ex[2 * i + s], 0, 0)),
                pl.BlockSpec((None, None, d, ff), lambda i, s, ex, va: (layer, ex[2 * i + s], 0, 0)),
                pl.BlockSpec((None, None, ff, d), lambda i, s, ex, va: (layer, ex[2 * i + s], 0, 0)),
            ],
            out_specs=pl.BlockSpec((tm, d // 2), lambda i, s, ex, va: (i, 0)),
            scratch_shapes=[pltpu.VMEM((tm, d), F32)],
        ),
        compiler_params=_cparams("arbitrary", "arbitrary"),
        name="moe_experts",
    )(ex, valid, xs, w1, w3, w2)


GATHER_CHUNKS = 8


def _gather_ln_kernel(dest_ref, ys_ref, h_ref, gate_ref, lng_ref, lnb_ref, o_ref, buf, sem, *, alpha, nt):
    tm = h_ref.shape[0]
    step = pl.program_id(0) * nt + pl.program_id(1)
    last = pl.num_programs(0) * nt - 1
    slot = step % 2

    def row_copy(tile, sl, r):
        return pltpu.make_async_copy(ys_ref.at[pl.ds(dest_ref[tile * tm + r], 1)], buf.at[sl, pl.ds(r, 1)],
                                     sem.at[sl])

    def wait_tile(sl):
        _wait_rows(ys_ref.at[pl.ds(0, 1)], buf.at[sl, pl.ds(0, 1)], sem.at[sl], tm)

    @pl.when(step == 0)
    def _():
        def issue(r, c):
            row_copy(0, 0, r).start()
            return c

        lax.fori_loop(0, tm, issue, 0, unroll=8)

    wait_tile(slot)
    nxt = jnp.minimum(step + 1, last)
    rows = tm // GATHER_CHUNKS
    for c in range(GATHER_CHUNKS):
        for r in range(c * rows, (c + 1) * rows):
            row_copy(nxt, 1 - slot, r).start()
        sl = slice(c * rows, (c + 1) * rows)
        f = _unpack_cols(buf[slot, sl])
        o_ref[sl] = _layer_norm(alpha * h_ref[sl] + gate_ref[...] * f, lng_ref[...], lnb_ref[...])

    @pl.when(step == last)
    def _():
        wait_tile(1 - slot)


def _gather_ln_call(dest, ys, h, gate, lng, lnb, alpha):
    bsz, s, d = h.shape
    tm = min(TOKEN_TILE, s)
    nt = s // tm
    kern = functools.partial(_gather_ln_kernel, alpha=alpha, nt=nt)
    return pl.pallas_call(
        kern,
        out_shape=jax.ShapeDtypeStruct((bsz, s, d), F32),
        grid_spec=pltpu.PrefetchScalarGridSpec(
            num_scalar_prefetch=1,
            grid=(bsz, nt),
            in_specs=[
                pl.BlockSpec(memory_space=pl.ANY),
                pl.BlockSpec((None, tm, d), lambda b, i, dest: (b, i, 0)),
                pl.BlockSpec((None, 1, d), lambda b, i, dest: (b, 0, 0)),
                pl.BlockSpec((1, d), lambda b, i, dest: (0, 0)),
                pl.BlockSpec((1, d), lambda b, i, dest: (0, 0)),
            ],
            out_specs=pl.BlockSpec((None, tm, d), lambda b, i, dest: (b, i, 0)),
            scratch_shapes=[pltpu.VMEM((2, tm, d // 2), U32), pltpu.SemaphoreType.DMA((2,))],
        ),
        compiler_params=_cparams("arbitrary", "arbitrary"),
        name="moe_gather_ln",
    )(dest, ys, h, gate, lng, lnb)


_PAIR_LO = (0, 0, 0, 1, 1, 2)
_PAIR_HI = (1, 2, 3, 2, 3, 3)


def _moe_layer(xp, meta, cnt, h1, gate2, lng, lnb, w1, w3, w2, layer, alpha):
    t = xp.shape[0]
    tm = min(TOKEN_TILE, t)
    ntiles = t // tm + N_BUCKETS
    counts = cnt[:N_BUCKETS, 0].astype(I32)
    tiles_b = (counts + tm - 1) // tm
    tile_end = jnp.cumsum(tiles_b)
    offs = (tile_end - tiles_b) * tm
    bucket = meta[4].astype(I32)
    dest = offs[bucket] + meta[5].astype(I32)

    tile = jnp.arange(ntiles, dtype=I32)
    valid = (tile < tile_end[-1]).astype(I32)
    tb = jnp.minimum(jnp.sum((tile[:, None] >= tile_end[None, :]).astype(I32), axis=1), N_BUCKETS - 1)
    lo = jnp.asarray(_PAIR_LO, I32)[tb % N_PAIRS] + EXPERTS_PER_GROUP * (tb // N_PAIRS)
    hi = jnp.asarray(_PAIR_HI, I32)[tb % N_PAIRS] + EXPERTS_PER_GROUP * (tb // N_PAIRS)
    odd = (tile % 2) == 1
    ex = jnp.stack([jnp.where(odd, hi, lo), jnp.where(odd, lo, hi)], axis=1).reshape(-1)

    xs = _scatter_call(dest, xp, ntiles * tm)
    ys = _moe_call(ex, valid, xs, w1, w3, w2, layer)
    return _gather_ln_call(dest, ys, h1, gate2, lng, lnb, alpha)


def _hy_in_kernel(xm_ref, xp_ref, xn_ref, sc_ref, sh_ref, w_ref, b_ref, cw_ref, cb_ref, x0_ref, z_ref, *, nt, tn):
    i = pl.program_id(1)
    tm, d = xm_ref.shape
    sc, sh = 1.0 + sc_ref[...], sh_ref[...]
    u = jnp.concatenate([xp_ref[...] * sc + sh, xm_ref[...] * sc + sh, xn_ref[...] * sc + sh], axis=0).astype(BF16)
    rows = tm + 16
    rid = lax.broadcasted_iota(I32, (rows, 1), 0)
    keep = ((rid >= 8) | (i > 0)) & ((rid < tm + 8) | (i < nt - 1))

    def conv(sec, j):
        col = sec * d + j * tn
        p = jnp.dot(u, w_ref[:, col:col + tn], preferred_element_type=F32) + b_ref[:, col:col + tn]
        p = jnp.where(keep, p, 0.0)
        cw = cw_ref[:, col:col + tn]
        out = (cw[0:1] * pltpu.roll(p, 1, 0) + cw[1:2] * p + cw[2:3] * pltpu.roll(p, rows - 1, 0)
               + cb_ref[:, col:col + tn])
        return out[8:8 + tm]

    nch = d // tn
    for j in range(nch // 2):
        x0_ref[:, j * tn:(j + 1) * tn] = _pack_pair(conv(0, j), conv(0, j + nch // 2))
        z_ref[:, j * tn:(j + 1) * tn] = _pack_pair(conv(1, j) * conv(2, j),
                                                   conv(1, j + nch // 2) * conv(2, j + nch // 2))


def _hy_in_call(h, sc, sh, w, b, cw, cb):
    bsz, s, d = h.shape
    tm = min(TOKEN_TILE, s)
    nt = s // tm
    hb = tm // 8
    tn = 512
    kern = functools.partial(_hy_in_kernel, nt=nt, tn=tn)
    return pl.pallas_call(
        kern,
        out_shape=(jax.ShapeDtypeStruct((bsz, s, d // 2), U32), jax.ShapeDtypeStruct((bsz, s, d // 2), U32)),
        grid=(bsz, nt),
        in_specs=[
            pl.BlockSpec((None, tm, d), lambda b, i: (b, i, 0)),
            pl.BlockSpec((None, 8, d), lambda b, i: (b, jnp.maximum(i * hb - 1, 0), 0)),
            pl.BlockSpec((None, 8, d), lambda b, i: (b, jnp.minimum((i + 1) * hb, nt * hb - 1), 0)),
            pl.BlockSpec((None, 1, d), lambda b, i: (b, 0, 0)),
            pl.BlockSpec((None, 1, d), lambda b, i: (b, 0, 0)),
            _resident((d, 3 * d), lambda b, i: (0, 0)),
            pl.BlockSpec((1, 3 * d), lambda b, i: (0, 0)),
            pl.BlockSpec((3, 3 * d), lambda b, i: (0, 0)),
            pl.BlockSpec((1, 3 * d), lambda b, i: (0, 0)),
        ],
        out_specs=(pl.BlockSpec((None, tm, d // 2), lambda b, i: (b, i, 0)),
                   pl.BlockSpec((None, tm, d // 2), lambda b, i: (b, i, 0))),
        compiler_params=_cparams("arbitrary", "arbitrary"),
        name="hyena_in",
    )(h, h, h, sc, sh, w, b, cw, cb)


def _filter_kernel(w1_ref, b1_ref, w2_ref, b2_ref, w3_ref, b3_ref, fr_ref, w4_ref, dl_ref, f_ref, l1_ref,
                   *, seq, tn):
    i = pl.program_id(0)

    @pl.when(i == 0)
    def _():
        l1_ref[...] = jnp.zeros_like(l1_ref)

    n_lane = i * tn + lax.broadcasted_iota(I32, (1, tn), 1)
    m_lane = jnp.where(n_lane < seq, n_lane, 2 * seq - n_lane).astype(F32)
    t_lane = m_lane / (seq - 1.0)
    wl = (2.0 * math.pi) * m_lane / float(seq)
    band = lax.broadcasted_iota(I32, (HY_BANDS, 1), 0).astype(F32)
    fb = 1e-4 + band * ((HY_BANDS - 1 - 1e-4) / (HY_BANDS - 1))
    ang = fb * wl
    z = jnp.concatenate([t_lane, jnp.cos(ang), -jnp.sin(ang),
                         jnp.zeros((7, tn), F32)], axis=0)
    fr = fr_ref[...]

    def layer(w_ref, b_ref, x):
        pre = lax.dot_general(w_ref[...], x, (((0,), (0,)), ((), ())), preferred_element_type=F32,
                              precision=HIGHEST)
        return jnp.sin(fr * (pre + b_ref[...]))

    hdn = layer(w1_ref, b1_ref, z)
    hdn = layer(w2_ref, b2_ref, hdn)
    hdn = layer(w3_ref, b3_ref, hdn)
    h = lax.dot_general(hdn.astype(BF16), w4_ref[...].astype(BF16), (((0,), (0,)), ((), ())),
                        preferred_element_type=F32)
    n_col = i * tn + lax.broadcasted_iota(I32, (tn, 1), 0)
    m_col = jnp.where(n_col < seq, n_col, 2 * seq - n_col).astype(F32)
    h = h * jnp.exp(-(m_col / (seq - 1.0)) * dl_ref[...])
    h = jnp.where(n_col == seq, 0.0, h)
    l1_ref[...] += jnp.sum(jnp.abs(h), axis=0, keepdims=True)
    f_ref[...] = _pack_cols(h)


def _filter_call(fw1, fb1, fw2, fb2, fw3, fb3, freq, fw4, seq):
    width = fw2.shape[0]
    d = fw4.shape[1] // 2
    tn = min(512, seq)
    steps = 2 * seq // tn
    max_decay = math.log(HY_TARGET) / HY_FAST_DECAY
    min_decay = math.log(HY_TARGET) / HY_SLOW_DECAY
    deltas = jnp.abs(jnp.linspace(min_decay, max_decay, d, dtype=F32)).reshape(1, d)
    w1p = jnp.concatenate([fw1, jnp.zeros((7, width), F32)], axis=0)
    col = lambda a: a.reshape(width, 1)
    small = lambda shp: pl.BlockSpec(shp, lambda i: (0, 0))
    kern = functools.partial(_filter_kernel, seq=seq, tn=tn)
    return pl.pallas_call(
        kern,
        out_shape=(jax.ShapeDtypeStruct((2 * seq, d // 2), U32), jax.ShapeDtypeStruct((1, d), F32)),
        grid=(steps,),
        in_specs=[small((40, width)), small((width, 1)), small((width, width)), small((width, 1)),
                  small((width, width)), small((width, 1)), small((width, 1)),
                  pl.BlockSpec((width, d), lambda i: (0, (i * tn) // seq)),
                  small((1, d))],
        out_specs=(pl.BlockSpec((tn, d // 2), lambda i: (i, 0)), small((1, d))),
        compiler_params=_cparams("arbitrary"),
        name="hyena_filter",
    )(w1p, col(fb1), fw2, col(fb2), fw3, col(fb3), col(freq), fw4, deltas)


def _dft_tables(n1, n2):
    n = n1 * n2
    k = np.arange(n1)[:, None]
    m = np.arange(n1)[None, :]
    ang1 = -2.0 * np.pi * ((k * m) % n1) / n1
    f1r, f1i = np.cos(ang1), np.sin(ang1)
    hn = n1 // 2
    a_data = np.block([[f1r[:, :hn], -f1i[:, :hn]], [f1i[:, :hn], f1r[:, :hn]]])
    a_filt = np.concatenate([f1r, f1i], axis=0)
    a_inv = np.block([[f1r.T[:hn], f1i.T[:hn]], [-f1i.T[:hn], f1r.T[:hn]]]) / n
    k2 = np.arange(n2)[:, None]
    m2 = np.arange(n2)[None, :]
    ang2 = -2.0 * np.pi * ((k2 * m2) % n2) / n2
    f2r, f2i = np.cos(ang2), np.sin(ang2)
    b_fwd = np.block([[f2r, -f2i], [f2i, f2r]])
    b_inv = np.block([[f2r, f2i], [-f2i, f2r]])
    angt = -2.0 * np.pi * ((np.arange(n2)[:, None] * np.arange(n1)[None, :]) % n) / n
    tw = (np.cos(angt), np.sin(angt))
    perm = (np.arange(n1 // FFT_K1_GROUP)[None, :] * FFT_K1_GROUP + np.arange(FFT_K1_GROUP)[:, None]).reshape(-1)
    perm2 = np.concatenate([perm, perm + n1])
    as_bf16 = lambda a: jnp.asarray(a, F32).astype(BF16)
    return dict(a_data=as_bf16(a_data[perm2]), a_filt=as_bf16(a_filt[perm2]), a_inv=as_bf16(a_inv[:, perm2]),
                b_fwd=as_bf16(b_fwd), b_inv=as_bf16(b_inv),
                tw_n2=tuple(jnp.asarray(t[:, perm], F32).reshape(n2, n1, 1) for t in tw),
                tw_k1=tuple(jnp.asarray(t.T.copy(), F32).reshape(n1, n2, 1) for t in tw))


FFT_K1_GROUP = 4


def _pack_pair(hi, lo):
    hb = lax.bitcast_convert_type(hi.astype(BF16).astype(F32), U32)
    lb = lax.bitcast_convert_type(lo.astype(BF16).astype(F32), U32)
    return (hb & jnp.uint32(0xFFFF0000)) | (lb >> 16)


def _unpack_pair(w):
    hi = lax.bitcast_convert_type(w & jnp.uint32(0xFFFF0000), F32)
    lo = lax.bitcast_convert_type(w << 16, F32)
    return hi, lo


def _fetch_step(view, buf, sem):
    k = pl.program_id(0)
    slot = k % 2

    def copy(step, sl):
        return pltpu.make_async_copy(view(step), buf.at[sl], sem.at[sl])

    @pl.when(k == 0)
    def _():
        copy(0, 0).start()

    @pl.when(k + 1 < pl.num_programs(0))
    def _():
        copy(k + 1, 1 - slot).start()

    copy(k, slot).wait()
    return buf.at[slot]


def _store_step(view, buf, sem, fill):
    k = pl.program_id(0)
    slot = k % 2

    def copy(step, sl):
        return pltpu.make_async_copy(buf.at[sl], view(step), sem.at[sl])

    @pl.when(k >= 2)
    def _():
        copy(k - 2, slot).wait()

    fill(buf.at[slot])
    copy(k, slot).start()

    @pl.when(k == pl.num_programs(0) - 1)
    def _():
        copy(k, slot).wait()

        @pl.when(k >= 1)
        def _():
            copy(k - 1, 1 - slot).wait()


def _unpack_cols(w):
    hi, lo = _unpack_pair(w)
    return jnp.concatenate([hi, lo], axis=1)


def _pack_cols(x):
    half = x.shape[1] // 2
    return _pack_pair(x[:, :half], x[:, half:])


def _fft_a_kernel(x_hbm, a_ref, twr_ref, twi_ref, y_ref, xbuf, sem):
    n1 = a_ref.shape[0] // 2
    x_ref = _fetch_step(lambda j: x_hbm.at[:, :, j, :], xbuf, sem)
    rhs = jnp.concatenate([_unpack_cols(x_ref[0]), _unpack_cols(x_ref[1])], axis=0).astype(BF16)
    y = jnp.dot(a_ref[...], rhs, preferred_element_type=F32)
    yr, yi = y[:n1], y[n1:]
    tr, ti = twr_ref[...], twi_ref[...]
    w = _pack_pair(yr * tr - yi * ti, yr * ti + yi * tr)
    q = n1 // FFT_K1_GROUP
    d = w.shape[1]
    for kk in range(FFT_K1_GROUP):
        y_ref[:, kk * d:(kk + 1) * d] = w[kk * q:(kk + 1) * q]


def _fft_a_call(x4, a_mat, tw_n2):
    _, hn, n2, half = x4.shape
    d = 2 * half
    n1 = 2 * hn
    q = n1 // FFT_K1_GROUP
    return pl.pallas_call(
        _fft_a_kernel,
        out_shape=jax.ShapeDtypeStruct((n2, q, FFT_K1_GROUP * d), U32),
        grid=(n2,),
        in_specs=[
            pl.BlockSpec(memory_space=pl.ANY),
            pl.BlockSpec((2 * n1, n1), lambda j: (0, 0)),
            pl.BlockSpec((None, n1, 1), lambda j: (j, 0, 0)),
            pl.BlockSpec((None, n1, 1), lambda j: (j, 0, 0)),
        ],
        out_specs=pl.BlockSpec((None, q, FFT_K1_GROUP * d), lambda j: (j, 0, 0)),
        scratch_shapes=[pltpu.VMEM((2, 2, hn, half), U32), pltpu.SemaphoreType.DMA((2,))],
        compiler_params=_cparams("arbitrary"),
        name="fft_stage_a",
    )(x4, a_mat, tw_n2[0], tw_n2[1])


def _load_k1(y_ref, kk):
    d = y_ref.shape[1] // FFT_K1_GROUP
    yr, yi = _unpack_pair(y_ref[:, kk * d:(kk + 1) * d])
    return jnp.concatenate([yr, yi], axis=0).astype(BF16)


def _fft_b_kernel(y_hbm, b_ref, h_ref, ybuf, sem):
    y_ref = _fetch_step(lambda kb: y_hbm.at[:, kb, :], ybuf, sem)
    n2 = y_ref.shape[0]
    for kk in range(FFT_K1_GROUP):
        x = jnp.dot(b_ref[...], _load_k1(y_ref, kk), preferred_element_type=F32)
        h_ref[kk] = _pack_pair(x[:n2], x[n2:])


def _fft_b_call(y, b_fwd):
    n2, q, gd = y.shape
    d = gd // FFT_K1_GROUP
    return pl.pallas_call(
        _fft_b_kernel,
        out_shape=jax.ShapeDtypeStruct((q * FFT_K1_GROUP, n2, d), U32),
        grid=(q,),
        in_specs=[pl.BlockSpec(memory_space=pl.ANY),
                  pl.BlockSpec((2 * n2, 2 * n2), lambda k: (0, 0))],
        out_specs=pl.BlockSpec((FFT_K1_GROUP, n2, d), lambda k: (k, 0, 0)),
        scratch_shapes=[pltpu.VMEM((2, n2, gd), U32), pltpu.SemaphoreType.DMA((2,))],
        compiler_params=_cparams("arbitrary"),
        name="fft_filter_b",
    )(y, b_fwd)


def _fft_bc_kernel(y_hbm, h_ref, bf_ref, bi_ref, twr_ref, twi_ref, g_hbm, ybuf, gbuf, sem_in, sem_out):
    y_ref = _fetch_step(lambda kb: y_hbm.at[:, kb, :], ybuf, sem_in)
    n2 = y_ref.shape[0]
    d = y_ref.shape[1] // FFT_K1_GROUP

    def fill(g_ref):
        for kk in range(FFT_K1_GROUP):
            x = jnp.dot(bf_ref[...], _load_k1(y_ref, kk), preferred_element_type=F32)
            xr, xi = x[:n2], x[n2:]
            hr, hi = _unpack_pair(h_ref[kk])
            z = jnp.concatenate([xr * hr - xi * hi, xr * hi + xi * hr], axis=0).astype(BF16)
            g = jnp.dot(bi_ref[...], z, preferred_element_type=F32)
            gr, gi = g[:n2], g[n2:]
            tr, ti = twr_ref[kk], twi_ref[kk]
            g_ref[:, kk * d:(kk + 1) * d] = _pack_pair(gr * tr + gi * ti, gi * tr - gr * ti)

    _store_step(lambda kb: g_hbm.at[:, kb, :], gbuf, sem_out, fill)


def _fft_bc_call(y, hspec, b_fwd, b_inv, tw_k1):
    n2, q, gd = y.shape
    d = gd // FFT_K1_GROUP
    mat = lambda: pl.BlockSpec((2 * n2, 2 * n2), lambda k: (0, 0))
    tw = lambda: pl.BlockSpec((FFT_K1_GROUP, n2, 1), lambda k: (k, 0, 0))
    return pl.pallas_call(
        _fft_bc_kernel,
        out_shape=jax.ShapeDtypeStruct((n2, q, gd), U32),
        grid=(q,),
        in_specs=[pl.BlockSpec(memory_space=pl.ANY),
                  pl.BlockSpec((FFT_K1_GROUP, n2, d), lambda k: (k, 0, 0)), mat(), mat(), tw(), tw()],
        out_specs=pl.BlockSpec(memory_space=pl.ANY),
        scratch_shapes=[pltpu.VMEM((2, n2, gd), U32), pltpu.VMEM((2, n2, gd), U32),
                        pltpu.SemaphoreType.DMA((2,)), pltpu.SemaphoreType.DMA((2,))],
        compiler_params=_cparams("arbitrary"),
        name="fft_stage_bc",
    )(y, hspec, b_fwd, b_inv, tw_k1[0], tw_k1[1])


def _fft_d_kernel(g_ref, a_ref, x0_hbm, z_hbm, l1_ref, fb_ref, o_hbm, x0buf, zbuf, obuf, sem_x, sem_z, sem_o):
    hn = a_ref.shape[0] // 2
    d = g_ref.shape[1] // FFT_K1_GROUP
    x0_ref = _fetch_step(lambda j: x0_hbm.at[:, :, j, :], x0buf, sem_x)
    z_ref = _fetch_step(lambda j: z_hbm.at[:, :, j, :], zbuf, sem_z)
    parts = [_unpack_pair(g_ref[:, kk * d:(kk + 1) * d]) for kk in range(FFT_K1_GROUP)]
    rhs = jnp.concatenate([p[0] for p in parts] + [p[1] for p in parts], axis=0).astype(BF16)
    y = jnp.dot(a_ref[...], rhs, preferred_element_type=F32)
    inv_l1 = 1.0 / l1_ref[...]
    fb = fb_ref[...]

    def fill(o_ref):
        for b in range(2):
            conv = y[b * hn:(b + 1) * hn] * inv_l1
            o_ref[b] = _pack_cols(_unpack_cols(x0_ref[b]) * (conv + fb * _unpack_cols(z_ref[b])))

    _store_step(lambda j: o_hbm.at[:, :, j, :], obuf, sem_o, fill)


def _fft_d_call(g, a_inv, x0_4, z_4, l1, fbias):
    n2, q, gd = g.shape
    d = gd // FFT_K1_GROUP
    n1 = q * FFT_K1_GROUP
    hn = n1 // 2
    half = d // 2
    tok_buf = lambda: pltpu.VMEM((2, 2, hn, half), U32)
    return pl.pallas_call(
        _fft_d_kernel,
        out_shape=jax.ShapeDtypeStruct((2, hn, n2, half), U32),
        grid=(n2,),
        in_specs=[pl.BlockSpec((None, q, gd), lambda j: (j, 0, 0)),
                  pl.BlockSpec((n1, 2 * n1), lambda j: (0, 0)),
                  pl.BlockSpec(memory_space=pl.ANY), pl.BlockSpec(memory_space=pl.ANY),
                  pl.BlockSpec((1, d), lambda j: (0, 0)),
                  pl.BlockSpec((1, d), lambda j: (0, 0))],
        out_specs=pl.BlockSpec(memory_space=pl.ANY),
        scratch_shapes=[tok_buf(), tok_buf(), tok_buf(),
                        pltpu.SemaphoreType.DMA((2,)), pltpu.SemaphoreType.DMA((2,)), pltpu.SemaphoreType.DMA((2,))],
        compiler_params=_cparams("arbitrary"),
        name="fft_stage_d",
    )(g, a_inv, x0_4, z_4, l1, fbias)


def _hyena_conv(x0p, zp, filtp, l1, fbias):
    bsz, seq, half = zp.shape
    assert bsz == 2, "the two batch rows are packed as one complex signal"
    n2 = FFT_N2
    n1 = 2 * seq // n2
    tabs = _dft_tables(n1, n2)
    hn = n1 // 2
    view = lambda a: a.reshape(2, hn, n2, half)
    hspec = _fft_b_call(_fft_a_call(view(filtp), tabs["a_filt"], tabs["tw_n2"]), tabs["b_fwd"])
    y = _fft_a_call(view(zp), tabs["a_data"], tabs["tw_n2"])
    g = _fft_bc_call(y, hspec, tabs["b_fwd"], tabs["b_inv"], tabs["tw_k1"])
    out = _fft_d_call(g, tabs["a_inv"], view(x0p), view(zp), l1, fbias)
    return out.reshape(bsz, seq, half)


def _rope_tables(seq, head_dim):
    axis_dim = head_dim // 2
    rows = seq // GRID_W
    inv = ROPE_BASE ** (-jnp.arange(0, axis_dim, 2, dtype=F32) / axis_dim)
    row = jnp.repeat(jnp.arange(rows, dtype=F32), GRID_W)[:, None] * inv
    col = jnp.tile(jnp.arange(GRID_W, dtype=F32), rows)[:, None] * inv
    quarter = axis_dim // 2
    cos = jnp.concatenate([jnp.cos(row), jnp.cos(row), jnp.cos(col), jnp.cos(col)], axis=1)
    sin = jnp.concatenate([jnp.sin(row), jnp.sin(row), jnp.sin(col), jnp.sin(col)], axis=1)
    second = (np.arange(head_dim) % axis_dim) >= quarter
    reps = LANES // head_dim
    cos = jnp.tile(cos, (1, reps))
    sin = jnp.tile(sin, (1, reps))
    second = jnp.asarray(np.tile(second, reps))[None, :]
    return cos, jnp.where(second, sin, 0.0), jnp.where(second, 0.0, -sin)


def kernel(x, c, ctx, c_ctx, ada_w, ada_b, attn_w_in, attn_b_in, attn_sink, attn_w_out, hy_w_in, hy_b_in, hy_conv_w, hy_conv_b, hy_f_w1, hy_f_b1, hy_f_w2, hy_f_b2, hy_f_w3, hy_f_b3, hy_f_freq, hy_f_w4, hy_f_bias, hy_w_out, hy_b_out, ln1_g, ln1_b, ln2_g, ln2_b, router_w, router_b, moe_w1, moe_w3, moe_w2):
    bsz, seq, d = x.shape
    depth = ada_w.shape[0]
    assert depth == 2 and attn_w_in.shape[0] == 1 and hy_w_in.shape[0] == 1
    alpha = (2 * depth) ** 0.25
    n_heads = attn_sink.shape[1]
    attn_dim = attn_w_out.shape[1]
    head_dim = attn_dim // n_heads
    kv_dim = (attn_w_in.shape[2] - attn_dim) // 2
    n_kv = kv_dim // head_dim
    group = n_heads // n_kv
    assert head_dim * 2 == LANES and group % 2 == 0
    n_exp = router_w.shape[1]
    assert n_exp == N_GROUPS * EXPERTS_PER_GROUP

    cond = jnp.concatenate([c, c_ctx[None, :], jnp.zeros((8 - bsz - 1, d), F32)], axis=0)
    mods = _ada_call(cond, ada_w, ada_b).reshape(depth, 8, 6, d)
    mod = lambda layer, k: mods[layer, :bsz, k].reshape(bsz, 1, d)
    cmod = lambda layer, k: mods[layer, bsz, k].reshape(1, d)
    row = lambda v: v.reshape(1, -1)

    rw_pad = jnp.concatenate([router_w, jnp.zeros((d, LANES - n_exp), F32)], axis=1).astype(BF16)
    rb = router_b.reshape(n_exp, 1)
    w1b, w3b, w2b = moe_w1.astype(BF16), moe_w3.astype(BF16), moe_w2.astype(BF16)

    w_ext = attn_w_in[0].astype(BF16)
    b_ext = row(attn_b_in[0])
    w_kv, b_kv = w_ext[:, attn_dim:], b_ext[:, attn_dim:]
    n_kd = 2 * kv_dim
    cos_t, sa_t, sb_t = _rope_tables(seq, head_dim)

    q, kd, vd = _qkv_call(x, mod(0, 1), mod(0, 0), w_ext, b_ext, cos_t, sa_t, sb_t, attn_dim, n_kd, head_dim)
    kxd, vxd = _ctx_kv_call(ctx, cmod(0, 1), cmod(0, 0), w_kv, b_kv, n_kd)
    att = _attn_call(attn_sink[0], q, kd, vd, kxd, vxd, n_kv, group)
    h1, xp, meta, cnt = _proj_ln_call(att, attn_w_out[0].astype(BF16), jnp.zeros((1, d), F32), x, mod(0, 2),
                                      row(ln1_g[0]), row(ln1_b[0]), mod(0, 4), mod(0, 3), rw_pad, rb, alpha)
    h = _moe_layer(xp, meta, cnt, h1, mod(0, 5), row(ln2_g[0]), row(ln2_b[0]), w1b, w3b, w2b, 0, alpha)

    x0, z = _hy_in_call(h, mod(1, 1), mod(1, 0), hy_w_in[0].astype(BF16), row(hy_b_in[0]), hy_conv_w[0],
                        row(hy_conv_b[0]))
    filt, l1 = _filter_call(hy_f_w1[0], hy_f_b1[0], hy_f_w2[0], hy_f_b2[0], hy_f_w3[0], hy_f_b3[0],
                            hy_f_freq[0], hy_f_w4[0], seq)
    yh = _hyena_conv(x0, z, filt, l1, row(hy_f_bias[0]))
    h1, xp, meta, cnt = _proj_ln_call(yh, hy_w_out[0].astype(BF16), row(hy_b_out[0]), h, mod(1, 2),
                                      row(ln1_g[1]), row(ln1_b[1]), mod(1, 4), mod(1, 3), rw_pad, rb, alpha)
    return _moe_layer(xp, meta, cnt, h1, mod(1, 5), row(ln2_g[1]), row(ln2_b[1]), w1b, w3b, w2b, 1, alpha)
```

```python
import functools
import math

import numpy as np
import jax
import jax.numpy as jnp
from jax import lax
from jax.experimental import pallas as pl
from jax.experimental.pallas import tpu as pltpu

F32 = jnp.float32
BF16 = jnp.bfloat16
I32 = jnp.int32
U32 = jnp.uint32
HIGHEST = lax.Precision.HIGHEST

LANES = 128
V7X_VMEM_LIMIT_BYTES = 56 * 1024 * 1024

GRID_W = 64
BLOCK = 128
ROPE_BASE = 10000.0
NEG_INF = -1e30
HY_BANDS = 16
HY_FAST_DECAY = 0.3
HY_SLOW_DECAY = 1.5
HY_TARGET = 1e-2
N_GROUPS = 4
EXPERTS_PER_GROUP = 4
N_PAIRS = 6
N_BUCKETS = N_GROUPS * N_PAIRS
BUCKET_ROWS = 32
LN_EPS = 1e-5
FFT_N2 = 128

TOKEN_TILE = 512
PROJ_CHUNKS = 4


def _cparams(*sem):
    return pltpu.CompilerParams(dimension_semantics=sem, vmem_limit_bytes=V7X_VMEM_LIMIT_BYTES)


def _resident(block_shape, index_map):
    return pl.BlockSpec(block_shape, index_map, pipeline_mode=pl.Buffered(1))


def _silu(x):
    return x * jax.nn.sigmoid(x)


def _ada_kernel(c_ref, w_ref, b_ref, o_ref):
    c = _silu(c_ref[...])
    o_ref[...] = jnp.dot(c, w_ref[...], preferred_element_type=F32, precision=HIGHEST) + b_ref[...]


def _ada_call(cond, ada_w, ada_b):
    depth, d, n6 = ada_w.shape
    tn = 1024
    rows = cond.shape[0]
    return pl.pallas_call(
        _ada_kernel,
        out_shape=jax.ShapeDtypeStruct((depth, rows, n6), F32),
        grid=(depth, n6 // tn),
        in_specs=[
            pl.BlockSpec((rows, d), lambda l, j: (0, 0)),
            pl.BlockSpec((None, d, tn), lambda l, j: (l, 0, j)),
            pl.BlockSpec((None, 1, tn), lambda l, j: (l, 0, j)),
        ],
        out_specs=pl.BlockSpec((None, rows, tn), lambda l, j: (l, 0, j)),
        compiler_params=_cparams("arbitrary", "arbitrary"),
        name="ada_mod",
    )(cond, ada_w, ada_b.reshape(depth, 1, n6))


def _qkv_kernel(x_ref, sc_ref, sh_ref, w_ref, b_ref, cos_ref, sa_ref, sb_ref, q_ref, k_ref, v_ref, *, scale):
    u = (x_ref[...] * (1.0 + sc_ref[...]) + sh_ref[...]).astype(BF16)
    p = jnp.dot(u, w_ref[...], preferred_element_type=F32) + b_ref[...]
    cos, sa, sb = cos_ref[...], sa_ref[...], sb_ref[...]
    nq = q_ref.shape[1]
    nk = k_ref.shape[1]

    def rope(xc):
        return xc * cos + pltpu.roll(xc, 16, 1) * sa + pltpu.roll(xc, LANES - 16, 1) * sb

    for c in range(nq // LANES):
        q_ref[:, c * LANES:(c + 1) * LANES] = (rope(p[:, c * LANES:(c + 1) * LANES]) * scale).astype(BF16)
    nkv = nk // 2
    for c in range(nkv // LANES):
        _store_dup_heads(k_ref, c, rope(p[:, nq + c * LANES:nq + (c + 1) * LANES]))
        _store_dup_heads(v_ref, c, p[:, nq + nkv + c * LANES:nq + nkv + (c + 1) * LANES])


def _store_dup_heads(ref, c, pair):
    lo = lax.broadcasted_iota(I32, (1, LANES), 1) < (LANES // 2)
    swapped = pltpu.roll(pair, LANES // 2, 1)
    ref[:, (2 * c) * LANES:(2 * c + 1) * LANES] = jnp.where(lo, pair, swapped).astype(ref.dtype)
    ref[:, (2 * c + 1) * LANES:(2 * c + 2) * LANES] = jnp.where(lo, swapped, pair).astype(ref.dtype)


def _qkv_call(x, sc, sh, w_ext, b_ext, cos_t, sa_t, sb_t, n_q, n_kd, head_dim):
    bsz, s, d = x.shape
    tm = min(TOKEN_TILE, s)
    n_out = w_ext.shape[1]
    kern = functools.partial(_qkv_kernel, scale=head_dim ** -0.5)
    return pl.pallas_call(
        kern,
        out_shape=(jax.ShapeDtypeStruct((bsz, s, n_q), BF16),
                   jax.ShapeDtypeStruct((bsz, s, n_kd), BF16),
                   jax.ShapeDtypeStruct((bsz, s, n_kd), BF16)),
        grid=(bsz, s // tm),
        in_specs=[
            pl.BlockSpec((None, tm, d), lambda b, i: (b, i, 0)),
            pl.BlockSpec((None, 1, d), lambda b, i: (b, 0, 0)),
            pl.BlockSpec((None, 1, d), lambda b, i: (b, 0, 0)),
            _resident((d, n_out), lambda b, i: (0, 0)),
            pl.BlockSpec((1, n_out), lambda b, i: (0, 0)),
            pl.BlockSpec((tm, LANES), lambda b, i: (i, 0)),
            pl.BlockSpec((tm, LANES), lambda b, i: (i, 0)),
            pl.BlockSpec((tm, LANES), lambda b, i: (i, 0)),
        ],
        out_specs=(pl.BlockSpec((None, tm, n_q), lambda b, i: (b, i, 0)),
                   pl.BlockSpec((None, tm, n_kd), lambda b, i: (b, i, 0)),
                   pl.BlockSpec((None, tm, n_kd), lambda b, i: (b, i, 0))),
        compiler_params=_cparams("arbitrary", "arbitrary"),
        name="attn_qkv",
    )(x, sc, sh, w_ext, b_ext, cos_t, sa_t, sb_t)


def _ctx_kv_kernel(x_ref, sc_ref, sh_ref, w_ref, b_ref, k_ref, v_ref):
    u = (x_ref[...] * (1.0 + sc_ref[...]) + sh_ref[...]).astype(BF16)
    p = jnp.dot(u, w_ref[...], preferred_element_type=F32) + b_ref[...]
    nkv = k_ref.shape[1] // 2
    for c in range(nkv // LANES):
        _store_dup_heads(k_ref, c, p[:, c * LANES:(c + 1) * LANES])
        _store_dup_heads(v_ref, c, p[:, nkv + c * LANES:nkv + (c + 1) * LANES])


def _ctx_kv_call(ctx, csc, csh, w_kv, b_kv, n_kd):
    bsz, c, d = ctx.shape
    return pl.pallas_call(
        _ctx_kv_kernel,
        out_shape=(jax.ShapeDtypeStruct((bsz, c, n_kd), BF16), jax.ShapeDtypeStruct((bsz, c, n_kd), BF16)),
        grid=(bsz,),
        in_specs=[
            pl.BlockSpec((None, c, d), lambda b: (b, 0, 0)),
            pl.BlockSpec((1, d), lambda b: (0, 0)),
            pl.BlockSpec((1, d), lambda b: (0, 0)),
            pl.BlockSpec((d, n_kd), lambda b: (0, 0)),
            pl.BlockSpec((1, n_kd), lambda b: (0, 0)),
        ],
        out_specs=(pl.BlockSpec((None, c, n_kd), lambda b: (b, 0, 0)),
                   pl.BlockSpec((None, c, n_kd), lambda b: (b, 0, 0))),
        compiler_params=_cparams("arbitrary"),
        name="attn_ctx_kv",
    )(ctx, csc, csh, w_kv, b_kv)


def _attn_kernel(sink_ref, q_ref, kp_ref, kc_ref, kn_ref, vp_ref, vc_ref, vn_ref, kx_ref, vx_ref, o_ref,
                 *, n_kv, group, nb):
    n = pl.program_id(1)
    r = lax.broadcasted_iota(I32, (BLOCK, BLOCK), 0)
    j = lax.broadcasted_iota(I32, (BLOCK, BLOCK), 1)
    prev_ok = (j >= r) & (n > 0)
    next_ok = (j <= r) & (n < nb - 1)
    lo = lax.broadcasted_iota(I32, (1, LANES), 1) < (LANES // 2)
    pairs = group // 2
    for kh in range(n_kv):
        sl = slice(kh * LANES, (kh + 1) * LANES)
        kcat = jnp.concatenate([kp_ref[:, sl], kc_ref[:, sl], kn_ref[:, sl], kx_ref[:, sl]], axis=0)
        vcat = jnp.concatenate([vp_ref[:, sl], vc_ref[:, sl], vn_ref[:, sl], vx_ref[:, sl]], axis=0)
        nkeys = kcat.shape[0]
        parts = []
        for pp in range(pairs):
            q2 = q_ref[:, (kh * pairs + pp) * LANES:(kh * pairs + pp + 1) * LANES]
            zq = jnp.zeros_like(q2)
            parts += [jnp.where(lo, q2, zq), jnp.where(lo, zq, q2)]
        qs = jnp.concatenate(parts, axis=0)
        s_all = lax.dot_general(qs, kcat, (((1,), (1,)), ((), ())), preferred_element_type=F32)
        e_parts, rdens = [], []
        for g in range(group):
            s = s_all[g * BLOCK:(g + 1) * BLOCK]
            s = jnp.concatenate([
                jnp.where(prev_ok, s[:, :BLOCK], NEG_INF),
                s[:, BLOCK:2 * BLOCK],
                jnp.where(next_ok, s[:, 2 * BLOCK:3 * BLOCK], NEG_INF),
                s[:, 3 * BLOCK:]], axis=1)
            sk = sink_ref[kh * group + g]
            m = jnp.maximum(jnp.max(s, axis=1, keepdims=True), sk)
            e = jnp.exp(s - m)
            rdens.append(1.0 / (jnp.sum(e, axis=1, keepdims=True) + jnp.exp(sk - m)))
            e_parts.append(e.astype(BF16))
        o = jnp.dot(jnp.concatenate(e_parts, axis=0), vcat, preferred_element_type=F32)
        for pp in range(pairs):
            p = kh * pairs + pp
            o_lo = o[(2 * pp) * BLOCK:(2 * pp + 1) * BLOCK] * rdens[2 * pp]
            o_hi = o[(2 * pp + 1) * BLOCK:(2 * pp + 2) * BLOCK] * rdens[2 * pp + 1]
            o_ref[:, p * LANES:(p + 1) * LANES] = jnp.where(lo, o_lo, o_hi).astype(BF16)


def _attn_call(sink, q, kd, vd, kxd, vxd, n_kv, group):
    bsz, s, n_q = q.shape
    n_kd = kd.shape[2]
    c = kxd.shape[1]
    nb = s // BLOCK
    kern = functools.partial(_attn_kernel, n_kv=n_kv, group=group, nb=nb)
    prev = lambda b, n: (b, jnp.maximum(n - 1, 0), 0)
    cur = lambda b, n: (b, n, 0)
    nxt = lambda b, n: (b, jnp.minimum(n + 1, nb - 1), 0)
    kv = lambda im: pl.BlockSpec((None, BLOCK, n_kd), im)
    return pl.pallas_call(
        kern,
        out_shape=jax.ShapeDtypeStruct((bsz, s, n_q), BF16),
        grid=(bsz, nb),
        in_specs=[
            pl.BlockSpec(memory_space=pltpu.SMEM),
            pl.BlockSpec((None, BLOCK, n_q), cur),
            kv(prev), kv(cur), kv(nxt), kv(prev), kv(cur), kv(nxt),
            pl.BlockSpec((None, c, n_kd), lambda b, n: (b, 0, 0)),
            pl.BlockSpec((None, c, n_kd), lambda b, n: (b, 0, 0)),
        ],
        out_specs=pl.BlockSpec((None, BLOCK, n_q), cur),
        compiler_params=_cparams("arbitrary", "arbitrary"),
        name="window_attn",
    )(sink, q, kd, kd, kd, vd, vd, vd, kxd, vxd)


def _layer_norm(r, g, b):
    mu = jnp.mean(r, axis=-1, keepdims=True)
    xc = r - mu
    var = jnp.mean(xc * xc, axis=-1, keepdims=True)
    return xc * lax.rsqrt(var + LN_EPS) * g + b


def _route(logits_t, rb):
    s = jax.nn.sigmoid(logits_t)
    sel = s + rb
    n_e = N_GROUPS * EXPERTS_PER_GROUP
    sel_r = [sel[e:e + 1, :] for e in range(n_e)]
    s_r = [s[e:e + 1, :] for e in range(n_e)]
    gscore = []
    for g in range(N_GROUPS):
        a, b, c, d = sel_r[4 * g:4 * g + 4]
        m1, n1, m2, n2 = jnp.maximum(a, b), jnp.minimum(a, b), jnp.maximum(c, d), jnp.minimum(c, d)
        gscore.append(jnp.maximum(m1, m2) + jnp.maximum(jnp.minimum(m1, m2), jnp.maximum(n1, n2)))
    best, gi = gscore[0], jnp.zeros_like(gscore[0], dtype=I32)
    for g in range(1, N_GROUPS):
        upd = gscore[g] > best
        gi = jnp.where(upd, g, gi)
        best = jnp.where(upd, gscore[g], best)

    def pick(rows, i):
        out = rows[i]
        for g in range(1, N_GROUPS):
            out = jnp.where(gi == g, rows[4 * g + i], out)
        return out

    v = [pick(sel_r, i) for i in range(EXPERTS_PER_GROUP)]
    sv = [pick(s_r, i) for i in range(EXPERTS_PER_GROUP)]

    def argmax4(vals):
        bv, bi = vals[0], jnp.zeros_like(gi)
        for i in range(1, EXPERTS_PER_GROUP):
            upd = vals[i] > bv
            bi = jnp.where(upd, i, bi)
            bv = jnp.where(upd, vals[i], bv)
        return bi

    def take4(vals, idx):
        out = vals[0]
        for i in range(1, EXPERTS_PER_GROUP):
            out = jnp.where(idx == i, vals[i], out)
        return out

    i1 = argmax4(v)
    i2 = argmax4([jnp.where(i1 == i, -jnp.inf, v[i]) for i in range(EXPERTS_PER_GROUP)])
    s1, s2 = take4(sv, i1), take4(sv, i2)
    tot = s1 + s2
    g1, g2 = s1 / tot, s2 / tot
    first_lo = i1 < i2
    i_lo, i_hi = jnp.minimum(i1, i2), jnp.maximum(i1, i2)
    g_lo, g_hi = jnp.where(first_lo, g1, g2), jnp.where(first_lo, g2, g1)
    pair = jnp.where(i_lo == 0, i_hi - 1, jnp.where(i_lo == 1, i_hi + 1, N_PAIRS - 1))
    bucket = gi * N_PAIRS + pair
    return 4 * gi + i_lo, 4 * gi + i_hi, g_lo, g_hi, bucket


def _proj_ln_kernel(a_ref, w_ref, bias_ref, h_ref, gate_ref, lng_ref, lnb_ref, sc_ref, sh_ref, rw_ref, rb_ref,
                    h1_ref, xp_ref, meta_ref, cnt_ref, run_ref, *, alpha, a_packed):
    first = (pl.program_id(0) == 0) & (pl.program_id(1) == 0)

    @pl.when(first)
    def _():
        run_ref[...] = jnp.zeros_like(run_ref)

    tm, d = h_ref.shape
    half = d // 2
    chunk = tm // PROJ_CHUNKS
    logit_parts = []
    ys = []
    for c in range(PROJ_CHUNKS):
        rs = slice(c * chunk, (c + 1) * chunk)
        a = _unpack_cols(a_ref[rs]).astype(BF16) if a_packed else a_ref[rs]
        ys.append(jnp.dot(a, w_ref[...], preferred_element_type=F32) + bias_ref[...])
    for c in range(PROJ_CHUNKS):
        rs = slice(c * chunk, (c + 1) * chunk)
        y = ys[c]
        h1 = _layer_norm(alpha * h_ref[rs] + gate_ref[...] * y, lng_ref[...], lnb_ref[...])
        h1_ref[rs] = h1
        tb = (h1 * (1.0 + sc_ref[...]) + sh_ref[...]).astype(BF16)
        bits = lax.bitcast_convert_type(tb.astype(F32), U32)
        xp_ref[rs, :half] = (bits[:, :half] & jnp.uint32(0xFFFF0000)) | (bits[:, half:] >> 16)
        logits = jnp.dot(tb, rw_ref[...], preferred_element_type=F32)
        logit_parts.append(jnp.transpose(logits)[:N_GROUPS * EXPERTS_PER_GROUP, :])
    logits_t = jnp.concatenate(logit_parts, axis=1)
    e_lo, e_hi, g_lo, g_hi, bucket = _route(logits_t, rb_ref[...])

    rows = lax.broadcasted_iota(I32, (BUCKET_ROWS, tm), 0)
    onehot = (rows == bucket).astype(F32)
    tri = (lax.broadcasted_iota(I32, (tm, tm), 0) <= lax.broadcasted_iota(I32, (tm, tm), 1)).astype(BF16)
    cum = jnp.dot(onehot.astype(BF16), tri, preferred_element_type=F32)
    run = run_ref[:, 0:1]
    rank = jnp.sum(onehot * (cum - 1.0 + run), axis=0, keepdims=True)
    new_run = run + cum[:, tm - 1:tm]
    run_ref[...] = jnp.broadcast_to(new_run, run_ref.shape)
    cnt_ref[...] = jnp.broadcast_to(new_run, cnt_ref.shape)

    mrow = lax.broadcasted_iota(I32, (8, tm), 0)
    meta = jnp.where(mrow == 0, e_lo.astype(F32), 0.0)
    meta = jnp.where(mrow == 1, e_hi.astype(F32), meta)
    meta = jnp.where(mrow == 2, g_lo, meta)
    meta = jnp.where(mrow == 3, g_hi, meta)
    meta = jnp.where(mrow == 4, bucket.astype(F32), meta)
    meta = jnp.where(mrow == 5, rank, meta)
    meta_ref[...] = meta

    grow = lax.broadcasted_iota(I32, (LANES, tm), 0)
    gates_t = jnp.where(grow == 0, g_lo, jnp.where(grow == 1, g_hi, 0.0))
    xp_ref[:, half:] = lax.bitcast_convert_type(jnp.transpose(gates_t), U32)


def _proj_ln_call(a, w, bias, h, gate, lng, lnb, sc, sh, rw_pad, rb, alpha):
    bsz, s, d = h.shape
    a_packed = a.dtype == U32
    da = a.shape[2]
    dm = w.shape[0]
    tm = min(TOKEN_TILE, s)
    t = bsz * s
    nt = s // tm
    dp = d // 2 + LANES
    kern = functools.partial(_proj_ln_kernel, alpha=alpha, a_packed=a_packed)
    vec = lambda: pl.BlockSpec((1, d), lambda b, i: (0, 0))
    bvec = lambda: pl.BlockSpec((None, 1, d), lambda b, i: (b, 0, 0))
    return pl.pallas_call(
        kern,
        out_shape=(jax.ShapeDtypeStruct((bsz, s, d), F32),
                   jax.ShapeDtypeStruct((t, dp), U32),
                   jax.ShapeDtypeStruct((8, t), F32),
                   jax.ShapeDtypeStruct((BUCKET_ROWS, LANES), F32)),
        grid=(bsz, nt),
        in_specs=[
            pl.BlockSpec((None, tm, da), lambda b, i: (b, i, 0)),
            _resident((dm, d), lambda b, i: (0, 0)),
            vec(),
            pl.BlockSpec((None, tm, d), lambda b, i: (b, i, 0)),
            bvec(), vec(), vec(), bvec(), bvec(),
            pl.BlockSpec((d, LANES), lambda b, i: (0, 0)),
            pl.BlockSpec((N_GROUPS * EXPERTS_PER_GROUP, 1), lambda b, i: (0, 0)),
        ],
        out_specs=(pl.BlockSpec((None, tm, d), lambda b, i: (b, i, 0)),
                   pl.BlockSpec((tm, dp), lambda b, i: (b * nt + i, 0)),
                   pl.BlockSpec((8, tm), lambda b, i: (0, b * nt + i)),
                   pl.BlockSpec((BUCKET_ROWS, LANES), lambda b, i: (0, 0))),
        scratch_shapes=[pltpu.VMEM((BUCKET_ROWS, LANES), F32)],
        compiler_params=_cparams("arbitrary", "arbitrary"),
        name="proj_ln_route",
    )(a, w, bias, h, gate, lng, lnb, sc, sh, rw_pad, rb)


def _wait_tile(src_tile, dst_tile, sem):
    pltpu.make_async_copy(src_tile, dst_tile, sem).wait()


def _scatter_kernel(dest_ref, xp_ref, init_ref, xs_ref, sem):
    del init_ref
    tm = xp_ref.shape[0]
    base = pl.program_id(0) * tm

    def issue(r, c):
        pltpu.make_async_copy(xp_ref.at[pl.ds(r, 1)], xs_ref.at[pl.ds(dest_ref[base + r], 1)], sem).start()
        return c

    lax.fori_loop(0, tm, issue, 0, unroll=8)
    _wait_tile(xp_ref, xs_ref.at[pl.ds(0, tm)], sem)


def _scatter_call(dest, xp, rows_out):
    t, dp = xp.shape
    tm = min(TOKEN_TILE, t)
    init = jnp.zeros((rows_out, dp), U32)
    return pl.pallas_call(
        _scatter_kernel,
        out_shape=jax.ShapeDtypeStruct((rows_out, dp), U32),
        grid_spec=pltpu.PrefetchScalarGridSpec(
            num_scalar_prefetch=1,
            grid=(t // tm,),
            in_specs=[pl.BlockSpec((tm, dp), lambda i, dest: (i, 0)),
                      pl.BlockSpec(memory_space=pl.ANY)],
            out_specs=pl.BlockSpec(memory_space=pl.ANY),
            scratch_shapes=[pltpu.SemaphoreType.DMA],
        ),
        input_output_aliases={2: 0},
        compiler_params=_cparams("arbitrary"),
        name="moe_scatter",
    )(dest, xp, init)


def _moe_kernel(ex_ref, valid_ref, xs_ref, w1_ref, w3_ref, w2_ref, y_ref, acc_ref):
    del ex_ref
    i = pl.program_id(0)
    s = pl.program_id(1)
    half = xs_ref.shape[1] - LANES
    ok = valid_ref[i] > 0

    @pl.when(ok)
    def _():
        w = xs_ref[:, :half]
        hi = lax.bitcast_convert_type(w & jnp.uint32(0xFFFF0000), F32).astype(BF16)
        lo = lax.bitcast_convert_type(w << 16, F32).astype(BF16)
        x = jnp.concatenate([hi, lo], axis=1)
        gates = lax.bitcast_convert_type(xs_ref[:, half:], F32)
        which = (s + i) % 2
        g = jnp.where(which == 0, gates[:, 0:1], gates[:, 1:2])
        a = jnp.dot(x, w1_ref[...], preferred_element_type=F32)
        b = jnp.dot(x, w3_ref[...], preferred_element_type=F32)
        hid = (_silu(a) * b * g).astype(BF16)
        y = jnp.dot(hid, w2_ref[...], preferred_element_type=F32)

        @pl.when(s == 0)
        def _():
            acc_ref[...] = y

        @pl.when(s != 0)
        def _():
            y_ref[...] = _pack_cols(acc_ref[...] + y)

    @pl.when(jnp.logical_not(ok) & (s == 0))
    def _():
        y_ref[...] = jnp.zeros_like(y_ref)


def _moe_call(ex, valid, xs, w1, w3, w2, layer):
    rows, dp = xs.shape
    _, n_e, d, ff = w1.shape
    tm = min(TOKEN_TILE, rows)
    ntiles = rows // tm
    return pl.pallas_call(
        _moe_kernel,
        out_shape=jax.ShapeDtypeStruct((rows, d // 2), U32),
        grid_spec=pltpu.PrefetchScalarGridSpec(
            num_scalar_prefetch=2,
            grid=(ntiles, 2),
            in_specs=[
                pl.BlockSpec((tm, dp), lambda i, s, ex, va: (i, 0)),
                pl.BlockSpec((None, None, d, ff), lambda i, s, ex, va: (layer, ex[2 * i + s], 0, 0)),
                pl.BlockSpec((None, None, d, ff), lambda i, s, ex, va: (layer, ex[2 * i + s], 0, 0)),
                pl.BlockSpec((None, None, ff, d), lambda i, s, ex, va: (layer, ex[2 * i + s], 0, 0)),
            ],
            out_specs=pl.BlockSpec((tm, d // 2), lambda i, s, ex, va: (i, 0)),
            scratch_shapes=[pltpu.VMEM((tm, d), F32)],
        ),
        compiler_params=_cparams("arbitrary", "arbitrary"),
        name="moe_experts",
    )(ex, valid, xs, w1, w3, w2)


GATHER_CHUNKS = 8


def _gather_ln_kernel(dest_ref, ys_ref, h_ref, gate_ref, lng_ref, lnb_ref, o_ref, buf, sem, *, alpha, nt):
    tm = h_ref.shape[0]
    step = pl.program_id(0) * nt + pl.program_id(1)
    last = pl.num_programs(0) * nt - 1
    slot = step % 2

    def row_copy(tile, sl, r):
        return pltpu.make_async_copy(ys_ref.at[pl.ds(dest_ref[tile * tm + r], 1)], buf.at[sl, pl.ds(r, 1)],
                                     sem.at[sl])

    def wait_tile(sl):
        _wait_tile(ys_ref.at[pl.ds(0, tm)], buf.at[sl], sem.at[sl])

    @pl.when(step == 0)
    def _():
        def issue(r, c):
            row_copy(0, 0, r).start()
            return c

        lax.fori_loop(0, tm, issue, 0, unroll=8)

    wait_tile(slot)
    nxt = jnp.minimum(step + 1, last)
    rows = tm // GATHER_CHUNKS
    for c in range(GATHER_CHUNKS):
        for r in range(c * rows, (c + 1) * rows):
            row_copy(nxt, 1 - slot, r).start()
        sl = slice(c * rows, (c + 1) * rows)
        f = _unpack_cols(buf[slot, sl])
        o_ref[sl] = _layer_norm(alpha * h_ref[sl] + gate_ref[...] * f, lng_ref[...], lnb_ref[...])

    @pl.when(step == last)
    def _():
        wait_tile(1 - slot)


def _gather_ln_call(dest, ys, h, gate, lng, lnb, alpha):
    bsz, s, d = h.shape
    tm = min(TOKEN_TILE, s)
    nt = s // tm
    kern = functools.partial(_gather_ln_kernel, alpha=alpha, nt=nt)
    return pl.pallas_call(
        kern,
        out_shape=jax.ShapeDtypeStruct((bsz, s, d), F32),
        grid_spec=pltpu.PrefetchScalarGridSpec(
            num_scalar_prefetch=1,
            grid=(bsz, nt),
            in_specs=[
                pl.BlockSpec(memory_space=pl.ANY),
                pl.BlockSpec((None, tm, d), lambda b, i, dest: (b, i, 0)),
                pl.BlockSpec((None, 1, d), lambda b, i, dest: (b, 0, 0)),
                pl.BlockSpec((1, d), lambda b, i, dest: (0, 0)),
                pl.BlockSpec((1, d), lambda b, i, dest: (0, 0)),
            ],
            out_specs=pl.BlockSpec((None, tm, d), lambda b, i, dest: (b, i, 0)),
            scratch_shapes=[pltpu.VMEM((2, tm, d // 2), U32), pltpu.SemaphoreType.DMA((2,))],
        ),
        compiler_params=_cparams("arbitrary", "arbitrary"),
        name="moe_gather_ln",
    )(dest, ys, h, gate, lng, lnb)


_PAIR_LO = (0, 0, 0, 1, 1, 2)
_PAIR_HI = (1, 2, 3, 2, 3, 3)


def _moe_layer(xp, meta, cnt, h1, gate2, lng, lnb, w1, w3, w2, layer, alpha):
    t = xp.shape[0]
    tm = min(TOKEN_TILE, t)
    ntiles = t // tm + N_BUCKETS
    counts = cnt[:N_BUCKETS, 0].astype(I32)
    tiles_b = (counts + tm - 1) // tm
    tile_end = jnp.cumsum(tiles_b)
    offs = (tile_end - tiles_b) * tm
    bucket = meta[4].astype(I32)
    dest = offs[bucket] + meta[5].astype(I32)

    tile = jnp.arange(ntiles, dtype=I32)
    valid = (tile < tile_end[-1]).astype(I32)
    tb = jnp.minimum(jnp.sum((tile[:, None] >= tile_end[None, :]).astype(I32), axis=1), N_BUCKETS - 1)
    lo = jnp.asarray(_PAIR_LO, I32)[tb % N_PAIRS] + EXPERTS_PER_GROUP * (tb // N_PAIRS)
    hi = jnp.asarray(_PAIR_HI, I32)[tb % N_PAIRS] + EXPERTS_PER_GROUP * (tb // N_PAIRS)
    odd = (tile % 2) == 1
    ex = jnp.stack([jnp.where(odd, hi, lo), jnp.where(odd, lo, hi)], axis=1).reshape(-1)

    xs = _scatter_call(dest, xp, ntiles * tm)
    ys = _moe_call(ex, valid, xs, w1, w3, w2, layer)
    return _gather_ln_call(dest, ys, h1, gate2, lng, lnb, alpha)


def _hy_in_kernel(xm_ref, xp_ref, xn_ref, sc_ref, sh_ref, w_ref, b_ref, cw_ref, cb_ref, x0_ref, z_ref, *, nt, tn):
    i = pl.program_id(1)
    tm, d = xm_ref.shape
    sc, sh = 1.0 + sc_ref[...], sh_ref[...]
    u = jnp.concatenate([xp_ref[...] * sc + sh, xm_ref[...] * sc + sh, xn_ref[...] * sc + sh], axis=0).astype(BF16)
    rows = tm + 16
    rid = lax.broadcasted_iota(I32, (rows, 1), 0)
    keep = ((rid >= 8) | (i > 0)) & ((rid < tm + 8) | (i < nt - 1))

    def conv(sec, j):
        col = sec * d + j * tn
        p = jnp.dot(u, w_ref[:, col:col + tn], preferred_element_type=F32) + b_ref[:, col:col + tn]
        p = jnp.where(keep, p, 0.0)
        cw = cw_ref[:, col:col + tn]
        out = (cw[0:1] * pltpu.roll(p, 1, 0) + cw[1:2] * p + cw[2:3] * pltpu.roll(p, rows - 1, 0)
               + cb_ref[:, col:col + tn])
        return out[8:8 + tm]

    nch = d // tn
    for j in range(nch // 2):
        x0_ref[:, j * tn:(j + 1) * tn] = _pack_pair(conv(0, j), conv(0, j + nch // 2))
        z_ref[:, j * tn:(j + 1) * tn] = _pack_pair(conv(1, j) * conv(2, j),
                                                   conv(1, j + nch // 2) * conv(2, j + nch // 2))


def _hy_in_call(h, sc, sh, w, b, cw, cb):
    bsz, s, d = h.shape
    tm = min(TOKEN_TILE, s)
    nt = s // tm
    hb = tm // 8
    tn = 1024
    kern = functools.partial(_hy_in_kernel, nt=nt, tn=tn)
    return pl.pallas_call(
        kern,
        out_shape=(jax.ShapeDtypeStruct((bsz, s, d // 2), U32), jax.ShapeDtypeStruct((bsz, s, d // 2), U32)),
        grid=(bsz, nt),
        in_specs=[
            pl.BlockSpec((None, tm, d), lambda b, i: (b, i, 0)),
            pl.BlockSpec((None, 8, d), lambda b, i: (b, jnp.maximum(i * hb - 1, 0), 0)),
            pl.BlockSpec((None, 8, d), lambda b, i: (b, jnp.minimum((i + 1) * hb, nt * hb - 1), 0)),
            pl.BlockSpec((None, 1, d), lambda b, i: (b, 0, 0)),
            pl.BlockSpec((None, 1, d), lambda b, i: (b, 0, 0)),
            _resident((d, 3 * d), lambda b, i: (0, 0)),
            pl.BlockSpec((1, 3 * d), lambda b, i: (0, 0)),
            pl.BlockSpec((3, 3 * d), lambda b, i: (0, 0)),
            pl.BlockSpec((1, 3 * d), lambda b, i: (0, 0)),
        ],
        out_specs=(pl.BlockSpec((None, tm, d // 2), lambda b, i: (b, i, 0)),
                   pl.BlockSpec((None, tm, d // 2), lambda b, i: (b, i, 0))),
        compiler_params=_cparams("arbitrary", "arbitrary"),
        name="hyena_in",
    )(h, h, h, sc, sh, w, b, cw, cb)


def _filter_kernel(w1_ref, b1_ref, w2_ref, b2_ref, w3_ref, b3_ref, fr_ref, w4_ref, dl_ref, f_ref, l1_ref,
                   *, seq, tn):
    i = pl.program_id(0)

    @pl.when(i == 0)
    def _():
        l1_ref[...] = jnp.zeros_like(l1_ref)

    n_lane = i * tn + lax.broadcasted_iota(I32, (1, tn), 1)
    m_lane = jnp.where(n_lane < seq, n_lane, 2 * seq - n_lane).astype(F32)
    t_lane = m_lane / (seq - 1.0)
    wl = (2.0 * math.pi) * m_lane / float(seq)
    band = lax.broadcasted_iota(I32, (HY_BANDS, 1), 0).astype(F32)
    fb = 1e-4 + band * ((HY_BANDS - 1 - 1e-4) / (HY_BANDS - 1))
    ang = fb * wl
    z = jnp.concatenate([t_lane, jnp.cos(ang), -jnp.sin(ang),
                         jnp.zeros((7, tn), F32)], axis=0)
    fr = fr_ref[...]

    def layer(w_ref, b_ref, x):
        pre = lax.dot_general(w_ref[...], x, (((0,), (0,)), ((), ())), preferred_element_type=F32,
                              precision=HIGHEST)
        return jnp.sin(fr * (pre + b_ref[...]))

    hdn = layer(w1_ref, b1_ref, z)
    hdn = layer(w2_ref, b2_ref, hdn)
    hdn = layer(w3_ref, b3_ref, hdn)
    h = lax.dot_general(hdn.astype(BF16), w4_ref[...].astype(BF16), (((0,), (0,)), ((), ())),
                        preferred_element_type=F32)
    n_col = i * tn + lax.broadcasted_iota(I32, (tn, 1), 0)
    m_col = jnp.where(n_col < seq, n_col, 2 * seq - n_col).astype(F32)
    h = h * jnp.exp(-(m_col / (seq - 1.0)) * dl_ref[...])
    h = jnp.where(n_col == seq, 0.0, h)
    l1_ref[...] += jnp.sum(jnp.abs(h), axis=0, keepdims=True)
    f_ref[...] = _pack_cols(h)


def _filter_call(fw1, fb1, fw2, fb2, fw3, fb3, freq, fw4, seq):
    width = fw2.shape[0]
    d = fw4.shape[1] // 2
    tn = min(512, seq)
    steps = 2 * seq // tn
    max_decay = math.log(HY_TARGET) / HY_FAST_DECAY
    min_decay = math.log(HY_TARGET) / HY_SLOW_DECAY
    deltas = jnp.abs(jnp.linspace(min_decay, max_decay, d, dtype=F32)).reshape(1, d)
    w1p = jnp.concatenate([fw1, jnp.zeros((7, width), F32)], axis=0)
    col = lambda a: a.reshape(width, 1)
    small = lambda shp: pl.BlockSpec(shp, lambda i: (0, 0))
    kern = functools.partial(_filter_kernel, seq=seq, tn=tn)
    return pl.pallas_call(
        kern,
        out_shape=(jax.ShapeDtypeStruct((2 * seq, d // 2), U32), jax.ShapeDtypeStruct((1, d), F32)),
        grid=(steps,),
        in_specs=[small((40, width)), small((width, 1)), small((width, width)), small((width, 1)),
                  small((width, width)), small((width, 1)), small((width, 1)),
                  pl.BlockSpec((width, d), lambda i: (0, (i * tn) // seq)),
                  small((1, d))],
        out_specs=(pl.BlockSpec((tn, d // 2), lambda i: (i, 0)), small((1, d))),
        compiler_params=_cparams("arbitrary"),
        name="hyena_filter",
    )(w1p, col(fb1), fw2, col(fb2), fw3, col(fb3), col(freq), fw4, deltas)


def _dft_tables(n1, n2):
    n = n1 * n2
    k = np.arange(n1)[:, None]
    m = np.arange(n1)[None, :]
    ang1 = -2.0 * np.pi * ((k * m) % n1) / n1
    f1r, f1i = np.cos(ang1), np.sin(ang1)
    hn = n1 // 2
    a_data = np.block([[f1r[:, :hn], -f1i[:, :hn]], [f1i[:, :hn], f1r[:, :hn]]])
    a_filt = np.concatenate([f1r, f1i], axis=0)
    a_inv = np.block([[f1r.T[:hn], f1i.T[:hn]], [-f1i.T[:hn], f1r.T[:hn]]]) / n
    k2 = np.arange(n2)[:, None]
    m2 = np.arange(n2)[None, :]
    ang2 = -2.0 * np.pi * ((k2 * m2) % n2) / n2
    f2r, f2i = np.cos(ang2), np.sin(ang2)
    b_fwd = np.block([[f2r, -f2i], [f2i, f2r]])
    b_inv = np.block([[f2r, f2i], [-f2i, f2r]])
    angt = -2.0 * np.pi * ((np.arange(n2)[:, None] * np.arange(n1)[None, :]) % n) / n
    tw = (np.cos(angt), np.sin(angt))
    perm = (np.arange(n1 // FFT_K1_GROUP)[None, :] * FFT_K1_GROUP + np.arange(FFT_K1_GROUP)[:, None]).reshape(-1)
    perm2 = np.concatenate([perm, perm + n1])
    as_bf16 = lambda a: jnp.asarray(a, F32).astype(BF16)
    return dict(a_data=as_bf16(a_data[perm2]), a_filt=as_bf16(a_filt[perm2]), a_inv=as_bf16(a_inv[:, perm2]),
                b_fwd=as_bf16(b_fwd), b_inv=as_bf16(b_inv),
                tw_n2=tuple(jnp.asarray(t[:, perm], F32).reshape(n2, n1, 1) for t in tw),
                tw_k1=tuple(jnp.asarray(t.T.copy(), F32).reshape(n1, n2, 1) for t in tw))


FFT_K1_GROUP = 4


def _pack_pair(hi, lo):
    hb = lax.bitcast_convert_type(hi.astype(BF16).astype(F32), U32)
    lb = lax.bitcast_convert_type(lo.astype(BF16).astype(F32), U32)
    return (hb & jnp.uint32(0xFFFF0000)) | (lb >> 16)


def _unpack_pair(w):
    hi = lax.bitcast_convert_type(w & jnp.uint32(0xFFFF0000), F32)
    lo = lax.bitcast_convert_type(w << 16, F32)
    return hi, lo


def _fetch_step(view, buf, sem):
    k = pl.program_id(0)
    slot = k % 2

    def copy(step, sl):
        return pltpu.make_async_copy(view(step), buf.at[sl], sem.at[sl])

    @pl.when(k == 0)
    def _():
        copy(0, 0).start()

    @pl.when(k + 1 < pl.num_programs(0))
    def _():
        copy(k + 1, 1 - slot).start()

    copy(k, slot).wait()
    return buf.at[slot]


def _store_step(view, buf, sem, fill):
    k = pl.program_id(0)
    slot = k % 2

    def copy(step, sl):
        return pltpu.make_async_copy(buf.at[sl], view(step), sem.at[sl])

    @pl.when(k >= 2)
    def _():
        copy(k - 2, slot).wait()

    fill(buf.at[slot])
    copy(k, slot).start()

    @pl.when(k == pl.num_programs(0) - 1)
    def _():
        copy(k, slot).wait()

        @pl.when(k >= 1)
        def _():
            copy(k - 1, 1 - slot).wait()


def _unpack_cols(w):
    hi, lo = _unpack_pair(w)
    return jnp.concatenate([hi, lo], axis=1)


def _pack_cols(x):
    half = x.shape[1] // 2
    return _pack_pair(x[:, :half], x[:, half:])


def _fft_a_kernel(x_hbm, a_ref, twr_ref, twi_ref, y_ref, xbuf, sem):
    n1 = a_ref.shape[0] // 2
    x_ref = _fetch_step(lambda j: x_hbm.at[:, :, j, :], xbuf, sem)
    rhs = jnp.concatenate([_unpack_cols(x_ref[0]), _unpack_cols(x_ref[1])], axis=0).astype(BF16)
    y = jnp.dot(a_ref[...], rhs, preferred_element_type=F32)
    yr, yi = y[:n1], y[n1:]
    tr, ti = twr_ref[...], twi_ref[...]
    w = _pack_pair(yr * tr - yi * ti, yr * ti + yi * tr)
    q = n1 // FFT_K1_GROUP
    d = w.shape[1]
    for kk in range(FFT_K1_GROUP):
        y_ref[:, kk * d:(kk + 1) * d] = w[kk * q:(kk + 1) * q]


def _fft_a_call(x4, a_mat, tw_n2):
    _, hn, n2, half = x4.shape
    d = 2 * half
    n1 = 2 * hn
    q = n1 // FFT_K1_GROUP
    return pl.pallas_call(
        _fft_a_kernel,
        out_shape=jax.ShapeDtypeStruct((n2, q, FFT_K1_GROUP * d), U32),
        grid=(n2,),
        in_specs=[
            pl.BlockSpec(memory_space=pl.ANY),
            pl.BlockSpec((2 * n1, n1), lambda j: (0, 0)),
            pl.BlockSpec((None, n1, 1), lambda j: (j, 0, 0)),
            pl.BlockSpec((None, n1, 1), lambda j: (j, 0, 0)),
        ],
        out_specs=pl.BlockSpec((None, q, FFT_K1_GROUP * d), lambda j: (j, 0, 0)),
        scratch_shapes=[pltpu.VMEM((2, 2, hn, half), U32), pltpu.SemaphoreType.DMA((2,))],
        compiler_params=_cparams("arbitrary"),
        name="fft_stage_a",
    )(x4, a_mat, tw_n2[0], tw_n2[1])


def _load_k1(y_ref, kk):
    d = y_ref.shape[1] // FFT_K1_GROUP
    yr, yi = _unpack_pair(y_ref[:, kk * d:(kk + 1) * d])
    return jnp.concatenate([yr, yi], axis=0).astype(BF16)


def _fft_b_kernel(y_hbm, b_ref, h_ref, ybuf, sem):
    y_ref = _fetch_step(lambda kb: y_hbm.at[:, kb, :], ybuf, sem)
    n2 = y_ref.shape[0]
    for kk in range(FFT_K1_GROUP):
        x = jnp.dot(b_ref[...], _load_k1(y_ref, kk), preferred_element_type=F32)
        h_ref[kk] = _pack_pair(x[:n2], x[n2:])


def _fft_b_call(y, b_fwd):
    n2, q, gd = y.shape
    d = gd // FFT_K1_GROUP
    return pl.pallas_call(
        _fft_b_kernel,
        out_shape=jax.ShapeDtypeStruct((q * FFT_K1_GROUP, n2, d), U32),
        grid=(q,),
        in_specs=[pl.BlockSpec(memory_space=pl.ANY),
                  pl.BlockSpec((2 * n2, 2 * n2), lambda k: (0, 0))],
        out_specs=pl.BlockSpec((FFT_K1_GROUP, n2, d), lambda k: (k, 0, 0)),
        scratch_shapes=[pltpu.VMEM((2, n2, gd), U32), pltpu.SemaphoreType.DMA((2,))],
        compiler_params=_cparams("arbitrary"),
        name="fft_filter_b",
    )(y, b_fwd)


def _fft_bc_kernel(y_hbm, h_ref, bf_ref, bi_ref, twr_ref, twi_ref, g_hbm, ybuf, gbuf, sem_in, sem_out):
    y_ref = _fetch_step(lambda kb: y_hbm.at[:, kb, :], ybuf, sem_in)
    n2 = y_ref.shape[0]
    d = y_ref.shape[1] // FFT_K1_GROUP

    def fill(g_ref):
        for kk in range(FFT_K1_GROUP):
            x = jnp.dot(bf_ref[...], _load_k1(y_ref, kk), preferred_element_type=F32)
            xr, xi = x[:n2], x[n2:]
            hr, hi = _unpack_pair(h_ref[kk])
            z = jnp.concatenate([xr * hr - xi * hi, xr * hi + xi * hr], axis=0).astype(BF16)
            g = jnp.dot(bi_ref[...], z, preferred_element_type=F32)
            gr, gi = g[:n2], g[n2:]
            tr, ti = twr_ref[kk], twi_ref[kk]
            g_ref[:, kk * d:(kk + 1) * d] = _pack_pair(gr * tr + gi * ti, gi * tr - gr * ti)

    _store_step(lambda kb: g_hbm.at[:, kb, :], gbuf, sem_out, fill)


def _fft_bc_call(y, hspec, b_fwd, b_inv, tw_k1):
    n2, q, gd = y.shape
    d = gd // FFT_K1_GROUP
    mat = lambda: pl.BlockSpec((2 * n2, 2 * n2), lambda k: (0, 0))
    tw = lambda: pl.BlockSpec((FFT_K1_GROUP, n2, 1), lambda k: (k, 0, 0))
    return pl.pallas_call(
        _fft_bc_kernel,
        out_shape=jax.ShapeDtypeStruct((n2, q, gd), U32),
        grid=(q,),
        in_specs=[pl.BlockSpec(memory_space=pl.ANY),
                  pl.BlockSpec((FFT_K1_GROUP, n2, d), lambda k: (k, 0, 0)), mat(), mat(), tw(), tw()],
        out_specs=pl.BlockSpec(memory_space=pl.ANY),
        scratch_shapes=[pltpu.VMEM((2, n2, gd), U32), pltpu.VMEM((2, n2, gd), U32),
                        pltpu.SemaphoreType.DMA((2,)), pltpu.SemaphoreType.DMA((2,))],
        compiler_params=_cparams("arbitrary"),
        name="fft_stage_bc",
    )(y, hspec, b_fwd, b_inv, tw_k1[0], tw_k1[1])


def _fft_d_kernel(g_ref, a_ref, x0_hbm, z_hbm, l1_ref, fb_ref, o_hbm, x0buf, zbuf, obuf, sem_x, sem_z, sem_o):
    hn = a_ref.shape[0] // 2
    d = g_ref.shape[1] // FFT_K1_GROUP
    x0_ref = _fetch_step(lambda j: x0_hbm.at[:, :, j, :], x0buf, sem_x)
    z_ref = _fetch_step(lambda j: z_hbm.at[:, :, j, :], zbuf, sem_z)
    parts = [_unpack_pair(g_ref[:, kk * d:(kk + 1) * d]) for kk in range(FFT_K1_GROUP)]
    rhs = jnp.concatenate([p[0] for p in parts] + [p[1] for p in parts], axis=0).astype(BF16)
    y = jnp.dot(a_ref[...], rhs, preferred_element_type=F32)
    inv_l1 = 1.0 / l1_ref[...]
    fb = fb_ref[...]

    def fill(o_ref):
        for b in range(2):
            conv = y[b * hn:(b + 1) * hn] * inv_l1
            o_ref[b] = _pack_cols(_unpack_cols(x0_ref[b]) * (conv + fb * _unpack_cols(z_ref[b])))

    _store_step(lambda j: o_hbm.at[:, :, j, :], obuf, sem_o, fill)


def _fft_d_call(g, a_inv, x0_4, z_4, l1, fbias):
    n2, q, gd = g.shape
    d = gd // FFT_K1_GROUP
    n1 = q * FFT_K1_GROUP
    hn = n1 // 2
    half = d // 2
    tok_buf = lambda: pltpu.VMEM((2, 2, hn, half), U32)
    return pl.pallas_call(
        _fft_d_kernel,
        out_shape=jax.ShapeDtypeStruct((2, hn, n2, half), U32),
        grid=(n2,),
        in_specs=[pl.BlockSpec((None, q, gd), lambda j: (j, 0, 0)),
                  pl.BlockSpec((n1, 2 * n1), lambda j: (0, 0)),
                  pl.BlockSpec(memory_space=pl.ANY), pl.BlockSpec(memory_space=pl.ANY),
                  pl.BlockSpec((1, d), lambda j: (0, 0)),
                  pl.BlockSpec((1, d), lambda j: (0, 0))],
        out_specs=pl.BlockSpec(memory_space=pl.ANY),
        scratch_shapes=[tok_buf(), tok_buf(), tok_buf(),
                        pltpu.SemaphoreType.DMA((2,)), pltpu.SemaphoreType.DMA((2,)), pltpu.SemaphoreType.DMA((2,))],
        compiler_params=_cparams("arbitrary"),
        name="fft_stage_d",
    )(g, a_inv, x0_4, z_4, l1, fbias)


def _hyena_conv(x0p, zp, filtp, l1, fbias):
    bsz, seq, half = zp.shape
    assert bsz == 2, "the two batch rows are packed as one complex signal"
    n2 = FFT_N2
    n1 = 2 * seq // n2
    tabs = _dft_tables(n1, n2)
    hn = n1 // 2
    view = lambda a: a.reshape(2, hn, n2, half)
    hspec = _fft_b_call(_fft_a_call(view(filtp), tabs["a_filt"], tabs["tw_n2"]), tabs["b_fwd"])
    y = _fft_a_call(view(zp), tabs["a_data"], tabs["tw_n2"])
    g = _fft_bc_call(y, hspec, tabs["b_fwd"], tabs["b_inv"], tabs["tw_k1"])
    out = _fft_d_call(g, tabs["a_inv"], view(x0p), view(zp), l1, fbias)
    return out.reshape(bsz, seq, half)


def _rope_tables(seq, head_dim):
    axis_dim = head_dim // 2
    rows = seq // GRID_W
    inv = ROPE_BASE ** (-jnp.arange(0, axis_dim, 2, dtype=F32) / axis_dim)
    row = jnp.repeat(jnp.arange(rows, dtype=F32), GRID_W)[:, None] * inv
    col = jnp.tile(jnp.arange(GRID_W, dtype=F32), rows)[:, None] * inv
    quarter = axis_dim // 2
    cos = jnp.concatenate([jnp.cos(row), jnp.cos(row), jnp.cos(col), jnp.cos(col)], axis=1)
    sin = jnp.concatenate([jnp.sin(row), jnp.sin(row), jnp.sin(col), jnp.sin(col)], axis=1)
    second = (np.arange(head_dim) % axis_dim) >= quarter
    reps = LANES // head_dim
    cos = jnp.tile(cos, (1, reps))
    sin = jnp.tile(sin, (1, reps))
    second = jnp.asarray(np.tile(second, reps))[None, :]
    return cos, jnp.where(second, sin, 0.0), jnp.where(second, 0.0, -sin)


def kernel(x, c, ctx, c_ctx, ada_w, ada_b, attn_w_in, attn_b_in, attn_sink, attn_w_out, hy_w_in, hy_b_in, hy_conv_w, hy_conv_b, hy_f_w1, hy_f_b1, hy_f_w2, hy_f_b2, hy_f_w3, hy_f_b3, hy_f_freq, hy_f_w4, hy_f_bias, hy_w_out, hy_b_out, ln1_g, ln1_b, ln2_g, ln2_b, router_w, router_b, moe_w1, moe_w3, moe_w2):
    bsz, seq, d = x.shape
    depth = ada_w.shape[0]
    assert depth == 2 and attn_w_in.shape[0] == 1 and hy_w_in.shape[0] == 1
    alpha = (2 * depth) ** 0.25
    n_heads = attn_sink.shape[1]
    attn_dim = attn_w_out.shape[1]
    head_dim = attn_dim // n_heads
    kv_dim = (attn_w_in.shape[2] - attn_dim) // 2
    n_kv = kv_dim // head_dim
    group = n_heads // n_kv
    assert head_dim * 2 == LANES and group % 2 == 0
    n_exp = router_w.shape[1]
    assert n_exp == N_GROUPS * EXPERTS_PER_GROUP

    cond = jnp.concatenate([c, c_ctx[None, :], jnp.zeros((8 - bsz - 1, d), F32)], axis=0)
    mods = _ada_call(cond, ada_w, ada_b).reshape(depth, 8, 6, d)
    mod = lambda layer, k: mods[layer, :bsz, k].reshape(bsz, 1, d)
    cmod = lambda layer, k: mods[layer, bsz, k].reshape(1, d)
    row = lambda v: v.reshape(1, -1)

    rw_pad = jnp.concatenate([router_w, jnp.zeros((d, LANES - n_exp), F32)], axis=1).astype(BF16)
    rb = router_b.reshape(n_exp, 1)
    w1b, w3b, w2b = moe_w1.astype(BF16), moe_w3.astype(BF16), moe_w2.astype(BF16)

    w_ext = attn_w_in[0].astype(BF16)
    b_ext = row(attn_b_in[0])
    w_kv, b_kv = w_ext[:, attn_dim:], b_ext[:, attn_dim:]
    n_kd = 2 * kv_dim
    cos_t, sa_t, sb_t = _rope_tables(seq, head_dim)

    q, kd, vd = _qkv_call(x, mod(0, 1), mod(0, 0), w_ext, b_ext, cos_t, sa_t, sb_t, attn_dim, n_kd, head_dim)
    kxd, vxd = _ctx_kv_call(ctx, cmod(0, 1), cmod(0, 0), w_kv, b_kv, n_kd)
    att = _attn_call(attn_sink[0], q, kd, vd, kxd, vxd, n_kv, group)
    h1, xp, meta, cnt = _proj_ln_call(att, attn_w_out[0].astype(BF16), jnp.zeros((1, d), F32), x, mod(0, 2),
                                      row(ln1_g[0]), row(ln1_b[0]), mod(0, 4), mod(0, 3), rw_pad, rb, alpha)
    h = _moe_layer(xp, meta, cnt, h1, mod(0, 5), row(ln2_g[0]), row(ln2_b[0]), w1b, w3b, w2b, 0, alpha)

    x0, z = _hy_in_call(h, mod(1, 1), mod(1, 0), hy_w_in[0].astype(BF16), row(hy_b_in[0]), hy_conv_w[0],
                        row(hy_conv_b[0]))
    filt, l1 = _filter_call(hy_f_w1[0], hy_f_b1[0], hy_f_w2[0], hy_f_b2[0], hy_f_w3[0], hy_f_b3[0],
                            hy_f_freq[0], hy_f_w4[0], seq)
    yh = _hyena_conv(x0, z, filt, l1, row(hy_f_bias[0]))
    h1, xp, meta, cnt = _proj_ln_call(yh, hy_w_out[0].astype(BF16), row(hy_b_out[0]), h, mod(1, 2),
                                      row(ln1_g[1]), row(ln1_b[1]), mod(1, 4), mod(1, 3), rw_pad, rb, alpha)
    return _moe_layer(xp, meta, cnt, h1, mod(1, 5), row(ln2_g[1]), row(ln2_b[1]), w1b, w3b, w2b, 1, alpha)
```

```python
import functools
import math

import numpy as np
import jax
import jax.numpy as jnp
from jax import lax
from jax.experimental import pallas as pl
from jax.experimental.pallas import tpu as pltpu

F32 = jnp.float32
BF16 = jnp.bfloat16
I32 = jnp.int32
U32 = jnp.uint32
HIGHEST = lax.Precision.HIGHEST

LANES = 128
V7X_VMEM_LIMIT_BYTES = 56 * 1024 * 1024

GRID_W = 64
BLOCK = 128
ROPE_BASE = 10000.0
NEG_INF = -1e30
HY_BANDS = 16
HY_FAST_DECAY = 0.3
HY_SLOW_DECAY = 1.5
HY_TARGET = 1e-2
N_GROUPS = 4
EXPERTS_PER_GROUP = 4
N_PAIRS = 6
N_BUCKETS = N_GROUPS * N_PAIRS
BUCKET_ROWS = 32
LN_EPS = 1e-5
FFT_N2 = 128

TOKEN_TILE = 512
PROJ_CHUNKS = 4
DMA_QUEUES = 2


def _cparams(*sem):
    return pltpu.CompilerParams(dimension_semantics=sem, vmem_limit_bytes=V7X_VMEM_LIMIT_BYTES)


def _resident(block_shape, index_map):
    return pl.BlockSpec(block_shape, index_map, pipeline_mode=pl.Buffered(1))


def _silu(x):
    return x * jax.nn.sigmoid(x)


def _ada_kernel(c_ref, w_ref, b_ref, o_ref):
    c = _silu(c_ref[...])
    o_ref[...] = jnp.dot(c, w_ref[...], preferred_element_type=F32, precision=HIGHEST) + b_ref[...]


def _ada_call(cond, ada_w, ada_b):
    depth, d, n6 = ada_w.shape
    tn = 1024
    rows = cond.shape[0]
    return pl.pallas_call(
        _ada_kernel,
        out_shape=jax.ShapeDtypeStruct((depth, rows, n6), F32),
        grid=(depth, n6 // tn),
        in_specs=[
            pl.BlockSpec((rows, d), lambda l, j: (0, 0)),
            pl.BlockSpec((None, d, tn), lambda l, j: (l, 0, j)),
            pl.BlockSpec((None, 1, tn), lambda l, j: (l, 0, j)),
        ],
        out_specs=pl.BlockSpec((None, rows, tn), lambda l, j: (l, 0, j)),
        compiler_params=_cparams("arbitrary", "arbitrary"),
        name="ada_mod",
    )(cond, ada_w, ada_b.reshape(depth, 1, n6))


def _qkv_kernel(x_ref, sc_ref, sh_ref, w_ref, b_ref, cos_ref, sa_ref, sb_ref, q_ref, k_ref, v_ref, *, scale):
    u = (x_ref[...] * (1.0 + sc_ref[...]) + sh_ref[...]).astype(BF16)
    p = jnp.dot(u, w_ref[...], preferred_element_type=F32) + b_ref[...]
    cos, sa, sb = cos_ref[...], sa_ref[...], sb_ref[...]
    nq = q_ref.shape[1]
    nk = k_ref.shape[1]

    def rope(xc):
        return xc * cos + pltpu.roll(xc, 16, 1) * sa + pltpu.roll(xc, LANES - 16, 1) * sb

    for c in range(nq // LANES):
        q_ref[:, c * LANES:(c + 1) * LANES] = (rope(p[:, c * LANES:(c + 1) * LANES]) * scale).astype(BF16)
    nkv = nk // 2
    for c in range(nkv // LANES):
        _store_dup_heads(k_ref, c, rope(p[:, nq + c * LANES:nq + (c + 1) * LANES]))
        _store_dup_heads(v_ref, c, p[:, nq + nkv + c * LANES:nq + nkv + (c + 1) * LANES])


def _store_dup_heads(ref, c, pair):
    lo = lax.broadcasted_iota(I32, (1, LANES), 1) < (LANES // 2)
    swapped = pltpu.roll(pair, LANES // 2, 1)
    ref[:, (2 * c) * LANES:(2 * c + 1) * LANES] = jnp.where(lo, pair, swapped).astype(ref.dtype)
    ref[:, (2 * c + 1) * LANES:(2 * c + 2) * LANES] = jnp.where(lo, swapped, pair).astype(ref.dtype)


def _qkv_call(x, sc, sh, w_ext, b_ext, cos_t, sa_t, sb_t, n_q, n_kd, head_dim):
    bsz, s, d = x.shape
    tm = min(TOKEN_TILE, s)
    n_out = w_ext.shape[1]
    kern = functools.partial(_qkv_kernel, scale=head_dim ** -0.5)
    return pl.pallas_call(
        kern,
        out_shape=(jax.ShapeDtypeStruct((bsz, s, n_q), BF16),
                   jax.ShapeDtypeStruct((bsz, s, n_kd), BF16),
                   jax.ShapeDtypeStruct((bsz, s, n_kd), BF16)),
        grid=(bsz, s // tm),
        in_specs=[
            pl.BlockSpec((None, tm, d), lambda b, i: (b, i, 0)),
            pl.BlockSpec((None, 1, d), lambda b, i: (b, 0, 0)),
            pl.BlockSpec((None, 1, d), lambda b, i: (b, 0, 0)),
            _resident((d, n_out), lambda b, i: (0, 0)),
            pl.BlockSpec((1, n_out), lambda b, i: (0, 0)),
            pl.BlockSpec((tm, LANES), lambda b, i: (i, 0)),
            pl.BlockSpec((tm, LANES), lambda b, i: (i, 0)),
            pl.BlockSpec((tm, LANES), lambda b, i: (i, 0)),
        ],
        out_specs=(pl.BlockSpec((None, tm, n_q), lambda b, i: (b, i, 0)),
                   pl.BlockSpec((None, tm, n_kd), lambda b, i: (b, i, 0)),
                   pl.BlockSpec((None, tm, n_kd), lambda b, i: (b, i, 0))),
        compiler_params=_cparams("arbitrary", "arbitrary"),
        name="attn_qkv",
    )(x, sc, sh, w_ext, b_ext, cos_t, sa_t, sb_t)


def _ctx_kv_kernel(x_ref, sc_ref, sh_ref, w_ref, b_ref, k_ref, v_ref):
    u = (x_ref[...] * (1.0 + sc_ref[...]) + sh_ref[...]).astype(BF16)
    p = jnp.dot(u, w_ref[...], preferred_element_type=F32) + b_ref[...]
    nkv = k_ref.shape[1] // 2
    for c in range(nkv // LANES):
        _store_dup_heads(k_ref, c, p[:, c * LANES:(c + 1) * LANES])
        _store_dup_heads(v_ref, c, p[:, nkv + c * LANES:nkv + (c + 1) * LANES])


def _ctx_kv_call(ctx, csc, csh, w_kv, b_kv, n_kd):
    bsz, c, d = ctx.shape
    return pl.pallas_call(
        _ctx_kv_kernel,
        out_shape=(jax.ShapeDtypeStruct((bsz, c, n_kd), BF16), jax.ShapeDtypeStruct((bsz, c, n_kd), BF16)),
        grid=(bsz,),
        in_specs=[
            pl.BlockSpec((None, c, d), lambda b: (b, 0, 0)),
            pl.BlockSpec((1, d), lambda b: (0, 0)),
            pl.BlockSpec((1, d), lambda b: (0, 0)),
            pl.BlockSpec((d, n_kd), lambda b: (0, 0)),
            pl.BlockSpec((1, n_kd), lambda b: (0, 0)),
        ],
        out_specs=(pl.BlockSpec((None, c, n_kd), lambda b: (b, 0, 0)),
                   pl.BlockSpec((None, c, n_kd), lambda b: (b, 0, 0))),
        compiler_params=_cparams("arbitrary"),
        name="attn_ctx_kv",
    )(ctx, csc, csh, w_kv, b_kv)


def _attn_kernel(sink_ref, q_ref, kp_ref, kc_ref, kn_ref, vp_ref, vc_ref, vn_ref, kx_ref, vx_ref, o_ref,
                 *, n_kv, group, nb):
    n = pl.program_id(1)
    r = lax.broadcasted_iota(I32, (BLOCK, BLOCK), 0)
    j = lax.broadcasted_iota(I32, (BLOCK, BLOCK), 1)
    prev_ok = (j >= r) & (n > 0)
    next_ok = (j <= r) & (n < nb - 1)
    lo = lax.broadcasted_iota(I32, (1, LANES), 1) < (LANES // 2)
    pairs = group // 2
    for kh in range(n_kv):
        sl = slice(kh * LANES, (kh + 1) * LANES)
        kcat = jnp.concatenate([kp_ref[:, sl], kc_ref[:, sl], kn_ref[:, sl], kx_ref[:, sl]], axis=0)
        vcat = jnp.concatenate([vp_ref[:, sl], vc_ref[:, sl], vn_ref[:, sl], vx_ref[:, sl]], axis=0)
        nkeys = kcat.shape[0]
        parts = []
        for pp in range(pairs):
            q2 = q_ref[:, (kh * pairs + pp) * LANES:(kh * pairs + pp + 1) * LANES]
            zq = jnp.zeros_like(q2)
            parts += [jnp.where(lo, q2, zq), jnp.where(lo, zq, q2)]
        qs = jnp.concatenate(parts, axis=0)
        s_all = lax.dot_general(qs, kcat, (((1,), (1,)), ((), ())), preferred_element_type=F32)
        e_parts, rdens = [], []
        for g in range(group):
            s = s_all[g * BLOCK:(g + 1) * BLOCK]
            s = jnp.concatenate([
                jnp.where(prev_ok, s[:, :BLOCK], NEG_INF),
                s[:, BLOCK:2 * BLOCK],
                jnp.where(next_ok, s[:, 2 * BLOCK:3 * BLOCK], NEG_INF),
                s[:, 3 * BLOCK:]], axis=1)
            sk = sink_ref[kh * group + g]
            m = jnp.maximum(jnp.max(s, axis=1, keepdims=True), sk)
            e = jnp.exp(s - m)
            rdens.append(1.0 / (jnp.sum(e, axis=1, keepdims=True) + jnp.exp(sk - m)))
            e_parts.append(e.astype(BF16))
        o = jnp.dot(jnp.concatenate(e_parts, axis=0), vcat, preferred_element_type=F32)
        for pp in range(pairs):
            p = kh * pairs + pp
            o_lo = o[(2 * pp) * BLOCK:(2 * pp + 1) * BLOCK] * rdens[2 * pp]
            o_hi = o[(2 * pp + 1) * BLOCK:(2 * pp + 2) * BLOCK] * rdens[2 * pp + 1]
            o_ref[:, p * LANES:(p + 1) * LANES] = jnp.where(lo, o_lo, o_hi).astype(BF16)


def _attn_call(sink, q, kd, vd, kxd, vxd, n_kv, group):
    bsz, s, n_q = q.shape
    n_kd = kd.shape[2]
    c = kxd.shape[1]
    nb = s // BLOCK
    kern = functools.partial(_attn_kernel, n_kv=n_kv, group=group, nb=nb)
    prev = lambda b, n: (b, jnp.maximum(n - 1, 0), 0)
    cur = lambda b, n: (b, n, 0)
    nxt = lambda b, n: (b, jnp.minimum(n + 1, nb - 1), 0)
    kv = lambda im: pl.BlockSpec((None, BLOCK, n_kd), im)
    return pl.pallas_call(
        kern,
        out_shape=jax.ShapeDtypeStruct((bsz, s, n_q), BF16),
        grid=(bsz, nb),
        in_specs=[
            pl.BlockSpec(memory_space=pltpu.SMEM),
            pl.BlockSpec((None, BLOCK, n_q), cur),
            kv(prev), kv(cur), kv(nxt), kv(prev), kv(cur), kv(nxt),
            pl.BlockSpec((None, c, n_kd), lambda b, n: (b, 0, 0)),
            pl.BlockSpec((None, c, n_kd), lambda b, n: (b, 0, 0)),
        ],
        out_specs=pl.BlockSpec((None, BLOCK, n_q), cur),
        compiler_params=_cparams("arbitrary", "arbitrary"),
        name="window_attn",
    )(sink, q, kd, kd, kd, vd, vd, vd, kxd, vxd)


def _layer_norm(r, g, b):
    mu = jnp.mean(r, axis=-1, keepdims=True)
    xc = r - mu
    var = jnp.mean(xc * xc, axis=-1, keepdims=True)
    return xc * lax.rsqrt(var + LN_EPS) * g + b


def _route(logits_t, rb):
    s = jax.nn.sigmoid(logits_t)
    sel = s + rb
    n_e = N_GROUPS * EXPERTS_PER_GROUP
    sel_r = [sel[e:e + 1, :] for e in range(n_e)]
    s_r = [s[e:e + 1, :] for e in range(n_e)]
    gscore = []
    for g in range(N_GROUPS):
        a, b, c, d = sel_r[4 * g:4 * g + 4]
        m1, n1, m2, n2 = jnp.maximum(a, b), jnp.minimum(a, b), jnp.maximum(c, d), jnp.minimum(c, d)
        gscore.append(jnp.maximum(m1, m2) + jnp.maximum(jnp.minimum(m1, m2), jnp.maximum(n1, n2)))
    best, gi = gscore[0], jnp.zeros_like(gscore[0], dtype=I32)
    for g in range(1, N_GROUPS):
        upd = gscore[g] > best
        gi = jnp.where(upd, g, gi)
        best = jnp.where(upd, gscore[g], best)

    def pick(rows, i):
        out = rows[i]
        for g in range(1, N_GROUPS):
            out = jnp.where(gi == g, rows[4 * g + i], out)
        return out

    v = [pick(sel_r, i) for i in range(EXPERTS_PER_GROUP)]
    sv = [pick(s_r, i) for i in range(EXPERTS_PER_GROUP)]

    def argmax4(vals):
        bv, bi = vals[0], jnp.zeros_like(gi)
        for i in range(1, EXPERTS_PER_GROUP):
            upd = vals[i] > bv
            bi = jnp.where(upd, i, bi)
            bv = jnp.where(upd, vals[i], bv)
        return bi

    def take4(vals, idx):
        out = vals[0]
        for i in range(1, EXPERTS_PER_GROUP):
            out = jnp.where(idx == i, vals[i], out)
        return out

    i1 = argmax4(v)
    i2 = argmax4([jnp.where(i1 == i, -jnp.inf, v[i]) for i in range(EXPERTS_PER_GROUP)])
    s1, s2 = take4(sv, i1), take4(sv, i2)
    tot = s1 + s2
    g1, g2 = s1 / tot, s2 / tot
    first_lo = i1 < i2
    i_lo, i_hi = jnp.minimum(i1, i2), jnp.maximum(i1, i2)
    g_lo, g_hi = jnp.where(first_lo, g1, g2), jnp.where(first_lo, g2, g1)
    pair = jnp.where(i_lo == 0, i_hi - 1, jnp.where(i_lo == 1, i_hi + 1, N_PAIRS - 1))
    bucket = gi * N_PAIRS + pair
    return 4 * gi + i_lo, 4 * gi + i_hi, g_lo, g_hi, bucket


def _proj_ln_kernel(a_ref, w_ref, bias_ref, h_ref, gate_ref, lng_ref, lnb_ref, sc_ref, sh_ref, rw_ref, rb_ref,
                    h1_ref, xp_ref, meta_ref, cnt_ref, run_ref, *, alpha, a_packed):
    first = (pl.program_id(0) == 0) & (pl.program_id(1) == 0)

    @pl.when(first)
    def _():
        run_ref[...] = jnp.zeros_like(run_ref)

    tm, d = h_ref.shape
    half = d // 2
    chunk = tm // PROJ_CHUNKS
    logit_parts = []
    ys = []
    for c in range(PROJ_CHUNKS):
        rs = slice(c * chunk, (c + 1) * chunk)
        a = _unpack_cols(a_ref[rs]).astype(BF16) if a_packed else a_ref[rs]
        ys.append(jnp.dot(a, w_ref[...], preferred_element_type=F32) + bias_ref[...])
    for c in range(PROJ_CHUNKS):
        rs = slice(c * chunk, (c + 1) * chunk)
        y = ys[c]
        h1 = _layer_norm(alpha * h_ref[rs] + gate_ref[...] * y, lng_ref[...], lnb_ref[...])
        h1_ref[rs] = h1
        tb = (h1 * (1.0 + sc_ref[...]) + sh_ref[...]).astype(BF16)
        bits = lax.bitcast_convert_type(tb.astype(F32), U32)
        xp_ref[rs, :half] = (bits[:, :half] & jnp.uint32(0xFFFF0000)) | (bits[:, half:] >> 16)
        logits = jnp.dot(tb, rw_ref[...], preferred_element_type=F32)
        logit_parts.append(jnp.transpose(logits)[:N_GROUPS * EXPERTS_PER_GROUP, :])
    logits_t = jnp.concatenate(logit_parts, axis=1)
    e_lo, e_hi, g_lo, g_hi, bucket = _route(logits_t, rb_ref[...])

    rows = lax.broadcasted_iota(I32, (BUCKET_ROWS, tm), 0)
    onehot = (rows == bucket).astype(F32)
    tri = (lax.broadcasted_iota(I32, (tm, tm), 0) <= lax.broadcasted_iota(I32, (tm, tm), 1)).astype(BF16)
    cum = jnp.dot(onehot.astype(BF16), tri, preferred_element_type=F32)
    run = run_ref[:, 0:1]
    rank = jnp.sum(onehot * (cum - 1.0 + run), axis=0, keepdims=True)
    new_run = run + cum[:, tm - 1:tm]
    run_ref[...] = jnp.broadcast_to(new_run, run_ref.shape)
    cnt_ref[...] = jnp.broadcast_to(new_run, cnt_ref.shape)

    mrow = lax.broadcasted_iota(I32, (8, tm), 0)
    meta = jnp.where(mrow == 0, e_lo.astype(F32), 0.0)
    meta = jnp.where(mrow == 1, e_hi.astype(F32), meta)
    meta = jnp.where(mrow == 2, g_lo, meta)
    meta = jnp.where(mrow == 3, g_hi, meta)
    meta = jnp.where(mrow == 4, bucket.astype(F32), meta)
    meta = jnp.where(mrow == 5, rank, meta)
    meta_ref[...] = meta

    grow = lax.broadcasted_iota(I32, (LANES, tm), 0)
    gates_t = jnp.where(grow == 0, g_lo, jnp.where(grow == 1, g_hi, 0.0))
    xp_ref[:, half:] = lax.bitcast_convert_type(jnp.transpose(gates_t), U32)


def _proj_ln_call(a, w, bias, h, gate, lng, lnb, sc, sh, rw_pad, rb, alpha):
    bsz, s, d = h.shape
    a_packed = a.dtype == U32
    da = a.shape[2]
    dm = w.shape[0]
    tm = min(TOKEN_TILE, s)
    t = bsz * s
    nt = s // tm
    dp = d // 2 + LANES
    kern = functools.partial(_proj_ln_kernel, alpha=alpha, a_packed=a_packed)
    vec = lambda: pl.BlockSpec((1, d), lambda b, i: (0, 0))
    bvec = lambda: pl.BlockSpec((None, 1, d), lambda b, i: (b, 0, 0))
    return pl.pallas_call(
        kern,
        out_shape=(jax.ShapeDtypeStruct((bsz, s, d), F32),
                   jax.ShapeDtypeStruct((t, dp), U32),
                   jax.ShapeDtypeStruct((8, t), F32),
                   jax.ShapeDtypeStruct((BUCKET_ROWS, LANES), F32)),
        grid=(bsz, nt),
        in_specs=[
            pl.BlockSpec((None, tm, da), lambda b, i: (b, i, 0)),
            _resident((dm, d), lambda b, i: (0, 0)),
            vec(),
            pl.BlockSpec((None, tm, d), lambda b, i: (b, i, 0)),
            bvec(), vec(), vec(), bvec(), bvec(),
            pl.BlockSpec((d, LANES), lambda b, i: (0, 0)),
            pl.BlockSpec((N_GROUPS * EXPERTS_PER_GROUP, 1), lambda b, i: (0, 0)),
        ],
        out_specs=(pl.BlockSpec((None, tm, d), lambda b, i: (b, i, 0)),
                   pl.BlockSpec((tm, dp), lambda b, i: (b * nt + i, 0)),
                   pl.BlockSpec((8, tm), lambda b, i: (0, b * nt + i)),
                   pl.BlockSpec((BUCKET_ROWS, LANES), lambda b, i: (0, 0))),
        scratch_shapes=[pltpu.VMEM((BUCKET_ROWS, LANES), F32)],
        compiler_params=_cparams("arbitrary", "arbitrary"),
        name="proj_ln_route",
    )(a, w, bias, h, gate, lng, lnb, sc, sh, rw_pad, rb)


def _wait_tile(src_tile, dst_tile, sem):
    pltpu.make_async_copy(src_tile, dst_tile, sem).wait()


def _scatter_kernel(dest_ref, xp_ref, init_ref, xs_ref, sem):
    del init_ref
    tm = xp_ref.shape[0]
    base = pl.program_id(0) * tm

    def issue(i, c):
        for p in range(DMA_QUEUES):
            r = i * DMA_QUEUES + p
            pltpu.make_async_copy(xp_ref.at[pl.ds(r, 1)], xs_ref.at[pl.ds(dest_ref[base + r], 1)],
                                  sem).start(priority=p)
        return c

    lax.fori_loop(0, tm // DMA_QUEUES, issue, 0, unroll=4)
    _wait_tile(xp_ref, xs_ref.at[pl.ds(0, tm)], sem)


def _scatter_call(dest, xp, rows_out):
    t, dp = xp.shape
    tm = min(TOKEN_TILE, t)
    init = jnp.zeros((rows_out, dp), U32)
    return pl.pallas_call(
        _scatter_kernel,
        out_shape=jax.ShapeDtypeStruct((rows_out, dp), U32),
        grid_spec=pltpu.PrefetchScalarGridSpec(
            num_scalar_prefetch=1,
            grid=(t // tm,),
            in_specs=[pl.BlockSpec((tm, dp), lambda i, dest: (i, 0)),
                      pl.BlockSpec(memory_space=pl.ANY)],
            out_specs=pl.BlockSpec(memory_space=pl.ANY),
            scratch_shapes=[pltpu.SemaphoreType.DMA],
        ),
        input_output_aliases={2: 0},
        compiler_params=_cparams("arbitrary"),
        name="moe_scatter",
    )(dest, xp, init)


def _moe_kernel(ex_ref, valid_ref, xs_ref, w1_ref, w3_ref, w2_ref, y_ref, acc_ref):
    del ex_ref
    i = pl.program_id(0)
    s = pl.program_id(1)
    half = xs_ref.shape[1] - LANES
    ok = valid_ref[i] > 0

    @pl.when(ok)
    def _():
        w = xs_ref[:, :half]
        hi = lax.bitcast_convert_type(w & jnp.uint32(0xFFFF0000), F32).astype(BF16)
        lo = lax.bitcast_convert_type(w << 16, F32).astype(BF16)
        x = jnp.concatenate([hi, lo], axis=1)
        gates = lax.bitcast_convert_type(xs_ref[:, half:], F32)
        which = (s + i) % 2
        g = jnp.where(which == 0, gates[:, 0:1], gates[:, 1:2])
        a = jnp.dot(x, w1_ref[...], preferred_element_type=F32)
        b = jnp.dot(x, w3_ref[...], preferred_element_type=F32)
        hid = (_silu(a) * b * g).astype(BF16)
        y = jnp.dot(hid, w2_ref[...], preferred_element_type=F32)

        @pl.when(s == 0)
        def _():
            acc_ref[...] = y

        @pl.when(s != 0)
        def _():
            y_ref[...] = _pack_cols(acc_ref[...] + y)

    @pl.when(jnp.logical_not(ok) & (s == 0))
    def _():
        y_ref[...] = jnp.zeros_like(y_ref)


def _moe_call(ex, valid, xs, w1, w3, w2, layer):
    rows, dp = xs.shape
    _, n_e, d, ff = w1.shape
    tm = min(TOKEN_TILE, rows)
    ntiles = rows // tm
    return pl.pallas_call(
        _moe_kernel,
        out_shape=jax.ShapeDtypeStruct((rows, d // 2), U32),
        grid_spec=pltpu.PrefetchScalarGridSpec(
            num_scalar_prefetch=2,
            grid=(ntiles, 2),
            in_specs=[
                pl.BlockSpec((tm, dp), lambda i, s, ex, va: (i, 0)),
                pl.BlockSpec((None, None, d, ff), lambda i, s, ex, va: (layer, ex[2 * i + s], 0, 0)),
                pl.BlockSpec((None, None, d, ff), lambda i, s, ex, va: (layer, ex[2 * i + s], 0, 0)),
                pl.BlockSpec((None, None, ff, d), lambda i, s, ex, va: (layer, ex[2 * i + s], 0, 0)),
            ],
            out_specs=pl.BlockSpec((tm, d // 2), lambda i, s, ex, va: (i, 0)),
            scratch_shapes=[pltpu.VMEM((tm, d), F32)],
        ),
        compiler_params=_cparams("arbitrary", "arbitrary"),
        name="moe_experts",
    )(ex, valid, xs, w1, w3, w2)


GATHER_CHUNKS = 8


def _gather_ln_kernel(dest_ref, ys_ref, h_ref, gate_ref, lng_ref, lnb_ref, o_ref, buf, sem, *, alpha, nt):
    tm = h_ref.shape[0]
    step = pl.program_id(0) * nt + pl.program_id(1)
    last = pl.num_programs(0) * nt - 1
    slot = step % 2

    def row_copy(tile, sl, r):
        return pltpu.make_async_copy(ys_ref.at[pl.ds(dest_ref[tile * tm + r], 1)], buf.at[sl, pl.ds(r, 1)],
                                     sem.at[sl])

    def wait_tile(sl):
        _wait_tile(ys_ref.at[pl.ds(0, tm)], buf.at[sl], sem.at[sl])

    @pl.when(step == 0)
    def _():
        def issue(i, c):
            for p in range(DMA_QUEUES):
                row_copy(0, 0, i * DMA_QUEUES + p).start(priority=p)
            return c

        lax.fori_loop(0, tm // DMA_QUEUES, issue, 0, unroll=4)

    wait_tile(slot)
    nxt = jnp.minimum(step + 1, last)
    rows = tm // GATHER_CHUNKS
    for c in range(GATHER_CHUNKS):
        for r in range(c * rows, (c + 1) * rows):
            row_copy(nxt, 1 - slot, r).start(priority=r % DMA_QUEUES)
        sl = slice(c * rows, (c + 1) * rows)
        f = _unpack_cols(buf[slot, sl])
        o_ref[sl] = _layer_norm(alpha * h_ref[sl] + gate_ref[...] * f, lng_ref[...], lnb_ref[...])

    @pl.when(step == last)
    def _():
        wait_tile(1 - slot)


def _gather_ln_call(dest, ys, h, gate, lng, lnb, alpha):
    bsz, s, d = h.shape
    tm = min(TOKEN_TILE, s)
    nt = s // tm
    kern = functools.partial(_gather_ln_kernel, alpha=alpha, nt=nt)
    return pl.pallas_call(
        kern,
        out_shape=jax.ShapeDtypeStruct((bsz, s, d), F32),
        grid_spec=pltpu.PrefetchScalarGridSpec(
            num_scalar_prefetch=1,
            grid=(bsz, nt),
            in_specs=[
                pl.BlockSpec(memory_space=pl.ANY),
                pl.BlockSpec((None, tm, d), lambda b, i, dest: (b, i, 0)),
                pl.BlockSpec((None, 1, d), lambda b, i, dest: (b, 0, 0)),
                pl.BlockSpec((1, d), lambda b, i, dest: (0, 0)),
                pl.BlockSpec((1, d), lambda b, i, dest: (0, 0)),
            ],
            out_specs=pl.BlockSpec((None, tm, d), lambda b, i, dest: (b, i, 0)),
            scratch_shapes=[pltpu.VMEM((2, tm, d // 2), U32), pltpu.SemaphoreType.DMA((2,))],
        ),
        compiler_params=_cparams("arbitrary", "arbitrary"),
        name="moe_gather_ln",
    )(dest, ys, h, gate, lng, lnb)


_PAIR_LO = (0, 0, 0, 1, 1, 2)
_PAIR_HI = (1, 2, 3, 2, 3, 3)


def _moe_layer(xp, meta, cnt, h1, gate2, lng, lnb, w1, w3, w2, layer, alpha):
    t = xp.shape[0]
    tm = min(TOKEN_TILE, t)
    ntiles = t // tm + N_BUCKETS
    counts = cnt[:N_BUCKETS, 0].astype(I32)
    tiles_b = (counts + tm - 1) // tm
    tile_end = jnp.cumsum(tiles_b)
    offs = (tile_end - tiles_b) * tm
    bucket = meta[4].astype(I32)
    dest = offs[bucket] + meta[5].astype(I32)

    tile = jnp.arange(ntiles, dtype=I32)
    valid = (tile < tile_end[-1]).astype(I32)
    tb = jnp.minimum(jnp.sum((tile[:, None] >= tile_end[None, :]).astype(I32), axis=1), N_BUCKETS - 1)
    lo = jnp.asarray(_PAIR_LO, I32)[tb % N_PAIRS] + EXPERTS_PER_GROUP * (tb // N_PAIRS)
    hi = jnp.asarray(_PAIR_HI, I32)[tb % N_PAIRS] + EXPERTS_PER_GROUP * (tb // N_PAIRS)
    odd = (tile % 2) == 1
    ex = jnp.stack([jnp.where(odd, hi, lo), jnp.where(odd, lo, hi)], axis=1).reshape(-1)

    xs = _scatter_call(dest, xp, ntiles * tm)
    ys = _moe_call(ex, valid, xs, w1, w3, w2, layer)
    return _gather_ln_call(dest, ys, h1, gate2, lng, lnb, alpha)


def _hy_in_kernel(xm_ref, xp_ref, xn_ref, sc_ref, sh_ref, w_ref, b_ref, cw_ref, cb_ref, x0_ref, z_ref, *, nt, tn):
    i = pl.program_id(1)
    tm, d = xm_ref.shape
    sc, sh = 1.0 + sc_ref[...], sh_ref[...]
    u = jnp.concatenate([xp_ref[...] * sc + sh, xm_ref[...] * sc + sh, xn_ref[...] * sc + sh], axis=0).astype(BF16)
    rows = tm + 16
    rid = lax.broadcasted_iota(I32, (rows, 1), 0)
    keep = ((rid >= 8) | (i > 0)) & ((rid < tm + 8) | (i < nt - 1))

    def conv(sec, j):
        col = sec * d + j * tn
        p = jnp.dot(u, w_ref[:, col:col + tn], preferred_element_type=F32) + b_ref[:, col:col + tn]
        p = jnp.where(keep, p, 0.0)
        cw = cw_ref[:, col:col + tn]
        out = (cw[0:1] * pltpu.roll(p, 1, 0) + cw[1:2] * p + cw[2:3] * pltpu.roll(p, rows - 1, 0)
               + cb_ref[:, col:col + tn])
        return out[8:8 + tm]

    nch = d // tn
    for j in range(nch // 2):
        x0_ref[:, j * tn:(j + 1) * tn] = _pack_pair(conv(0, j), conv(0, j + nch // 2))
        z_ref[:, j * tn:(j + 1) * tn] = _pack_pair(conv(1, j) * conv(2, j),
                                                   conv(1, j + nch // 2) * conv(2, j + nch // 2))


def _hy_in_call(h, sc, sh, w, b, cw, cb):
    bsz, s, d = h.shape
    tm = min(TOKEN_TILE, s)
    nt = s // tm
    hb = tm // 8
    tn = 1024
    kern = functools.partial(_hy_in_kernel, nt=nt, tn=tn)
    return pl.pallas_call(
        kern,
        out_shape=(jax.ShapeDtypeStruct((bsz, s, d // 2), U32), jax.ShapeDtypeStruct((bsz, s, d // 2), U32)),
        grid=(bsz, nt),
        in_specs=[
            pl.BlockSpec((None, tm, d), lambda b, i: (b, i, 0)),
            pl.BlockSpec((None, 8, d), lambda b, i: (b, jnp.maximum(i * hb - 1, 0), 0)),
            pl.BlockSpec((None, 8, d), lambda b, i: (b, jnp.minimum((i + 1) * hb, nt * hb - 1), 0)),
            pl.BlockSpec((None, 1, d), lambda b, i: (b, 0, 0)),
            pl.BlockSpec((None, 1, d), lambda b, i: (b, 0, 0)),
            _resident((d, 3 * d), lambda b, i: (0, 0)),
            pl.BlockSpec((1, 3 * d), lambda b, i: (0, 0)),
            pl.BlockSpec((3, 3 * d), lambda b, i: (0, 0)),
            pl.BlockSpec((1, 3 * d), lambda b, i: (0, 0)),
        ],
        out_specs=(pl.BlockSpec((None, tm, d // 2), lambda b, i: (b, i, 0)),
                   pl.BlockSpec((None, tm, d // 2), lambda b, i: (b, i, 0))),
        compiler_params=_cparams("arbitrary", "arbitrary"),
        name="hyena_in",
    )(h, h, h, sc, sh, w, b, cw, cb)


def _filter_kernel(w1_ref, b1_ref, w2_ref, b2_ref, w3_ref, b3_ref, fr_ref, w4_ref, dl_ref, f_ref, l1_ref,
                   *, seq, tn):
    i = pl.program_id(0)

    @pl.when(i == 0)
    def _():
        l1_ref[...] = jnp.zeros_like(l1_ref)

    n_lane = i * tn + lax.broadcasted_iota(I32, (1, tn), 1)
    m_lane = jnp.where(n_lane < seq, n_lane, 2 * seq - n_lane).astype(F32)
    t_lane = m_lane / (seq - 1.0)
    wl = (2.0 * math.pi) * m_lane / float(seq)
    band = lax.broadcasted_iota(I32, (HY_BANDS, 1), 0).astype(F32)
    fb = 1e-4 + band * ((HY_BANDS - 1 - 1e-4) / (HY_BANDS - 1))
    ang = fb * wl
    z = jnp.concatenate([t_lane, jnp.cos(ang), -jnp.sin(ang),
                         jnp.zeros((7, tn), F32)], axis=0)
    fr = fr_ref[...]

    def layer(w_ref, b_ref, x):
        pre = lax.dot_general(w_ref[...], x, (((0,), (0,)), ((), ())), preferred_element_type=F32,
                              precision=HIGHEST)
        return jnp.sin(fr * (pre + b_ref[...]))

    hdn = layer(w1_ref, b1_ref, z)
    hdn = layer(w2_ref, b2_ref, hdn)
    hdn = layer(w3_ref, b3_ref, hdn)
    h = lax.dot_general(hdn.astype(BF16), w4_ref[...].astype(BF16), (((0,), (0,)), ((), ())),
                        preferred_element_type=F32)
    n_col = i * tn + lax.broadcasted_iota(I32, (tn, 1), 0)
    m_col = jnp.where(n_col < seq, n_col, 2 * seq - n_col).astype(F32)
    h = h * jnp.exp(-(m_col / (seq - 1.0)) * dl_ref[...])
    h = jnp.where(n_col == seq, 0.0, h)
    l1_ref[...] += jnp.sum(jnp.abs(h), axis=0, keepdims=True)
    f_ref[...] = _pack_cols(h)


def _filter_call(fw1, fb1, fw2, fb2, fw3, fb3, freq, fw4, seq):
    width = fw2.shape[0]
    d = fw4.shape[1] // 2
    tn = min(512, seq)
    steps = 2 * seq // tn
    max_decay = math.log(HY_TARGET) / HY_FAST_DECAY
    min_decay = math.log(HY_TARGET) / HY_SLOW_DECAY
    deltas = jnp.abs(jnp.linspace(min_decay, max_decay, d, dtype=F32)).reshape(1, d)
    w1p = jnp.concatenate([fw1, jnp.zeros((7, width), F32)], axis=0)
    col = lambda a: a.reshape(width, 1)
    small = lambda shp: pl.BlockSpec(shp, lambda i: (0, 0))
    kern = functools.partial(_filter_kernel, seq=seq, tn=tn)
    return pl.pallas_call(
        kern,
        out_shape=(jax.ShapeDtypeStruct((2 * seq, d // 2), U32), jax.ShapeDtypeStruct((1, d), F32)),
        grid=(steps,),
        in_specs=[small((40, width)), small((width, 1)), small((width, width)), small((width, 1)),
                  small((width, width)), small((width, 1)), small((width, 1)),
                  pl.BlockSpec((width, d), lambda i: (0, (i * tn) // seq)),
                  small((1, d))],
        out_specs=(pl.BlockSpec((tn, d // 2), lambda i: (i, 0)), small((1, d))),
        compiler_params=_cparams("arbitrary"),
        name="hyena_filter",
    )(w1p, col(fb1), fw2, col(fb2), fw3, col(fb3), col(freq), fw4, deltas)


def _dft_tables(n1, n2):
    n = n1 * n2
    k = np.arange(n1)[:, None]
    m = np.arange(n1)[None, :]
    ang1 = -2.0 * np.pi * ((k * m) % n1) / n1
    f1r, f1i = np.cos(ang1), np.sin(ang1)
    hn = n1 // 2
    a_data = np.block([[f1r[:, :hn], -f1i[:, :hn]], [f1i[:, :hn], f1r[:, :hn]]])
    a_filt = np.concatenate([f1r, f1i], axis=0)
    a_inv = np.block([[f1r.T[:hn], f1i.T[:hn]], [-f1i.T[:hn], f1r.T[:hn]]]) / n
    k2 = np.arange(n2)[:, None]
    m2 = np.arange(n2)[None, :]
    ang2 = -2.0 * np.pi * ((k2 * m2) % n2) / n2
    f2r, f2i = np.cos(ang2), np.sin(ang2)
    b_fwd = np.block([[f2r, -f2i], [f2i, f2r]])
    b_inv = np.block([[f2r, f2i], [-f2i, f2r]])
    angt = -2.0 * np.pi * ((np.arange(n2)[:, None] * np.arange(n1)[None, :]) % n) / n
    tw = (np.cos(angt), np.sin(angt))
    perm = (np.arange(n1 // FFT_K1_GROUP)[None, :] * FFT_K1_GROUP + np.arange(FFT_K1_GROUP)[:, None]).reshape(-1)
    perm2 = np.concatenate([perm, perm + n1])
    as_bf16 = lambda a: jnp.asarray(a, F32).astype(BF16)
    return dict(a_data=as_bf16(a_data[perm2]), a_filt=as_bf16(a_filt[perm2]), a_inv=as_bf16(a_inv[:, perm2]),
                b_fwd=as_bf16(b_fwd), b_inv=as_bf16(b_inv),
                tw_n2=tuple(jnp.asarray(t[:, perm], F32).reshape(n2, n1, 1) for t in tw),
                tw_k1=tuple(jnp.asarray(t.T.copy(), F32).reshape(n1, n2, 1) for t in tw))


FFT_K1_GROUP = 4


def _pack_pair(hi, lo):
    hb = lax.bitcast_convert_type(hi.astype(BF16).astype(F32), U32)
    lb = lax.bitcast_convert_type(lo.astype(BF16).astype(F32), U32)
    return (hb & jnp.uint32(0xFFFF0000)) | (lb >> 16)


def _unpack_pair(w):
    hi = lax.bitcast_convert_type(w & jnp.uint32(0xFFFF0000), F32)
    lo = lax.bitcast_convert_type(w << 16, F32)
    return hi, lo


def _fetch_step(view, buf, sem):
    k = pl.program_id(0)
    slot = k % 2

    def copy(step, sl):
        return pltpu.make_async_copy(view(step), buf.at[sl], sem.at[sl])

    @pl.when(k == 0)
    def _():
        copy(0, 0).start()

    @pl.when(k + 1 < pl.num_programs(0))
    def _():
        copy(k + 1, 1 - slot).start()

    copy(k, slot).wait()
    return buf.at[slot]


def _store_step(view, buf, sem, fill):
    k = pl.program_id(0)
    slot = k % 2

    def copy(step, sl):
        return pltpu.make_async_copy(buf.at[sl], view(step), sem.at[sl])

    @pl.when(k >= 2)
    def _():
        copy(k - 2, slot).wait()

    fill(buf.at[slot])
    copy(k, slot).start()

    @pl.when(k == pl.num_programs(0) - 1)
    def _():
        copy(k, slot).wait()

        @pl.when(k >= 1)
        def _():
            copy(k - 1, 1 - slot).wait()


def _unpack_cols(w):
    hi, lo = _unpack_pair(w)
    return jnp.concatenate([hi, lo], axis=1)


def _pack_cols(x):
    half = x.shape[1] // 2
    return _pack_pair(x[:, :half], x[:, half:])


def _fft_a_kernel(x_hbm, a_ref, twr_ref, twi_ref, y_ref, xbuf, sem):
    n1 = a_ref.shape[0] // 2
    x_ref = _fetch_step(lambda j: x_hbm.at[:, :, j, :], xbuf, sem)
    rhs = jnp.concatenate([_unpack_cols(x_ref[0]), _unpack_cols(x_ref[1])], axis=0).astype(BF16)
    y = jnp.dot(a_ref[...], rhs, preferred_element_type=F32)
    yr, yi = y[:n1], y[n1:]
    tr, ti = twr_ref[...], twi_ref[...]
    w = _pack_pair(yr * tr - yi * ti, yr * ti + yi * tr)
    q = n1 // FFT_K1_GROUP
    d = w.shape[1]
    for kk in range(FFT_K1_GROUP):
        y_ref[:, kk * d:(kk + 1) * d] = w[kk * q:(kk + 1) * q]


def _fft_a_call(x4, a_mat, tw_n2):
    _, hn, n2, half = x4.shape
    d = 2 * half
    n1 = 2 * hn
    q = n1 // FFT_K1_GROUP
    return pl.pallas_call(
        _fft_a_kernel,
        out_shape=jax.ShapeDtypeStruct((n2, q, FFT_K1_GROUP * d), U32),
        grid=(n2,),
        in_specs=[
            pl.BlockSpec(memory_space=pl.ANY),
            pl.BlockSpec((2 * n1, n1), lambda j: (0, 0)),
            pl.BlockSpec((None, n1, 1), lambda j: (j, 0, 0)),
            pl.BlockSpec((None, n1, 1), lambda j: (j, 0, 0)),
        ],
        out_specs=pl.BlockSpec((None, q, FFT_K1_GROUP * d), lambda j: (j, 0, 0)),
        scratch_shapes=[pltpu.VMEM((2, 2, hn, half), U32), pltpu.SemaphoreType.DMA((2,))],
        compiler_params=_cparams("arbitrary"),
        name="fft_stage_a",
    )(x4, a_mat, tw_n2[0], tw_n2[1])


def _load_k1(y_ref, kk):
    d = y_ref.shape[1] // FFT_K1_GROUP
    yr, yi = _unpack_pair(y_ref[:, kk * d:(kk + 1) * d])
    return jnp.concatenate([yr, yi], axis=0).astype(BF16)


def _fft_b_kernel(y_hbm, b_ref, h_ref, ybuf, sem):
    y_ref = _fetch_step(lambda kb: y_hbm.at[:, kb, :], ybuf, sem)
    n2 = y_ref.shape[0]
    for kk in range(FFT_K1_GROUP):
        x = jnp.dot(b_ref[...], _load_k1(y_ref, kk), preferred_element_type=F32)
        h_ref[kk] = _pack_pair(x[:n2], x[n2:])


def _fft_b_call(y, b_fwd):
    n2, q, gd = y.shape
    d = gd // FFT_K1_GROUP
    return pl.pallas_call(
        _fft_b_kernel,
        out_shape=jax.ShapeDtypeStruct((q * FFT_K1_GROUP, n2, d), U32),
        grid=(q,),
        in_specs=[pl.BlockSpec(memory_space=pl.ANY),
                  pl.BlockSpec((2 * n2, 2 * n2), lambda k: (0, 0))],
        out_specs=pl.BlockSpec((FFT_K1_GROUP, n2, d), lambda k: (k, 0, 0)),
        scratch_shapes=[pltpu.VMEM((2, n2, gd), U32), pltpu.SemaphoreType.DMA((2,))],
        compiler_params=_cparams("arbitrary"),
        name="fft_filter_b",
    )(y, b_fwd)


def _fft_bc_kernel(y_hbm, h_ref, bf_ref, bi_ref, twr_ref, twi_ref, g_hbm, ybuf, gbuf, sem_in, sem_out):
    y_ref = _fetch_step(lambda kb: y_hbm.at[:, kb, :], ybuf, sem_in)
    n2 = y_ref.shape[0]
    d = y_ref.shape[1] // FFT_K1_GROUP

    def fill(g_ref):
        for kk in range(FFT_K1_GROUP):
            x = jnp.dot(bf_ref[...], _load_k1(y_ref, kk), preferred_element_type=F32)
            xr, xi = x[:n2], x[n2:]
            hr, hi = _unpack_pair(h_ref[kk])
            z = jnp.concatenate([xr * hr - xi * hi, xr * hi + xi * hr], axis=0).astype(BF16)
            g = jnp.dot(bi_ref[...], z, preferred_element_type=F32)
            gr, gi = g[:n2], g[n2:]
            tr, ti = twr_ref[kk], twi_ref[kk]
            g_ref[:, kk * d:(kk + 1) * d] = _pack_pair(gr * tr + gi * ti, gi * tr - gr * ti)

    _store_step(lambda kb: g_hbm.at[:, kb, :], gbuf, sem_out, fill)


def _fft_bc_call(y, hspec, b_fwd, b_inv, tw_k1):
    n2, q, gd = y.shape
    d = gd // FFT_K1_GROUP
    mat = lambda: pl.BlockSpec((2 * n2, 2 * n2), lambda k: (0, 0))
    tw = lambda: pl.BlockSpec((FFT_K1_GROUP, n2, 1), lambda k: (k, 0, 0))
    return pl.pallas_call(
        _fft_bc_kernel,
        out_shape=jax.ShapeDtypeStruct((n2, q, gd), U32),
        grid=(q,),
        in_specs=[pl.BlockSpec(memory_space=pl.ANY),
                  pl.BlockSpec((FFT_K1_GROUP, n2, d), lambda k: (k, 0, 0)), mat(), mat(), tw(), tw()],
        out_specs=pl.BlockSpec(memory_space=pl.ANY),
        scratch_shapes=[pltpu.VMEM((2, n2, gd), U32), pltpu.VMEM((2, n2, gd), U32),
                        pltpu.SemaphoreType.DMA((2,)), pltpu.SemaphoreType.DMA((2,))],
        compiler_params=_cparams("arbitrary"),
        name="fft_stage_bc",
    )(y, hspec, b_fwd, b_inv, tw_k1[0], tw_k1[1])


def _fft_d_kernel(g_ref, a_ref, x0_hbm, z_hbm, l1_ref, fb_ref, o_hbm, x0buf, zbuf, obuf, sem_x, sem_z, sem_o):
    hn = a_ref.shape[0] // 2
    d = g_ref.shape[1] // FFT_K1_GROUP
    x0_ref = _fetch_step(lambda j: x0_hbm.at[:, :, j, :], x0buf, sem_x)
    z_ref = _fetch_step(lambda j: z_hbm.at[:, :, j, :], zbuf, sem_z)
    parts = [_unpack_pair(g_ref[:, kk * d:(kk + 1) * d]) for kk in range(FFT_K1_GROUP)]
    rhs = jnp.concatenate([p[0] for p in parts] + [p[1] for p in parts], axis=0).astype(BF16)
    y = jnp.dot(a_ref[...], rhs, preferred_element_type=F32)
    inv_l1 = 1.0 / l1_ref[...]
    fb = fb_ref[...]

    def fill(o_ref):
        for b in range(2):
            conv = y[b * hn:(b + 1) * hn] * inv_l1
            o_ref[b] = _pack_cols(_unpack_cols(x0_ref[b]) * (conv + fb * _unpack_cols(z_ref[b])))

    _store_step(lambda j: o_hbm.at[:, :, j, :], obuf, sem_o, fill)


def _fft_d_call(g, a_inv, x0_4, z_4, l1, fbias):
    n2, q, gd = g.shape
    d = gd // FFT_K1_GROUP
    n1 = q * FFT_K1_GROUP
    hn = n1 // 2
    half = d // 2
    tok_buf = lambda: pltpu.VMEM((2, 2, hn, half), U32)
    return pl.pallas_call(
        _fft_d_kernel,
        out_shape=jax.ShapeDtypeStruct((2, hn, n2, half), U32),
        grid=(n2,),
        in_specs=[pl.BlockSpec((None, q, gd), lambda j: (j, 0, 0)),
                  pl.BlockSpec((n1, 2 * n1), lambda j: (0, 0)),
                  pl.BlockSpec(memory_space=pl.ANY), pl.BlockSpec(memory_space=pl.ANY),
                  pl.BlockSpec((1, d), lambda j: (0, 0)),
                  pl.BlockSpec((1, d), lambda j: (0, 0))],
        out_specs=pl.BlockSpec(memory_space=pl.ANY),
        scratch_shapes=[tok_buf(), tok_buf(), tok_buf(),
                        pltpu.SemaphoreType.DMA((2,)), pltpu.SemaphoreType.DMA((2,)), pltpu.SemaphoreType.DMA((2,))],
        compiler_params=_cparams("arbitrary"),
        name="fft_stage_d",
    )(g, a_inv, x0_4, z_4, l1, fbias)


def _hyena_conv(x0p, zp, filtp, l1, fbias):
    bsz, seq, half = zp.shape
    assert bsz == 2, "the two batch rows are packed as one complex signal"
    n2 = FFT_N2
    n1 = 2 * seq // n2
    tabs = _dft_tables(n1, n2)
    hn = n1 // 2
    view = lambda a: a.reshape(2, hn, n2, half)
    hspec = _fft_b_call(_fft_a_call(view(filtp), tabs["a_filt"], tabs["tw_n2"]), tabs["b_fwd"])
    y = _fft_a_call(view(zp), tabs["a_data"], tabs["tw_n2"])
    g = _fft_bc_call(y, hspec, tabs["b_fwd"], tabs["b_inv"], tabs["tw_k1"])
    out = _fft_d_call(g, tabs["a_inv"], view(x0p), view(zp), l1, fbias)
    return out.reshape(bsz, seq, half)


def _rope_tables(seq, head_dim):
    axis_dim = head_dim // 2
    rows = seq // GRID_W
    inv = ROPE_BASE ** (-jnp.arange(0, axis_dim, 2, dtype=F32) / axis_dim)
    row = jnp.repeat(jnp.arange(rows, dtype=F32), GRID_W)[:, None] * inv
    col = jnp.tile(jnp.arange(GRID_W, dtype=F32), rows)[:, None] * inv
    quarter = axis_dim // 2
    cos = jnp.concatenate([jnp.cos(row), jnp.cos(row), jnp.cos(col), jnp.cos(col)], axis=1)
    sin = jnp.concatenate([jnp.sin(row), jnp.sin(row), jnp.sin(col), jnp.sin(col)], axis=1)
    second = (np.arange(head_dim) % axis_dim) >= quarter
    reps = LANES // head_dim
    cos = jnp.tile(cos, (1, reps))
    sin = jnp.tile(sin, (1, reps))
    second = jnp.asarray(np.tile(second, reps))[None, :]
    return cos, jnp.where(second, sin, 0.0), jnp.where(second, 0.0, -sin)


def kernel(x, c, ctx, c_ctx, ada_w, ada_b, attn_w_in, attn_b_in, attn_sink, attn_w_out, hy_w_in, hy_b_in, hy_conv_w, hy_conv_b, hy_f_w1, hy_f_b1, hy_f_w2, hy_f_b2, hy_f_w3, hy_f_b3, hy_f_freq, hy_f_w4, hy_f_bias, hy_w_out, hy_b_out, ln1_g, ln1_b, ln2_g, ln2_b, router_w, router_b, moe_w1, moe_w3, moe_w2):
    bsz, seq, d = x.shape
    depth = ada_w.shape[0]
    assert depth == 2 and attn_w_in.shape[0] == 1 and hy_w_in.shape[0] == 1
    alpha = (2 * depth) ** 0.25
    n_heads = attn_sink.shape[1]
    attn_dim = attn_w_out.shape[1]
    head_dim = attn_dim // n_heads
    kv_dim = (attn_w_in.shape[2] - attn_dim) // 2
    n_kv = kv_dim // head_dim
    group = n_heads // n_kv
    assert head_dim * 2 == LANES and group % 2 == 0
    n_exp = router_w.shape[1]
    assert n_exp == N_GROUPS * EXPERTS_PER_GROUP

    cond = jnp.concatenate([c, c_ctx[None, :], jnp.zeros((8 - bsz - 1, d), F32)], axis=0)
    mods = _ada_call(cond, ada_w, ada_b).reshape(depth, 8, 6, d)
    mod = lambda layer, k: mods[layer, :bsz, k].reshape(bsz, 1, d)
    cmod = lambda layer, k: mods[layer, bsz, k].reshape(1, d)
    row = lambda v: v.reshape(1, -1)

    rw_pad = jnp.concatenate([router_w, jnp.zeros((d, LANES - n_exp), F32)], axis=1).astype(BF16)
    rb = router_b.reshape(n_exp, 1)
    w1b, w3b, w2b = moe_w1.astype(BF16), moe_w3.astype(BF16), moe_w2.astype(BF16)

    w_ext = attn_w_in[0].astype(BF16)
    b_ext = row(attn_b_in[0])
    w_kv, b_kv = w_ext[:, attn_dim:], b_ext[:, attn_dim:]
    n_kd = 2 * kv_dim
    cos_t, sa_t, sb_t = _rope_tables(seq, head_dim)

    q, kd, vd = _qkv_call(x, mod(0, 1), mod(0, 0), w_ext, b_ext, cos_t, sa_t, sb_t, attn_dim, n_kd, head_dim)
    kxd, vxd = _ctx_kv_call(ctx, cmod(0, 1), cmod(0, 0), w_kv, b_kv, n_kd)
    att = _attn_call(attn_sink[0], q, kd, vd, kxd, vxd, n_kv, group)
    h1, xp, meta, cnt = _proj_ln_call(att, attn_w_out[0].astype(BF16), jnp.zeros((1, d), F32), x, mod(0, 2),
                                      row(ln1_g[0]), row(ln1_b[0]), mod(0, 4), mod(0, 3), rw_pad, rb, alpha)
    h = _moe_layer(xp, meta, cnt, h1, mod(0, 5), row(ln2_g[0]), row(ln2_b[0]), w1b, w3b, w2b, 0, alpha)

    x0, z = _hy_in_call(h, mod(1, 1), mod(1, 0), hy_w_in[0].astype(BF16), row(hy_b_in[0]), hy_conv_w[0],
                        row(hy_conv_b[0]))
    filt, l1 = _filter_call(hy_f_w1[0], hy_f_b1[0], hy_f_w2[0], hy_f_b2[0], hy_f_w3[0], hy_f_b3[0],
                            hy_f_freq[0], hy_f_w4[0], seq)
    yh = _hyena_conv(x0, z, filt, l1, row(hy_f_bias[0]))
    h1, xp, meta, cnt = _proj_ln_call(yh, hy_w_out[0].astype(BF16), row(hy_b_out[0]), h, mod(1, 2),
                                      row(ln1_g[1]), row(ln1_b[1]), mod(1, 4), mod(1, 3), rw_pad, rb, alpha)
    return _moe_layer(xp, meta, cnt, h1, mod(1, 5), row(ln2_g[1]), row(ln2_b[1]), w1b, w3b, w2b, 1, alpha)
```

```python
import functools
import math

import numpy as np
import jax
import jax.numpy as jnp
from jax import lax
from jax.experimental import pallas as pl
from jax.experimental.pallas import tpu as pltpu

F32 = jnp.float32
BF16 = jnp.bfloat16
I32 = jnp.int32
U32 = jnp.uint32
HIGHEST = lax.Precision.HIGHEST

LANES = 128
V7X_VMEM_LIMIT_BYTES = 56 * 1024 * 1024

GRID_W = 64
BLOCK = 128
ROPE_BASE = 10000.0
NEG_INF = -1e30
HY_BANDS = 16
HY_FAST_DECAY = 0.3
HY_SLOW_DECAY = 1.5
HY_TARGET = 1e-2
N_GROUPS = 4
EXPERTS_PER_GROUP = 4
N_PAIRS = 6
N_BUCKETS = N_GROUPS * N_PAIRS
BUCKET_ROWS = 32
LN_EPS = 1e-5
FFT_N2 = 128

TOKEN_TILE = 512
PROJ_CHUNKS = 4
DMA_QUEUES = 2


def _cparams(*sem):
    return pltpu.CompilerParams(dimension_semantics=sem, vmem_limit_bytes=V7X_VMEM_LIMIT_BYTES)


def _resident(block_shape, index_map):
    return pl.BlockSpec(block_shape, index_map, pipeline_mode=pl.Buffered(1))


def _silu(x):
    return x * jax.nn.sigmoid(x)


def _ada_kernel(c_ref, w_ref, b_ref, o_ref):
    c = _silu(c_ref[...])
    o_ref[...] = jnp.dot(c, w_ref[...], preferred_element_type=F32, precision=HIGHEST) + b_ref[...]


def _ada_call(cond, ada_w, ada_b):
    depth, d, n6 = ada_w.shape
    tn = 1024
    rows = cond.shape[0]
    return pl.pallas_call(
        _ada_kernel,
        out_shape=jax.ShapeDtypeStruct((depth, rows, n6), F32),
        grid=(depth, n6 // tn),
        in_specs=[
            pl.BlockSpec((rows, d), lambda l, j: (0, 0)),
            pl.BlockSpec((None, d, tn), lambda l, j: (l, 0, j)),
            pl.BlockSpec((None, 1, tn), lambda l, j: (l, 0, j)),
        ],
        out_specs=pl.BlockSpec((None, rows, tn), lambda l, j: (l, 0, j)),
        compiler_params=_cparams("arbitrary", "arbitrary"),
        name="ada_mod",
    )(cond, ada_w, ada_b.reshape(depth, 1, n6))


def _qkv_kernel(x_ref, sc_ref, sh_ref, w_ref, b_ref, cos_ref, sa_ref, sb_ref, q_ref, k_ref, v_ref, *, scale):
    u = (x_ref[...] * (1.0 + sc_ref[...]) + sh_ref[...]).astype(BF16)
    p = jnp.dot(u, w_ref[...], preferred_element_type=F32) + b_ref[...]
    cos, sa, sb = cos_ref[...], sa_ref[...], sb_ref[...]
    nq = q_ref.shape[1]
    nk = k_ref.shape[1]

    def rope(xc):
        return xc * cos + pltpu.roll(xc, 16, 1) * sa + pltpu.roll(xc, LANES - 16, 1) * sb

    for c in range(nq // LANES):
        q_ref[:, c * LANES:(c + 1) * LANES] = (rope(p[:, c * LANES:(c + 1) * LANES]) * scale).astype(BF16)
    nkv = nk // 2
    for c in range(nkv // LANES):
        _store_dup_heads(k_ref, c, rope(p[:, nq + c * LANES:nq + (c + 1) * LANES]))
        _store_dup_heads(v_ref, c, p[:, nq + nkv + c * LANES:nq + nkv + (c + 1) * LANES])


def _store_dup_heads(ref, c, pair):
    lo = lax.broadcasted_iota(I32, (1, LANES), 1) < (LANES // 2)
    swapped = pltpu.roll(pair, LANES // 2, 1)
    ref[:, (2 * c) * LANES:(2 * c + 1) * LANES] = jnp.where(lo, pair, swapped).astype(ref.dtype)
    ref[:, (2 * c + 1) * LANES:(2 * c + 2) * LANES] = jnp.where(lo, swapped, pair).astype(ref.dtype)


def _qkv_call(x, sc, sh, w_ext, b_ext, cos_t, sa_t, sb_t, n_q, n_kd, head_dim):
    bsz, s, d = x.shape
    tm = min(TOKEN_TILE, s)
    n_out = w_ext.shape[1]
    kern = functools.partial(_qkv_kernel, scale=head_dim ** -0.5)
    return pl.pallas_call(
        kern,
        out_shape=(jax.ShapeDtypeStruct((bsz, s, n_q), BF16),
                   jax.ShapeDtypeStruct((bsz, s, n_kd), BF16),
                   jax.ShapeDtypeStruct((bsz, s, n_kd), BF16)),
        grid=(bsz, s // tm),
        in_specs=[
            pl.BlockSpec((None, tm, d), lambda b, i: (b, i, 0)),
            pl.BlockSpec((None, 1, d), lambda b, i: (b, 0, 0)),
            pl.BlockSpec((None, 1, d), lambda b, i: (b, 0, 0)),
            _resident((d, n_out), lambda b, i: (0, 0)),
            pl.BlockSpec((1, n_out), lambda b, i: (0, 0)),
            pl.BlockSpec((tm, LANES), lambda b, i: (i, 0)),
            pl.BlockSpec((tm, LANES), lambda b, i: (i, 0)),
            pl.BlockSpec((tm, LANES), lambda b, i: (i, 0)),
        ],
        out_specs=(pl.BlockSpec((None, tm, n_q), lambda b, i: (b, i, 0)),
                   pl.BlockSpec((None, tm, n_kd), lambda b, i: (b, i, 0)),
                   pl.BlockSpec((None, tm, n_kd), lambda b, i: (b, i, 0))),
        compiler_params=_cparams("arbitrary", "arbitrary"),
        name="attn_qkv",
    )(x, sc, sh, w_ext, b_ext, cos_t, sa_t, sb_t)


def _ctx_kv_kernel(x_ref, sc_ref, sh_ref, w_ref, b_ref, k_ref, v_ref):
    u = (x_ref[...] * (1.0 + sc_ref[...]) + sh_ref[...]).astype(BF16)
    p = jnp.dot(u, w_ref[...], preferred_element_type=F32) + b_ref[...]
    nkv = k_ref.shape[1] // 2
    for c in range(nkv // LANES):
        _store_dup_heads(k_ref, c, p[:, c * LANES:(c + 1) * LANES])
        _store_dup_heads(v_ref, c, p[:, nkv + c * LANES:nkv + (c + 1) * LANES])


def _ctx_kv_call(ctx, csc, csh, w_kv, b_kv, n_kd):
    bsz, c, d = ctx.shape
    return pl.pallas_call(
        _ctx_kv_kernel,
        out_shape=(jax.ShapeDtypeStruct((bsz, c, n_kd), BF16), jax.ShapeDtypeStruct((bsz, c, n_kd), BF16)),
        grid=(bsz,),
        in_specs=[
            pl.BlockSpec((None, c, d), lambda b: (b, 0, 0)),
            pl.BlockSpec((1, d), lambda b: (0, 0)),
            pl.BlockSpec((1, d), lambda b: (0, 0)),
            pl.BlockSpec((d, n_kd), lambda b: (0, 0)),
            pl.BlockSpec((1, n_kd), lambda b: (0, 0)),
        ],
        out_specs=(pl.BlockSpec((None, c, n_kd), lambda b: (b, 0, 0)),
                   pl.BlockSpec((None, c, n_kd), lambda b: (b, 0, 0))),
        compiler_params=_cparams("arbitrary"),
        name="attn_ctx_kv",
    )(ctx, csc, csh, w_kv, b_kv)


def _attn_kernel(sink_ref, q_ref, kp_ref, kc_ref, kn_ref, vp_ref, vc_ref, vn_ref, kx_ref, vx_ref, o_ref,
                 *, n_kv, group, nb):
    n = pl.program_id(1)
    r = lax.broadcasted_iota(I32, (BLOCK, BLOCK), 0)
    j = lax.broadcasted_iota(I32, (BLOCK, BLOCK), 1)
    prev_ok = (j >= r) & (n > 0)
    next_ok = (j <= r) & (n < nb - 1)
    lo = lax.broadcasted_iota(I32, (1, LANES), 1) < (LANES // 2)
    pairs = group // 2
    for kh in range(n_kv):
        sl = slice(kh * LANES, (kh + 1) * LANES)
        kcat = jnp.concatenate([kp_ref[:, sl], kc_ref[:, sl], kn_ref[:, sl], kx_ref[:, sl]], axis=0)
        vcat = jnp.concatenate([vp_ref[:, sl], vc_ref[:, sl], vn_ref[:, sl], vx_ref[:, sl]], axis=0)
        nkeys = kcat.shape[0]
        parts = []
        for pp in range(pairs):
            q2 = q_ref[:, (kh * pairs + pp) * LANES:(kh * pairs + pp + 1) * LANES]
            zq = jnp.zeros_like(q2)
            parts += [jnp.where(lo, q2, zq), jnp.where(lo, zq, q2)]
        qs = jnp.concatenate(parts, axis=0)
        s_all = lax.dot_general(qs, kcat, (((1,), (1,)), ((), ())), preferred_element_type=F32)
        e_parts, rdens = [], []
        for g in range(group):
            s = s_all[g * BLOCK:(g + 1) * BLOCK]
            s = jnp.concatenate([
                jnp.where(prev_ok, s[:, :BLOCK], NEG_INF),
                s[:, BLOCK:2 * BLOCK],
                jnp.where(next_ok, s[:, 2 * BLOCK:3 * BLOCK], NEG_INF),
                s[:, 3 * BLOCK:]], axis=1)
            sk = sink_ref[kh * group + g]
            m = jnp.maximum(jnp.max(s, axis=1, keepdims=True), sk)
            e = jnp.exp(s - m)
            rdens.append(1.0 / (jnp.sum(e, axis=1, keepdims=True) + jnp.exp(sk - m)))
            e_parts.append(e.astype(BF16))
        o = jnp.dot(jnp.concatenate(e_parts, axis=0), vcat, preferred_element_type=F32)
        for pp in range(pairs):
            p = kh * pairs + pp
            o_lo = o[(2 * pp) * BLOCK:(2 * pp + 1) * BLOCK] * rdens[2 * pp]
            o_hi = o[(2 * pp + 1) * BLOCK:(2 * pp + 2) * BLOCK] * rdens[2 * pp + 1]
            o_ref[:, p * LANES:(p + 1) * LANES] = jnp.where(lo, o_lo, o_hi).astype(BF16)


def _attn_call(sink, q, kd, vd, kxd, vxd, n_kv, group):
    bsz, s, n_q = q.shape
    n_kd = kd.shape[2]
    c = kxd.shape[1]
    nb = s // BLOCK
    kern = functools.partial(_attn_kernel, n_kv=n_kv, group=group, nb=nb)
    prev = lambda b, n: (b, jnp.maximum(n - 1, 0), 0)
    cur = lambda b, n: (b, n, 0)
    nxt = lambda b, n: (b, jnp.minimum(n + 1, nb - 1), 0)
    kv = lambda im: pl.BlockSpec((None, BLOCK, n_kd), im)
    return pl.pallas_call(
        kern,
        out_shape=jax.ShapeDtypeStruct((bsz, s, n_q), BF16),
        grid=(bsz, nb),
        in_specs=[
            pl.BlockSpec(memory_space=pltpu.SMEM),
            pl.BlockSpec((None, BLOCK, n_q), cur),
            kv(prev), kv(cur), kv(nxt), kv(prev), kv(cur), kv(nxt),
            pl.BlockSpec((None, c, n_kd), lambda b, n: (b, 0, 0)),
            pl.BlockSpec((None, c, n_kd), lambda b, n: (b, 0, 0)),
        ],
        out_specs=pl.BlockSpec((None, BLOCK, n_q), cur),
        compiler_params=_cparams("arbitrary", "arbitrary"),
        name="window_attn",
    )(sink, q, kd, kd, kd, vd, vd, vd, kxd, vxd)


def _layer_norm(r, g, b):
    mu = jnp.mean(r, axis=-1, keepdims=True)
    xc = r - mu
    var = jnp.mean(xc * xc, axis=-1, keepdims=True)
    return xc * lax.rsqrt(var + LN_EPS) * g + b


def _route(logits_t, rb):
    s = jax.nn.sigmoid(logits_t)
    sel = s + rb
    n_e = N_GROUPS * EXPERTS_PER_GROUP
    sel_r = [sel[e:e + 1, :] for e in range(n_e)]
    s_r = [s[e:e + 1, :] for e in range(n_e)]
    gscore = []
    for g in range(N_GROUPS):
        a, b, c, d = sel_r[4 * g:4 * g + 4]
        m1, n1, m2, n2 = jnp.maximum(a, b), jnp.minimum(a, b), jnp.maximum(c, d), jnp.minimum(c, d)
        gscore.append(jnp.maximum(m1, m2) + jnp.maximum(jnp.minimum(m1, m2), jnp.maximum(n1, n2)))
    best, gi = gscore[0], jnp.zeros_like(gscore[0], dtype=I32)
    for g in range(1, N_GROUPS):
        upd = gscore[g] > best
        gi = jnp.where(upd, g, gi)
        best = jnp.where(upd, gscore[g], best)

    def pick(rows, i):
        out = rows[i]
        for g in range(1, N_GROUPS):
            out = jnp.where(gi == g, rows[4 * g + i], out)
        return out

    v = [pick(sel_r, i) for i in range(EXPERTS_PER_GROUP)]
    sv = [pick(s_r, i) for i in range(EXPERTS_PER_GROUP)]

    def argmax4(vals):
        bv, bi = vals[0], jnp.zeros_like(gi)
        for i in range(1, EXPERTS_PER_GROUP):
            upd = vals[i] > bv
            bi = jnp.where(upd, i, bi)
            bv = jnp.where(upd, vals[i], bv)
        return bi

    def take4(vals, idx):
        out = vals[0]
        for i in range(1, EXPERTS_PER_GROUP):
            out = jnp.where(idx == i, vals[i], out)
        return out

    i1 = argmax4(v)
    i2 = argmax4([jnp.where(i1 == i, -jnp.inf, v[i]) for i in range(EXPERTS_PER_GROUP)])
    s1, s2 = take4(sv, i1), take4(sv, i2)
    tot = s1 + s2
    g1, g2 = s1 / tot, s2 / tot
    first_lo = i1 < i2
    i_lo, i_hi = jnp.minimum(i1, i2), jnp.maximum(i1, i2)
    g_lo, g_hi = jnp.where(first_lo, g1, g2), jnp.where(first_lo, g2, g1)
    pair = jnp.where(i_lo == 0, i_hi - 1, jnp.where(i_lo == 1, i_hi + 1, N_PAIRS - 1))
    bucket = gi * N_PAIRS + pair
    return 4 * gi + i_lo, 4 * gi + i_hi, g_lo, g_hi, bucket


def _proj_ln_kernel(a_ref, w_ref, bias_ref, h_ref, gate_ref, lng_ref, lnb_ref, sc_ref, sh_ref, rw_ref, rb_ref,
                    h1_ref, xp_ref, meta_ref, cnt_ref, run_ref, *, alpha, a_packed):
    first = (pl.program_id(0) == 0) & (pl.program_id(1) == 0)

    @pl.when(first)
    def _():
        run_ref[...] = jnp.zeros_like(run_ref)

    tm, d = h_ref.shape
    half = d // 2
    chunk = tm // PROJ_CHUNKS
    logit_parts = []
    ys = []
    for c in range(PROJ_CHUNKS):
        rs = slice(c * chunk, (c + 1) * chunk)
        a = _unpack_cols(a_ref[rs]).astype(BF16) if a_packed else a_ref[rs]
        ys.append(jnp.dot(a, w_ref[...], preferred_element_type=F32) + bias_ref[...])
    for c in range(PROJ_CHUNKS):
        rs = slice(c * chunk, (c + 1) * chunk)
        y = ys[c]
        h1 = _layer_norm(alpha * h_ref[rs] + gate_ref[...] * y, lng_ref[...], lnb_ref[...])
        h1_ref[rs] = h1
        tb = (h1 * (1.0 + sc_ref[...]) + sh_ref[...]).astype(BF16)
        bits = lax.bitcast_convert_type(tb.astype(F32), U32)
        xp_ref[rs, :half] = (bits[:, :half] & jnp.uint32(0xFFFF0000)) | (bits[:, half:] >> 16)
        logits = jnp.dot(tb, rw_ref[...], preferred_element_type=F32)
        logit_parts.append(jnp.transpose(logits)[:N_GROUPS * EXPERTS_PER_GROUP, :])
    logits_t = jnp.concatenate(logit_parts, axis=1)
    e_lo, e_hi, g_lo, g_hi, bucket = _route(logits_t, rb_ref[...])

    rows = lax.broadcasted_iota(I32, (BUCKET_ROWS, tm), 0)
    onehot = (rows == bucket).astype(F32)
    tri = (lax.broadcasted_iota(I32, (tm, tm), 0) <= lax.broadcasted_iota(I32, (tm, tm), 1)).astype(BF16)
    cum = jnp.dot(onehot.astype(BF16), tri, preferred_element_type=F32)
    run = run_ref[:, 0:1]
    rank = jnp.sum(onehot * (cum - 1.0 + run), axis=0, keepdims=True)
    new_run = run + cum[:, tm - 1:tm]
    run_ref[...] = jnp.broadcast_to(new_run, run_ref.shape)
    cnt_ref[...] = jnp.broadcast_to(new_run, cnt_ref.shape)

    mrow = lax.broadcasted_iota(I32, (8, tm), 0)
    meta = jnp.where(mrow == 0, e_lo.astype(F32), 0.0)
    meta = jnp.where(mrow == 1, e_hi.astype(F32), meta)
    meta = jnp.where(mrow == 2, g_lo, meta)
    meta = jnp.where(mrow == 3, g_hi, meta)
    meta = jnp.where(mrow == 4, bucket.astype(F32), meta)
    meta = jnp.where(mrow == 5, rank, meta)
    meta_ref[...] = meta

    grow = lax.broadcasted_iota(I32, (LANES, tm), 0)
    gates_t = jnp.where(grow == 0, g_lo, jnp.where(grow == 1, g_hi, 0.0))
    xp_ref[:, half:] = lax.bitcast_convert_type(jnp.transpose(gates_t), U32)


def _proj_ln_call(a, w, bias, h, gate, lng, lnb, sc, sh, rw_pad, rb, alpha):
    bsz, s, d = h.shape
    a_packed = a.dtype == U32
    da = a.shape[2]
    dm = w.shape[0]
    tm = min(TOKEN_TILE, s)
    t = bsz * s
    nt = s // tm
    dp = d // 2 + LANES
    kern = functools.partial(_proj_ln_kernel, alpha=alpha, a_packed=a_packed)
    vec = lambda: pl.BlockSpec((1, d), lambda b, i: (0, 0))
    bvec = lambda: pl.BlockSpec((None, 1, d), lambda b, i: (b, 0, 0))
    return pl.pallas_call(
        kern,
        out_shape=(jax.ShapeDtypeStruct((bsz, s, d), F32),
                   jax.ShapeDtypeStruct((t, dp), U32),
                   jax.ShapeDtypeStruct((8, t), F32),
                   jax.ShapeDtypeStruct((BUCKET_ROWS, LANES), F32)),
        grid=(bsz, nt),
        in_specs=[
            pl.BlockSpec((None, tm, da), lambda b, i: (b, i, 0)),
            _resident((dm, d), lambda b, i: (0, 0)),
            vec(),
            pl.BlockSpec((None, tm, d), lambda b, i: (b, i, 0)),
            bvec(), vec(), vec(), bvec(), bvec(),
            pl.BlockSpec((d, LANES), lambda b, i: (0, 0)),
            pl.BlockSpec((N_GROUPS * EXPERTS_PER_GROUP, 1), lambda b, i: (0, 0)),
        ],
        out_specs=(pl.BlockSpec((None, tm, d), lambda b, i: (b, i, 0)),
                   pl.BlockSpec((tm, dp), lambda b, i: (b * nt + i, 0)),
                   pl.BlockSpec((8, tm), lambda b, i: (0, b * nt + i)),
                   pl.BlockSpec((BUCKET_ROWS, LANES), lambda b, i: (0, 0))),
        scratch_shapes=[pltpu.VMEM((BUCKET_ROWS, LANES), F32)],
        compiler_params=_cparams("arbitrary", "arbitrary"),
        name="proj_ln_route",
    )(a, w, bias, h, gate, lng, lnb, sc, sh, rw_pad, rb)


def _wait_tile(src_tile, dst_tile, sem):
    pltpu.make_async_copy(src_tile, dst_tile, sem).wait()


def _scatter_kernel(dest_ref, xp_ref, init_ref, xs_ref, sem):
    del init_ref
    tm = xp_ref.shape[0]
    base = pl.program_id(0) * tm

    def issue(i, c):
        for p in range(DMA_QUEUES):
            r = i * DMA_QUEUES + p
            pltpu.make_async_copy(xp_ref.at[pl.ds(r, 1)], xs_ref.at[pl.ds(dest_ref[base + r], 1)],
                                  sem).start(priority=p)
        return c

    lax.fori_loop(0, tm // DMA_QUEUES, issue, 0, unroll=4)
    _wait_tile(xp_ref, xs_ref.at[pl.ds(0, tm)], sem)


def _scatter_call(dest, xp, rows_out):
    t, dp = xp.shape
    tm = min(TOKEN_TILE, t)
    init = jnp.zeros((rows_out, dp), U32)
    return pl.pallas_call(
        _scatter_kernel,
        out_shape=jax.ShapeDtypeStruct((rows_out, dp), U32),
        grid_spec=pltpu.PrefetchScalarGridSpec(
            num_scalar_prefetch=1,
            grid=(t // tm,),
            in_specs=[pl.BlockSpec((tm, dp), lambda i, dest: (i, 0)),
                      pl.BlockSpec(memory_space=pl.ANY)],
            out_specs=pl.BlockSpec(memory_space=pl.ANY),
            scratch_shapes=[pltpu.SemaphoreType.DMA],
        ),
        input_output_aliases={2: 0},
        compiler_params=_cparams("arbitrary"),
        name="moe_scatter",
    )(dest, xp, init)


def _moe_kernel(ex_ref, valid_ref, xs_ref, w1_ref, w3_ref, w2_ref, y_ref, acc_ref):
    del ex_ref
    i = pl.program_id(0)
    s = pl.program_id(1)
    half = xs_ref.shape[1] - LANES
    ok = valid_ref[i] > 0

    @pl.when((i == 0) & (s == 0))
    def _():
        acc_ref[...] = jnp.zeros_like(acc_ref)

    @pl.when(ok)
    def _():
        w = xs_ref[:, :half]
        hi = lax.bitcast_convert_type(w & jnp.uint32(0xFFFF0000), F32).astype(BF16)
        lo = lax.bitcast_convert_type(w << 16, F32).astype(BF16)
        x = jnp.concatenate([hi, lo], axis=1)
        gates = lax.bitcast_convert_type(xs_ref[:, half:], F32)
        which = (s + i) % 2
        g = jnp.where(which == 0, gates[:, 0:1], gates[:, 1:2])
        a = jnp.dot(x, w1_ref[...], preferred_element_type=F32)
        b = jnp.dot(x, w3_ref[...], preferred_element_type=F32)
        hid = (_silu(a) * b * g).astype(BF16)
        y = jnp.dot(hid, w2_ref[...], preferred_element_type=F32)
        total = jnp.where(s == 0, 0.0, acc_ref[...]) + y
        acc_ref[...] = total
        y_ref[...] = _pack_cols(total)

    @pl.when(jnp.logical_not(ok) & (s == 0))
    def _():
        y_ref[...] = jnp.zeros_like(y_ref)


def _moe_call(ex, valid, xs, w1, w3, w2, layer):
    rows, dp = xs.shape
    _, n_e, d, ff = w1.shape
    tm = min(TOKEN_TILE, rows)
    ntiles = rows // tm
    return pl.pallas_call(
        _moe_kernel,
        out_shape=jax.ShapeDtypeStruct((rows, d // 2), U32),
        grid_spec=pltpu.PrefetchScalarGridSpec(
            num_scalar_prefetch=2,
            grid=(ntiles, 2),
            in_specs=[
                pl.BlockSpec((tm, dp), lambda i, s, ex, va: (i, 0)),
                pl.BlockSpec((None, None, d, ff), lambda i, s, ex, va: (layer, ex[2 * i + s], 0, 0)),
                pl.BlockSpec((None, None, d, ff), lambda i, s, ex, va: (layer, ex[2 * i + s], 0, 0)),
                pl.BlockSpec((None, None, ff, d), lambda i, s, ex, va: (layer, ex[2 * i + s], 0, 0)),
            ],
            out_specs=pl.BlockSpec((tm, d // 2), lambda i, s, ex, va: (i, 0)),
            scratch_shapes=[pltpu.VMEM((tm, d), F32)],
        ),
        compiler_params=_cparams("arbitrary", "arbitrary"),
        name="moe_experts",
    )(ex, valid, xs, w1, w3, w2)


GATHER_CHUNKS = 8


def _gather_ln_kernel(dest_ref, ys_ref, h_ref, gate_ref, lng_ref, lnb_ref, o_ref, buf, sem, *, alpha, nt):
    tm = h_ref.shape[0]
    step = pl.program_id(0) * nt + pl.program_id(1)
    last = pl.num_programs(0) * nt - 1
    slot = step % 2

    def row_copy(tile, sl, r):
        return pltpu.make_async_copy(ys_ref.at[pl.ds(dest_ref[tile * tm + r], 1)], buf.at[sl, pl.ds(r, 1)],
                                     sem.at[sl])

    def wait_tile(sl):
        _wait_tile(ys_ref.at[pl.ds(0, tm)], buf.at[sl], sem.at[sl])

    @pl.when(step == 0)
    def _():
        def issue(i, c):
            for p in range(DMA_QUEUES):
                row_copy(0, 0, i * DMA_QUEUES + p).start(priority=p)
            return c

        lax.fori_loop(0, tm // DMA_QUEUES, issue, 0, unroll=4)

    wait_tile(slot)
    nxt = jnp.minimum(step + 1, last)
    rows = tm // GATHER_CHUNKS
    for c in range(GATHER_CHUNKS):
        for r in range(c * rows, (c + 1) * rows):
            row_copy(nxt, 1 - slot, r).start(priority=r % DMA_QUEUES)
        sl = slice(c * rows, (c + 1) * rows)
        f = _unpack_cols(buf[slot, sl])
        o_ref[sl] = _layer_norm(alpha * h_ref[sl] + gate_ref[...] * f, lng_ref[...], lnb_ref[...])

    @pl.when(step == last)
    def _():
        wait_tile(1 - slot)


def _gather_ln_call(dest, ys, h, gate, lng, lnb, alpha):
    bsz, s, d = h.shape
    tm = min(TOKEN_TILE, s)
    nt = s // tm
    kern = functools.partial(_gather_ln_kernel, alpha=alpha, nt=nt)
    return pl.pallas_call(
        kern,
        out_shape=jax.ShapeDtypeStruct((bsz, s, d), F32),
        grid_spec=pltpu.PrefetchScalarGridSpec(
            num_scalar_prefetch=1,
            grid=(bsz, nt),
            in_specs=[
                pl.BlockSpec(memory_space=pl.ANY),
                pl.BlockSpec((None, tm, d), lambda b, i, dest: (b, i, 0)),
                pl.BlockSpec((None, 1, d), lambda b, i, dest: (b, 0, 0)),
                pl.BlockSpec((1, d), lambda b, i, dest: (0, 0)),
                pl.BlockSpec((1, d), lambda b, i, dest: (0, 0)),
            ],
            out_specs=pl.BlockSpec((None, tm, d), lambda b, i, dest: (b, i, 0)),
            scratch_shapes=[pltpu.VMEM((2, tm, d // 2), U32), pltpu.SemaphoreType.DMA((2,))],
        ),
        compiler_params=_cparams("arbitrary", "arbitrary"),
        name="moe_gather_ln",
    )(dest, ys, h, gate, lng, lnb)


_PAIR_LO = (0, 0, 0, 1, 1, 2)
_PAIR_HI = (1, 2, 3, 2, 3, 3)


def _moe_layer(xp, meta, cnt, h1, gate2, lng, lnb, w1, w3, w2, layer, alpha):
    t = xp.shape[0]
    tm = min(TOKEN_TILE, t)
    ntiles = t // tm + N_BUCKETS
    counts = cnt[:N_BUCKETS, 0].astype(I32)
    tiles_b = (counts + tm - 1) // tm
    tile_end = jnp.cumsum(tiles_b)
    offs = (tile_end - tiles_b) * tm
    bucket = meta[4].astype(I32)
    dest = offs[bucket] + meta[5].astype(I32)

    tile = jnp.arange(ntiles, dtype=I32)
    valid = (tile < tile_end[-1]).astype(I32)
    tb = jnp.minimum(jnp.sum((tile[:, None] >= tile_end[None, :]).astype(I32), axis=1), N_BUCKETS - 1)
    lo = jnp.asarray(_PAIR_LO, I32)[tb % N_PAIRS] + EXPERTS_PER_GROUP * (tb // N_PAIRS)
    hi = jnp.asarray(_PAIR_HI, I32)[tb % N_PAIRS] + EXPERTS_PER_GROUP * (tb // N_PAIRS)
    odd = (tile % 2) == 1
    ex = jnp.stack([jnp.where(odd, hi, lo), jnp.where(odd, lo, hi)], axis=1).reshape(-1)

    xs = _scatter_call(dest, xp, ntiles * tm)
    ys = _moe_call(ex, valid, xs, w1, w3, w2, layer)
    return _gather_ln_call(dest, ys, h1, gate2, lng, lnb, alpha)


def _hy_in_kernel(xm_ref, xp_ref, xn_ref, sc_ref, sh_ref, w_ref, b_ref, cw_ref, cb_ref, x0_ref, z_ref, *, nt, tn):
    i = pl.program_id(1)
    tm, d = xm_ref.shape
    sc, sh = 1.0 + sc_ref[...], sh_ref[...]
    u = jnp.concatenate([xp_ref[...] * sc + sh, xm_ref[...] * sc + sh, xn_ref[...] * sc + sh], axis=0).astype(BF16)
    rows = tm + 16
    rid = lax.broadcasted_iota(I32, (rows, 1), 0)
    keep = ((rid >= 8) | (i > 0)) & ((rid < tm + 8) | (i < nt - 1))

    def conv(sec, j):
        col = sec * d + j * tn
        p = jnp.dot(u, w_ref[:, col:col + tn], preferred_element_type=F32) + b_ref[:, col:col + tn]
        p = jnp.where(keep, p, 0.0)
        cw = cw_ref[:, col:col + tn]
        out = (cw[0:1] * pltpu.roll(p, 1, 0) + cw[1:2] * p + cw[2:3] * pltpu.roll(p, rows - 1, 0)
               + cb_ref[:, col:col + tn])
        return out[8:8 + tm]

    nch = d // tn
    for j in range(nch // 2):
        x0_ref[:, j * tn:(j + 1) * tn] = _pack_pair(conv(0, j), conv(0, j + nch // 2))
        z_ref[:, j * tn:(j + 1) * tn] = _pack_pair(conv(1, j) * conv(2, j),
                                                   conv(1, j + nch // 2) * conv(2, j + nch // 2))


def _hy_in_call(h, sc, sh, w, b, cw, cb):
    bsz, s, d = h.shape
    tm = min(TOKEN_TILE, s)
    nt = s // tm
    hb = tm // 8
    tn = 1024
    kern = functools.partial(_hy_in_kernel, nt=nt, tn=tn)
    return pl.pallas_call(
        kern,
        out_shape=(jax.ShapeDtypeStruct((bsz, s, d // 2), U32), jax.ShapeDtypeStruct((bsz, s, d // 2), U32)),
        grid=(bsz, nt),
        in_specs=[
            pl.BlockSpec((None, tm, d), lambda b, i: (b, i, 0)),
            pl.BlockSpec((None, 8, d), lambda b, i: (b, jnp.maximum(i * hb - 1, 0), 0)),
            pl.BlockSpec((None, 8, d), lambda b, i: (b, jnp.minimum((i + 1) * hb, nt * hb - 1), 0)),
            pl.BlockSpec((None, 1, d), lambda b, i: (b, 0, 0)),
            pl.BlockSpec((None, 1, d), lambda b, i: (b, 0, 0)),
            _resident((d, 3 * d), lambda b, i: (0, 0)),
            pl.BlockSpec((1, 3 * d), lambda b, i: (0, 0)),
            pl.BlockSpec((3, 3 * d), lambda b, i: (0, 0)),
            pl.BlockSpec((1, 3 * d), lambda b, i: (0, 0)),
        ],
        out_specs=(pl.BlockSpec((None, tm, d // 2), lambda b, i: (b, i, 0)),
                   pl.BlockSpec((None, tm, d // 2), lambda b, i: (b, i, 0))),
        compiler_params=_cparams("arbitrary", "arbitrary"),
        name="hyena_in",
    )(h, h, h, sc, sh, w, b, cw, cb)


def _filter_kernel(w1_ref, b1_ref, w2_ref, b2_ref, w3_ref, b3_ref, fr_ref, w4_ref, dl_ref, f_ref, l1_ref,
                   *, seq, tn):
    i = pl.program_id(0)

    @pl.when(i == 0)
    def _():
        l1_ref[...] = jnp.zeros_like(l1_ref)

    n_lane = i * tn + lax.broadcasted_iota(I32, (1, tn), 1)
    m_lane = jnp.where(n_lane < seq, n_lane, 2 * seq - n_lane).astype(F32)
    t_lane = m_lane / (seq - 1.0)
    wl = (2.0 * math.pi) * m_lane / float(seq)
    band = lax.broadcasted_iota(I32, (HY_BANDS, 1), 0).astype(F32)
    fb = 1e-4 + band * ((HY_BANDS - 1 - 1e-4) / (HY_BANDS - 1))
    ang = fb * wl
    z = jnp.concatenate([t_lane, jnp.cos(ang), -jnp.sin(ang),
                         jnp.zeros((7, tn), F32)], axis=0)
    fr = fr_ref[...]

    def layer(w_ref, b_ref, x):
        pre = lax.dot_general(w_ref[...], x, (((0,), (0,)), ((), ())), preferred_element_type=F32,
                              precision=HIGHEST)
        return jnp.sin(fr * (pre + b_ref[...]))

    hdn = layer(w1_ref, b1_ref, z)
    hdn = layer(w2_ref, b2_ref, hdn)
    hdn = layer(w3_ref, b3_ref, hdn)
    h = lax.dot_general(hdn.astype(BF16), w4_ref[...].astype(BF16), (((0,), (0,)), ((), ())),
                        preferred_element_type=F32)
    n_col = i * tn + lax.broadcasted_iota(I32, (tn, 1), 0)
    m_col = jnp.where(n_col < seq, n_col, 2 * seq - n_col).astype(F32)
    h = h * jnp.exp(-(m_col / (seq - 1.0)) * dl_ref[...])
    h = jnp.where(n_col == seq, 0.0, h)
    l1_ref[...] += jnp.sum(jnp.abs(h), axis=0, keepdims=True)
    f_ref[...] = _pack_cols(h)


def _filter_call(fw1, fb1, fw2, fb2, fw3, fb3, freq, fw4, seq):
    width = fw2.shape[0]
    d = fw4.shape[1] // 2
    tn = min(512, seq)
    steps = 2 * seq // tn
    max_decay = math.log(HY_TARGET) / HY_FAST_DECAY
    min_decay = math.log(HY_TARGET) / HY_SLOW_DECAY
    deltas = jnp.abs(jnp.linspace(min_decay, max_decay, d, dtype=F32)).reshape(1, d)
    w1p = jnp.concatenate([fw1, jnp.zeros((7, width), F32)], axis=0)
    col = lambda a: a.reshape(width, 1)
    small = lambda shp: pl.BlockSpec(shp, lambda i: (0, 0))
    kern = functools.partial(_filter_kernel, seq=seq, tn=tn)
    return pl.pallas_call(
        kern,
        out_shape=(jax.ShapeDtypeStruct((2 * seq, d // 2), U32), jax.ShapeDtypeStruct((1, d), F32)),
        grid=(steps,),
        in_specs=[small((40, width)), small((width, 1)), small((width, width)), small((width, 1)),
                  small((width, width)), small((width, 1)), small((width, 1)),
                  pl.BlockSpec((width, d), lambda i: (0, (i * tn) // seq)),
                  small((1, d))],
        out_specs=(pl.BlockSpec((tn, d // 2), lambda i: (i, 0)), small((1, d))),
        compiler_params=_cparams("arbitrary"),
        name="hyena_filter",
    )(w1p, col(fb1), fw2, col(fb2), fw3, col(fb3), col(freq), fw4, deltas)


def _dft_tables(n1, n2):
    n = n1 * n2
    k = np.arange(n1)[:, None]
    m = np.arange(n1)[None, :]
    ang1 = -2.0 * np.pi * ((k * m) % n1) / n1
    f1r, f1i = np.cos(ang1), np.sin(ang1)
    hn = n1 // 2
    a_data = np.block([[f1r[:, :hn], -f1i[:, :hn]], [f1i[:, :hn], f1r[:, :hn]]])
    a_filt = np.concatenate([f1r, f1i], axis=0)
    a_inv = np.block([[f1r.T[:hn], f1i.T[:hn]], [-f1i.T[:hn], f1r.T[:hn]]]) / n
    k2 = np.arange(n2)[:, None]
    m2 = np.arange(n2)[None, :]
    ang2 = -2.0 * np.pi * ((k2 * m2) % n2) / n2
    f2r, f2i = np.cos(ang2), np.sin(ang2)
    b_fwd = np.block([[f2r, -f2i], [f2i, f2r]])
    b_inv = np.block([[f2r, f2i], [-f2i, f2r]])
    angt = -2.0 * np.pi * ((np.arange(n2)[:, None] * np.arange(n1)[None, :]) % n) / n
    tw = (np.cos(angt), np.sin(angt))
    perm = (np.arange(n1 // FFT_K1_GROUP)[None, :] * FFT_K1_GROUP + np.arange(FFT_K1_GROUP)[:, None]).reshape(-1)
    perm2 = np.concatenate([perm, perm + n1])
    as_bf16 = lambda a: jnp.asarray(a, F32).astype(BF16)
    return dict(a_data=as_bf16(a_data[perm2]), a_filt=as_bf16(a_filt[perm2]), a_inv=as_bf16(a_inv[:, perm2]),
                b_fwd=as_bf16(b_fwd), b_inv=as_bf16(b_inv),
                tw_n2=tuple(jnp.asarray(t[:, perm], F32).reshape(n2, n1, 1) for t in tw),
                tw_k1=tuple(jnp.asarray(t.T.copy(), F32).reshape(n1, n2, 1) for t in tw))


FFT_K1_GROUP = 4


def _pack_pair(hi, lo):
    hb = lax.bitcast_convert_type(hi.astype(BF16).astype(F32), U32)
    lb = lax.bitcast_convert_type(lo.astype(BF16).astype(F32), U32)
    return (hb & jnp.uint32(0xFFFF0000)) | (lb >> 16)


def _unpack_pair(w):
    hi = lax.bitcast_convert_type(w & jnp.uint32(0xFFFF0000), F32)
    lo = lax.bitcast_convert_type(w << 16, F32)
    return hi, lo


def _fetch_step(view, buf, sem):
    k = pl.program_id(0)
    slot = k % 2

    def copy(step, sl):
        return pltpu.make_async_copy(view(step), buf.at[sl], sem.at[sl])

    @pl.when(k == 0)
    def _():
        copy(0, 0).start()

    @pl.when(k + 1 < pl.num_programs(0))
    def _():
        copy(k + 1, 1 - slot).start()

    copy(k, slot).wait()
    return buf.at[slot]


def _store_step(view, buf, sem, fill):
    k = pl.program_id(0)
    slot = k % 2

    def copy(step, sl):
        return pltpu.make_async_copy(buf.at[sl], view(step), sem.at[sl])

    @pl.when(k >= 2)
    def _():
        copy(k - 2, slot).wait()

    fill(buf.at[slot])
    copy(k, slot).start()

    @pl.when(k == pl.num_programs(0) - 1)
    def _():
        copy(k, slot).wait()

        @pl.when(k >= 1)
        def _():
            copy(k - 1, 1 - slot).wait()


def _unpack_cols(w):
    hi, lo = _unpack_pair(w)
    return jnp.concatenate([hi, lo], axis=1)


def _pack_cols(x):
    half = x.shape[1] // 2
    return _pack_pair(x[:, :half], x[:, half:])


def _fft_a_kernel(x_hbm, a_ref, twr_ref, twi_ref, y_ref, xbuf, sem):
    n1 = a_ref.shape[0] // 2
    x_ref = _fetch_step(lambda j: x_hbm.at[:, :, j, :], xbuf, sem)
    rhs = jnp.concatenate([_unpack_cols(x_ref[0]), _unpack_cols(x_ref[1])], axis=0).astype(BF16)
    y = jnp.dot(a_ref[...], rhs, preferred_element_type=F32)
    yr, yi = y[:n1], y[n1:]
    tr, ti = twr_ref[...], twi_ref[...]
    w = _pack_pair(yr * tr - yi * ti, yr * ti + yi * tr)
    q = n1 // FFT_K1_GROUP
    d = w.shape[1]
    for kk in range(FFT_K1_GROUP):
        y_ref[:, kk * d:(kk + 1) * d] = w[kk * q:(kk + 1) * q]


def _fft_a_call(x4, a_mat, tw_n2):
    _, hn, n2, half = x4.shape
    d = 2 * half
    n1 = 2 * hn
    q = n1 // FFT_K1_GROUP
    return pl.pallas_call(
        _fft_a_kernel,
        out_shape=jax.ShapeDtypeStruct((n2, q, FFT_K1_GROUP * d), U32),
        grid=(n2,),
        in_specs=[
            pl.BlockSpec(memory_space=pl.ANY),
            pl.BlockSpec((2 * n1, n1), lambda j: (0, 0)),
            pl.BlockSpec((None, n1, 1), lambda j: (j, 0, 0)),
            pl.BlockSpec((None, n1, 1), lambda j: (j, 0, 0)),
        ],
        out_specs=pl.BlockSpec((None, q, FFT_K1_GROUP * d), lambda j: (j, 0, 0)),
        scratch_shapes=[pltpu.VMEM((2, 2, hn, half), U32), pltpu.SemaphoreType.DMA((2,))],
        compiler_params=_cparams("arbitrary"),
        name="fft_stage_a",
    )(x4, a_mat, tw_n2[0], tw_n2[1])


def _load_k1(y_ref, kk):
    d = y_ref.shape[1] // FFT_K1_GROUP
    yr, yi = _unpack_pair(y_ref[:, kk * d:(kk + 1) * d])
    return jnp.concatenate([yr, yi], axis=0).astype(BF16)


def _fft_b_kernel(y_hbm, b_ref, h_ref, ybuf, sem):
    y_ref = _fetch_step(lambda kb: y_hbm.at[:, kb, :], ybuf, sem)
    n2 = y_ref.shape[0]
    for kk in range(FFT_K1_GROUP):
        x = jnp.dot(b_ref[...], _load_k1(y_ref, kk), preferred_element_type=F32)
        h_ref[kk] = _pack_pair(x[:n2], x[n2:])


def _fft_b_call(y, b_fwd):
    n2, q, gd = y.shape
    d = gd // FFT_K1_GROUP
    return pl.pallas_call(
        _fft_b_kernel,
        out_shape=jax.ShapeDtypeStruct((q * FFT_K1_GROUP, n2, d), U32),
        grid=(q,),
        in_specs=[pl.BlockSpec(memory_space=pl.ANY),
                  pl.BlockSpec((2 * n2, 2 * n2), lambda k: (0, 0))],
        out_specs=pl.BlockSpec((FFT_K1_GROUP, n2, d), lambda k: (k, 0, 0)),
        scratch_shapes=[pltpu.VMEM((2, n2, gd), U32), pltpu.SemaphoreType.DMA((2,))],
        compiler_params=_cparams("arbitrary"),
        name="fft_filter_b",
    )(y, b_fwd)


def _fft_bc_kernel(y_hbm, h_ref, bf_ref, bi_ref, twr_ref, twi_ref, g_hbm, ybuf, gbuf, sem_in, sem_out):
    y_ref = _fetch_step(lambda kb: y_hbm.at[:, kb, :], ybuf, sem_in)
    n2 = y_ref.shape[0]
    d = y_ref.shape[1] // FFT_K1_GROUP

    def fill(g_ref):
        for kk in range(FFT_K1_GROUP):
            x = jnp.dot(bf_ref[...], _load_k1(y_ref, kk), preferred_element_type=F32)
            xr, xi = x[:n2], x[n2:]
            hr, hi = _unpack_pair(h_ref[kk])
            z = jnp.concatenate([xr * hr - xi * hi, xr * hi + xi * hr], axis=0).astype(BF16)
            g = jnp.dot(bi_ref[...], z, preferred_element_type=F32)
            gr, gi = g[:n2], g[n2:]
            tr, ti = twr_ref[kk], twi_ref[kk]
            g_ref[:, kk * d:(kk + 1) * d] = _pack_pair(gr * tr + gi * ti, gi * tr - gr * ti)

    _store_step(lambda kb: g_hbm.at[:, kb, :], gbuf, sem_out, fill)


def _fft_bc_call(y, hspec, b_fwd, b_inv, tw_k1):
    n2, q, gd = y.shape
    d = gd // FFT_K1_GROUP
    mat = lambda: pl.BlockSpec((2 * n2, 2 * n2), lambda k: (0, 0))
    tw = lambda: pl.BlockSpec((FFT_K1_GROUP, n2, 1), lambda k: (k, 0, 0))
    return pl.pallas_call(
        _fft_bc_kernel,
        out_shape=jax.ShapeDtypeStruct((n2, q, gd), U32),
        grid=(q,),
        in_specs=[pl.BlockSpec(memory_space=pl.ANY),
                  pl.BlockSpec((FFT_K1_GROUP, n2, d), lambda k: (k, 0, 0)), mat(), mat(), tw(), tw()],
        out_specs=pl.BlockSpec(memory_space=pl.ANY),
        scratch_shapes=[pltpu.VMEM((2, n2, gd), U32), pltpu.VMEM((2, n2, gd), U32),
                        pltpu.SemaphoreType.DMA((2,)), pltpu.SemaphoreType.DMA((2,))],
        compiler_params=_cparams("arbitrary"),
        name="fft_stage_bc",
    )(y, hspec, b_fwd, b_inv, tw_k1[0], tw_k1[1])


def _fft_d_kernel(g_ref, a_ref, x0_hbm, z_hbm, l1_ref, fb_ref, o_hbm, x0buf, zbuf, obuf, sem_x, sem_z, sem_o):
    hn = a_ref.shape[0] // 2
    d = g_ref.shape[1] // FFT_K1_GROUP
    x0_ref = _fetch_step(lambda j: x0_hbm.at[:, :, j, :], x0buf, sem_x)
    z_ref = _fetch_step(lambda j: z_hbm.at[:, :, j, :], zbuf, sem_z)
    parts = [_unpack_pair(g_ref[:, kk * d:(kk + 1) * d]) for kk in range(FFT_K1_GROUP)]
    rhs = jnp.concatenate([p[0] for p in parts] + [p[1] for p in parts], axis=0).astype(BF16)
    y = jnp.dot(a_ref[...], rhs, preferred_element_type=F32)
    inv_l1 = 1.0 / l1_ref[...]
    fb = fb_ref[...]

    def fill(o_ref):
        for b in range(2):
            conv = y[b * hn:(b + 1) * hn] * inv_l1
            o_ref[b] = _pack_cols(_unpack_cols(x0_ref[b]) * (conv + fb * _unpack_cols(z_ref[b])))

    _store_step(lambda j: o_hbm.at[:, :, j, :], obuf, sem_o, fill)


def _fft_d_call(g, a_inv, x0_4, z_4, l1, fbias):
    n2, q, gd = g.shape
    d = gd // FFT_K1_GROUP
    n1 = q * FFT_K1_GROUP
    hn = n1 // 2
    half = d // 2
    tok_buf = lambda: pltpu.VMEM((2, 2, hn, half), U32)
    return pl.pallas_call(
        _fft_d_kernel,
        out_shape=jax.ShapeDtypeStruct((2, hn, n2, half), U32),
        grid=(n2,),
        in_specs=[pl.BlockSpec((None, q, gd), lambda j: (j, 0, 0)),
                  pl.BlockSpec((n1, 2 * n1), lambda j: (0, 0)),
                  pl.BlockSpec(memory_space=pl.ANY), pl.BlockSpec(memory_space=pl.ANY),
                  pl.BlockSpec((1, d), lambda j: (0, 0)),
                  pl.BlockSpec((1, d), lambda j: (0, 0))],
        out_specs=pl.BlockSpec(memory_space=pl.ANY),
        scratch_shapes=[tok_buf(), tok_buf(), tok_buf(),
                        pltpu.SemaphoreType.DMA((2,)), pltpu.SemaphoreType.DMA((2,)), pltpu.SemaphoreType.DMA((2,))],
        compiler_params=_cparams("arbitrary"),
        name="fft_stage_d",
    )(g, a_inv, x0_4, z_4, l1, fbias)


def _hyena_conv(x0p, zp, filtp, l1, fbias):
    bsz, seq, half = zp.shape
    assert bsz == 2, "the two batch rows are packed as one complex signal"
    n2 = FFT_N2
    n1 = 2 * seq // n2
    tabs = _dft_tables(n1, n2)
    hn = n1 // 2
    view = lambda a: a.reshape(2, hn, n2, half)
    hspec = _fft_b_call(_fft_a_call(view(filtp), tabs["a_filt"], tabs["tw_n2"]), tabs["b_fwd"])
    y = _fft_a_call(view(zp), tabs["a_data"], tabs["tw_n2"])
    g = _fft_bc_call(y, hspec, tabs["b_fwd"], tabs["b_inv"], tabs["tw_k1"])
    out = _fft_d_call(g, tabs["a_inv"], view(x0p), view(zp), l1, fbias)
    return out.reshape(bsz, seq, half)


def _rope_tables(seq, head_dim):
    axis_dim = head_dim // 2
    rows = seq // GRID_W
    inv = ROPE_BASE ** (-jnp.arange(0, axis_dim, 2, dtype=F32) / axis_dim)
    row = jnp.repeat(jnp.arange(rows, dtype=F32), GRID_W)[:, None] * inv
    col = jnp.tile(jnp.arange(GRID_W, dtype=F32), rows)[:, None] * inv
    quarter = axis_dim // 2
    cos = jnp.concatenate([jnp.cos(row), jnp.cos(row), jnp.cos(col), jnp.cos(col)], axis=1)
    sin = jnp.concatenate([jnp.sin(row), jnp.sin(row), jnp.sin(col), jnp.sin(col)], axis=1)
    second = (np.arange(head_dim) % axis_dim) >= quarter
    reps = LANES // head_dim
    cos = jnp.tile(cos, (1, reps))
    sin = jnp.tile(sin, (1, reps))
    second = jnp.asarray(np.tile(second, reps))[None, :]
    return cos, jnp.where(second, sin, 0.0), jnp.where(second, 0.0, -sin)


def kernel(x, c, ctx, c_ctx, ada_w, ada_b, attn_w_in, attn_b_in, attn_sink, attn_w_out, hy_w_in, hy_b_in, hy_conv_w, hy_conv_b, hy_f_w1, hy_f_b1, hy_f_w2, hy_f_b2, hy_f_w3, hy_f_b3, hy_f_freq, hy_f_w4, hy_f_bias, hy_w_out, hy_b_out, ln1_g, ln1_b, ln2_g, ln2_b, router_w, router_b, moe_w1, moe_w3, moe_w2):
    bsz, seq, d = x.shape
    depth = ada_w.shape[0]
    assert depth == 2 and attn_w_in.shape[0] == 1 and hy_w_in.shape[0] == 1
    alpha = (2 * depth) ** 0.25
    n_heads = attn_sink.shape[1]
    attn_dim = attn_w_out.shape[1]
    head_dim = attn_dim // n_heads
    kv_dim = (attn_w_in.shape[2] - attn_dim) // 2
    n_kv = kv_dim // head_dim
    group = n_heads // n_kv
    assert head_dim * 2 == LANES and group % 2 == 0
    n_exp = router_w.shape[1]
    assert n_exp == N_GROUPS * EXPERTS_PER_GROUP

    cond = jnp.concatenate([c, c_ctx[None, :], jnp.zeros((8 - bsz - 1, d), F32)], axis=0)
    mods = _ada_call(cond, ada_w, ada_b).reshape(depth, 8, 6, d)
    mod = lambda layer, k: mods[layer, :bsz, k].reshape(bsz, 1, d)
    cmod = lambda layer, k: mods[layer, bsz, k].reshape(1, d)
    row = lambda v: v.reshape(1, -1)

    rw_pad = jnp.concatenate([router_w, jnp.zeros((d, LANES - n_exp), F32)], axis=1).astype(BF16)
    rb = router_b.reshape(n_exp, 1)
    w1b, w3b, w2b = moe_w1.astype(BF16), moe_w3.astype(BF16), moe_w2.astype(BF16)

    w_ext = attn_w_in[0].astype(BF16)
    b_ext = row(attn_b_in[0])
    w_kv, b_kv = w_ext[:, attn_dim:], b_ext[:, attn_dim:]
    n_kd = 2 * kv_dim
    cos_t, sa_t, sb_t = _rope_tables(seq, head_dim)

    q, kd, vd = _qkv_call(x, mod(0, 1), mod(0, 0), w_ext, b_ext, cos_t, sa_t, sb_t, attn_dim, n_kd, head_dim)
    kxd, vxd = _ctx_kv_call(ctx, cmod(0, 1), cmod(0, 0), w_kv, b_kv, n_kd)
    att = _attn_call(attn_sink[0], q, kd, vd, kxd, vxd, n_kv, group)
    h1, xp, meta, cnt = _proj_ln_call(att, attn_w_out[0].astype(BF16), jnp.zeros((1, d), F32), x, mod(0, 2),
                                      row(ln1_g[0]), row(ln1_b[0]), mod(0, 4), mod(0, 3), rw_pad, rb, alpha)
    h = _moe_layer(xp, meta, cnt, h1, mod(0, 5), row(ln2_g[0]), row(ln2_b[0]), w1b, w3b, w2b, 0, alpha)

    x0, z = _hy_in_call(h, mod(1, 1), mod(1, 0), hy_w_in[0].astype(BF16), row(hy_b_in[0]), hy_conv_w[0],
                        row(hy_conv_b[0]))
    filt, l1 = _filter_call(hy_f_w1[0], hy_f_b1[0], hy_f_w2[0], hy_f_b2[0], hy_f_w3[0], hy_f_b3[0],
                            hy_f_freq[0], hy_f_w4[0], seq)
    yh = _hyena_conv(x0, z, filt, l1, row(hy_f_bias[0]))
    h1, xp, meta, cnt = _proj_ln_call(yh, hy_w_out[0].astype(BF16), row(hy_b_out[0]), h, mod(1, 2),
                                      row(ln1_g[1]), row(ln1_b[1]), mod(1, 4), mod(1, 3), rw_pad, rb, alpha)
    return _moe_layer(xp, meta, cnt, h1, mod(1, 5), row(ln2_g[1]), row(ln2_b[1]), w1b, w3b, w2b, 1, alpha)
```

```python
import functools
import math

import numpy as np
import jax
import jax.numpy as jnp
from jax import lax
from jax.experimental import pallas as pl
from jax.experimental.pallas import tpu as pltpu

F32 = jnp.float32
BF16 = jnp.bfloat16
I32 = jnp.int32
U32 = jnp.uint32
HIGHEST = lax.Precision.HIGHEST

LANES = 128
V7X_VMEM_LIMIT_BYTES = 56 * 1024 * 1024

GRID_W = 64
BLOCK = 128
ROPE_BASE = 10000.0
NEG_INF = -1e30
HY_BANDS = 16
HY_FAST_DECAY = 0.3
HY_SLOW_DECAY = 1.5
HY_TARGET = 1e-2
N_GROUPS = 4
EXPERTS_PER_GROUP = 4
N_PAIRS = 6
N_BUCKETS = N_GROUPS * N_PAIRS
BUCKET_ROWS = 32
LN_EPS = 1e-5
FFT_N2 = 128

TOKEN_TILE = 512
PROJ_CHUNKS = 4
DMA_QUEUES = 2


def _cparams(*sem):
    return pltpu.CompilerParams(dimension_semantics=sem, vmem_limit_bytes=V7X_VMEM_LIMIT_BYTES)


def _resident(block_shape, index_map):
    return pl.BlockSpec(block_shape, index_map, pipeline_mode=pl.Buffered(1))


def _silu(x):
    return x * jax.nn.sigmoid(x)


def _ada_kernel(c_ref, w_ref, b_ref, o_ref):
    c = _silu(c_ref[...])
    o_ref[...] = jnp.dot(c, w_ref[...], preferred_element_type=F32, precision=HIGHEST) + b_ref[...]


def _ada_call(cond, ada_w, ada_b):
    depth, d, n6 = ada_w.shape
    tn = 1024
    rows = cond.shape[0]
    return pl.pallas_call(
        _ada_kernel,
        out_shape=jax.ShapeDtypeStruct((depth, rows, n6), F32),
        grid=(depth, n6 // tn),
        in_specs=[
            pl.BlockSpec((rows, d), lambda l, j: (0, 0)),
            pl.BlockSpec((None, d, tn), lambda l, j: (l, 0, j)),
            pl.BlockSpec((None, 1, tn), lambda l, j: (l, 0, j)),
        ],
        out_specs=pl.BlockSpec((None, rows, tn), lambda l, j: (l, 0, j)),
        compiler_params=_cparams("arbitrary", "arbitrary"),
        name="ada_mod",
    )(cond, ada_w, ada_b.reshape(depth, 1, n6))


def _qkv_kernel(x_ref, sc_ref, sh_ref, w_ref, b_ref, cos_ref, sa_ref, sb_ref, q_ref, k_ref, v_ref, *, scale):
    u = (x_ref[...] * (1.0 + sc_ref[...]) + sh_ref[...]).astype(BF16)
    p = jnp.dot(u, w_ref[...], preferred_element_type=F32) + b_ref[...]
    cos, sa, sb = cos_ref[...], sa_ref[...], sb_ref[...]
    nq = q_ref.shape[1]
    nk = k_ref.shape[1]

    def rope(xc):
        return xc * cos + pltpu.roll(xc, 16, 1) * sa + pltpu.roll(xc, LANES - 16, 1) * sb

    for c in range(nq // LANES):
        q_ref[:, c * LANES:(c + 1) * LANES] = (rope(p[:, c * LANES:(c + 1) * LANES]) * scale).astype(BF16)
    nkv = nk // 2
    for c in range(nkv // LANES):
        _store_dup_heads(k_ref, c, rope(p[:, nq + c * LANES:nq + (c + 1) * LANES]))
        _store_dup_heads(v_ref, c, p[:, nq + nkv + c * LANES:nq + nkv + (c + 1) * LANES])


def _store_dup_heads(ref, c, pair):
    lo = lax.broadcasted_iota(I32, (1, LANES), 1) < (LANES // 2)
    swapped = pltpu.roll(pair, LANES // 2, 1)
    ref[:, (2 * c) * LANES:(2 * c + 1) * LANES] = jnp.where(lo, pair, swapped).astype(ref.dtype)
    ref[:, (2 * c + 1) * LANES:(2 * c + 2) * LANES] = jnp.where(lo, swapped, pair).astype(ref.dtype)


def _qkv_call(x, sc, sh, w_ext, b_ext, cos_t, sa_t, sb_t, n_q, n_kd, head_dim):
    bsz, s, d = x.shape
    tm = min(TOKEN_TILE, s)
    n_out = w_ext.shape[1]
    kern = functools.partial(_qkv_kernel, scale=head_dim ** -0.5)
    return pl.pallas_call(
        kern,
        out_shape=(jax.ShapeDtypeStruct((bsz, s, n_q), BF16),
                   jax.ShapeDtypeStruct((bsz, s, n_kd), BF16),
                   jax.ShapeDtypeStruct((bsz, s, n_kd), BF16)),
        grid=(bsz, s // tm),
        in_specs=[
            pl.BlockSpec((None, tm, d), lambda b, i: (b, i, 0)),
            pl.BlockSpec((None, 1, d), lambda b, i: (b, 0, 0)),
            pl.BlockSpec((None, 1, d), lambda b, i: (b, 0, 0)),
            _resident((d, n_out), lambda b, i: (0, 0)),
            pl.BlockSpec((1, n_out), lambda b, i: (0, 0)),
            pl.BlockSpec((tm, LANES), lambda b, i: (i, 0)),
            pl.BlockSpec((tm, LANES), lambda b, i: (i, 0)),
            pl.BlockSpec((tm, LANES), lambda b, i: (i, 0)),
        ],
        out_specs=(pl.BlockSpec((None, tm, n_q), lambda b, i: (b, i, 0)),
                   pl.BlockSpec((None, tm, n_kd), lambda b, i: (b, i, 0)),
                   pl.BlockSpec((None, tm, n_kd), lambda b, i: (b, i, 0))),
        compiler_params=_cparams("arbitrary", "arbitrary"),
        name="attn_qkv",
    )(x, sc, sh, w_ext, b_ext, cos_t, sa_t, sb_t)


def _ctx_kv_kernel(x_ref, sc_ref, sh_ref, w_ref, b_ref, k_ref, v_ref):
    u = (x_ref[...] * (1.0 + sc_ref[...]) + sh_ref[...]).astype(BF16)
    p = jnp.dot(u, w_ref[...], preferred_element_type=F32) + b_ref[...]
    nkv = k_ref.shape[1] // 2
    for c in range(nkv // LANES):
        _store_dup_heads(k_ref, c, p[:, c * LANES:(c + 1) * LANES])
        _store_dup_heads(v_ref, c, p[:, nkv + c * LANES:nkv + (c + 1) * LANES])


def _ctx_kv_call(ctx, csc, csh, w_kv, b_kv, n_kd):
    bsz, c, d = ctx.shape
    return pl.pallas_call(
        _ctx_kv_kernel,
        out_shape=(jax.ShapeDtypeStruct((bsz, c, n_kd), BF16), jax.ShapeDtypeStruct((bsz, c, n_kd), BF16)),
        grid=(bsz,),
        in_specs=[
            pl.BlockSpec((None, c, d), lambda b: (b, 0, 0)),
            pl.BlockSpec((1, d), lambda b: (0, 0)),
            pl.BlockSpec((1, d), lambda b: (0, 0)),
            pl.BlockSpec((d, n_kd), lambda b: (0, 0)),
            pl.BlockSpec((1, n_kd), lambda b: (0, 0)),
        ],
        out_specs=(pl.BlockSpec((None, c, n_kd), lambda b: (b, 0, 0)),
                   pl.BlockSpec((None, c, n_kd), lambda b: (b, 0, 0))),
        compiler_params=_cparams("arbitrary"),
        name="attn_ctx_kv",
    )(ctx, csc, csh, w_kv, b_kv)


def _attn_kernel(sink_ref, q_ref, kp_ref, kc_ref, kn_ref, vp_ref, vc_ref, vn_ref, kx_ref, vx_ref, o_ref,
                 *, n_kv, group, nb):
    n = pl.program_id(1)
    r = lax.broadcasted_iota(I32, (BLOCK, BLOCK), 0)
    j = lax.broadcasted_iota(I32, (BLOCK, BLOCK), 1)
    prev_ok = (j >= r) & (n > 0)
    next_ok = (j <= r) & (n < nb - 1)
    lo = lax.broadcasted_iota(I32, (1, LANES), 1) < (LANES // 2)
    pairs = group // 2
    for kh in range(n_kv):
        sl = slice(kh * LANES, (kh + 1) * LANES)
        kcat = jnp.concatenate([kp_ref[:, sl], kc_ref[:, sl], kn_ref[:, sl], kx_ref[:, sl]], axis=0)
        vcat = jnp.concatenate([vp_ref[:, sl], vc_ref[:, sl], vn_ref[:, sl], vx_ref[:, sl]], axis=0)
        nkeys = kcat.shape[0]
        parts = []
        for pp in range(pairs):
            q2 = q_ref[:, (kh * pairs + pp) * LANES:(kh * pairs + pp + 1) * LANES]
            zq = jnp.zeros_like(q2)
            parts += [jnp.where(lo, q2, zq), jnp.where(lo, zq, q2)]
        qs = jnp.concatenate(parts, axis=0)
        s_all = lax.dot_general(qs, kcat, (((1,), (1,)), ((), ())), preferred_element_type=F32)
        e_parts, rdens = [], []
        for g in range(group):
            s = s_all[g * BLOCK:(g + 1) * BLOCK]
            s = jnp.concatenate([
                jnp.where(prev_ok, s[:, :BLOCK], NEG_INF),
                s[:, BLOCK:2 * BLOCK],
                jnp.where(next_ok, s[:, 2 * BLOCK:3 * BLOCK], NEG_INF),
                s[:, 3 * BLOCK:]], axis=1)
            sk = sink_ref[kh * group + g]
            m = jnp.maximum(jnp.max(s, axis=1, keepdims=True), sk)
            e = jnp.exp(s - m)
            rdens.append(1.0 / (jnp.sum(e, axis=1, keepdims=True) + jnp.exp(sk - m)))
            e_parts.append(e.astype(BF16))
        o = jnp.dot(jnp.concatenate(e_parts, axis=0), vcat, preferred_element_type=F32)
        for pp in range(pairs):
            p = kh * pairs + pp
            o_lo = o[(2 * pp) * BLOCK:(2 * pp + 1) * BLOCK] * rdens[2 * pp]
            o_hi = o[(2 * pp + 1) * BLOCK:(2 * pp + 2) * BLOCK] * rdens[2 * pp + 1]
            o_ref[:, p * LANES:(p + 1) * LANES] = jnp.where(lo, o_lo, o_hi).astype(BF16)


def _attn_call(sink, q, kd, vd, kxd, vxd, n_kv, group):
    bsz, s, n_q = q.shape
    n_kd = kd.shape[2]
    c = kxd.shape[1]
    nb = s // BLOCK
    kern = functools.partial(_attn_kernel, n_kv=n_kv, group=group, nb=nb)
    prev = lambda b, n: (b, jnp.maximum(n - 1, 0), 0)
    cur = lambda b, n: (b, n, 0)
    nxt = lambda b, n: (b, jnp.minimum(n + 1, nb - 1), 0)
    kv = lambda im: pl.BlockSpec((None, BLOCK, n_kd), im)
    return pl.pallas_call(
        kern,
        out_shape=jax.ShapeDtypeStruct((bsz, s, n_q), BF16),
        grid=(bsz, nb),
        in_specs=[
            pl.BlockSpec(memory_space=pltpu.SMEM),
            pl.BlockSpec((None, BLOCK, n_q), cur),
            kv(prev), kv(cur), kv(nxt), kv(prev), kv(cur), kv(nxt),
            pl.BlockSpec((None, c, n_kd), lambda b, n: (b, 0, 0)),
            pl.BlockSpec((None, c, n_kd), lambda b, n: (b, 0, 0)),
        ],
        out_specs=pl.BlockSpec((None, BLOCK, n_q), cur),
        compiler_params=_cparams("arbitrary", "arbitrary"),
        name="window_attn",
    )(sink, q, kd, kd, kd, vd, vd, vd, kxd, vxd)


def _layer_norm(r, g, b):
    mu = jnp.mean(r, axis=-1, keepdims=True)
    xc = r - mu
    var = jnp.mean(xc * xc, axis=-1, keepdims=True)
    return xc * lax.rsqrt(var + LN_EPS) * g + b


def _route(logits_t, rb):
    s = jax.nn.sigmoid(logits_t)
    sel = s + rb
    n_e = N_GROUPS * EXPERTS_PER_GROUP
    sel_r = [sel[e:e + 1, :] for e in range(n_e)]
    s_r = [s[e:e + 1, :] for e in range(n_e)]
    gscore = []
    for g in range(N_GROUPS):
        a, b, c, d = sel_r[4 * g:4 * g + 4]
        m1, n1, m2, n2 = jnp.maximum(a, b), jnp.minimum(a, b), jnp.maximum(c, d), jnp.minimum(c, d)
        gscore.append(jnp.maximum(m1, m2) + jnp.maximum(jnp.minimum(m1, m2), jnp.maximum(n1, n2)))
    best, gi = gscore[0], jnp.zeros_like(gscore[0], dtype=I32)
    for g in range(1, N_GROUPS):
        upd = gscore[g] > best
        gi = jnp.where(upd, g, gi)
        best = jnp.where(upd, gscore[g], best)

    def pick(rows, i):
        out = rows[i]
        for g in range(1, N_GROUPS):
            out = jnp.where(gi == g, rows[4 * g + i], out)
        return out

    v = [pick(sel_r, i) for i in range(EXPERTS_PER_GROUP)]
    sv = [pick(s_r, i) for i in range(EXPERTS_PER_GROUP)]

    def argmax4(vals):
        bv, bi = vals[0], jnp.zeros_like(gi)
        for i in range(1, EXPERTS_PER_GROUP):
            upd = vals[i] > bv
            bi = jnp.where(upd, i, bi)
            bv = jnp.where(upd, vals[i], bv)
        return bi

    def take4(vals, idx):
        out = vals[0]
        for i in range(1, EXPERTS_PER_GROUP):
            out = jnp.where(idx == i, vals[i], out)
        return out

    i1 = argmax4(v)
    i2 = argmax4([jnp.where(i1 == i, -jnp.inf, v[i]) for i in range(EXPERTS_PER_GROUP)])
    s1, s2 = take4(sv, i1), take4(sv, i2)
    tot = s1 + s2
    g1, g2 = s1 / tot, s2 / tot
    first_lo = i1 < i2
    i_lo, i_hi = jnp.minimum(i1, i2), jnp.maximum(i1, i2)
    g_lo, g_hi = jnp.where(first_lo, g1, g2), jnp.where(first_lo, g2, g1)
    pair = jnp.where(i_lo == 0, i_hi - 1, jnp.where(i_lo == 1, i_hi + 1, N_PAIRS - 1))
    bucket = gi * N_PAIRS + pair
    return 4 * gi + i_lo, 4 * gi + i_hi, g_lo, g_hi, bucket


def _proj_ln_kernel(a_ref, w_ref, bias_ref, h_ref, gate_ref, lng_ref, lnb_ref, sc_ref, sh_ref, rw_ref, rb_ref,
                    h1_ref, xp_ref, meta_ref, cnt_ref, run_ref, *, alpha, a_packed):
    first = (pl.program_id(0) == 0) & (pl.program_id(1) == 0)

    @pl.when(first)
    def _():
        run_ref[...] = jnp.zeros_like(run_ref)

    tm, d = h_ref.shape
    half = d // 2
    chunk = tm // PROJ_CHUNKS
    logit_parts = []
    ys = []
    for c in range(PROJ_CHUNKS):
        rs = slice(c * chunk, (c + 1) * chunk)
        a = _unpack_cols(a_ref[rs]).astype(BF16) if a_packed else a_ref[rs]
        ys.append(jnp.dot(a, w_ref[...], preferred_element_type=F32) + bias_ref[...])
    for c in range(PROJ_CHUNKS):
        rs = slice(c * chunk, (c + 1) * chunk)
        y = ys[c]
        h1 = _layer_norm(alpha * h_ref[rs] + gate_ref[...] * y, lng_ref[...], lnb_ref[...])
        h1_ref[rs] = h1
        tb = (h1 * (1.0 + sc_ref[...]) + sh_ref[...]).astype(BF16)
        bits = lax.bitcast_convert_type(tb.astype(F32), U32)
        xp_ref[rs, :half] = (bits[:, :half] & jnp.uint32(0xFFFF0000)) | (bits[:, half:] >> 16)
        logits = jnp.dot(tb, rw_ref[...], preferred_element_type=F32)
        logit_parts.append(jnp.transpose(logits)[:N_GROUPS * EXPERTS_PER_GROUP, :])
    logits_t = jnp.concatenate(logit_parts, axis=1)
    e_lo, e_hi, g_lo, g_hi, bucket = _route(logits_t, rb_ref[...])

    rows = lax.broadcasted_iota(I32, (BUCKET_ROWS, tm), 0)
    onehot = (rows == bucket).astype(F32)
    tri = (lax.broadcasted_iota(I32, (tm, tm), 0) <= lax.broadcasted_iota(I32, (tm, tm), 1)).astype(BF16)
    cum = jnp.dot(onehot.astype(BF16), tri, preferred_element_type=F32)
    run = run_ref[:, 0:1]
    rank = jnp.sum(onehot * (cum - 1.0 + run), axis=0, keepdims=True)
    new_run = run + cum[:, tm - 1:tm]
    run_ref[...] = jnp.broadcast_to(new_run, run_ref.shape)
    cnt_ref[...] = jnp.broadcast_to(new_run, cnt_ref.shape)

    mrow = lax.broadcasted_iota(I32, (8, tm), 0)
    meta = jnp.where(mrow == 0, e_lo.astype(F32), 0.0)
    meta = jnp.where(mrow == 1, e_hi.astype(F32), meta)
    meta = jnp.where(mrow == 2, g_lo, meta)
    meta = jnp.where(mrow == 3, g_hi, meta)
    meta = jnp.where(mrow == 4, bucket.astype(F32), meta)
    meta = jnp.where(mrow == 5, rank, meta)
    meta_ref[...] = meta

    grow = lax.broadcasted_iota(I32, (LANES, tm), 0)
    gates_t = jnp.where(grow == 0, g_lo, jnp.where(grow == 1, g_hi, 0.0))
    xp_ref[:, half:] = lax.bitcast_convert_type(jnp.transpose(gates_t), U32)


def _proj_ln_call(a, w, bias, h, gate, lng, lnb, sc, sh, rw_pad, rb, alpha):
    bsz, s, d = h.shape
    a_packed = a.dtype == U32
    da = a.shape[2]
    dm = w.shape[0]
    tm = min(TOKEN_TILE, s)
    t = bsz * s
    nt = s // tm
    dp = d // 2 + LANES
    kern = functools.partial(_proj_ln_kernel, alpha=alpha, a_packed=a_packed)
    vec = lambda: pl.BlockSpec((1, d), lambda b, i: (0, 0))
    bvec = lambda: pl.BlockSpec((None, 1, d), lambda b, i: (b, 0, 0))
    return pl.pallas_call(
        kern,
        out_shape=(jax.ShapeDtypeStruct((bsz, s, d), F32),
                   jax.ShapeDtypeStruct((t, dp), U32),
                   jax.ShapeDtypeStruct((8, t), F32),
                   jax.ShapeDtypeStruct((BUCKET_ROWS, LANES), F32)),
        grid=(bsz, nt),
        in_specs=[
            pl.BlockSpec((None, tm, da), lambda b, i: (b, i, 0)),
            _resident((dm, d), lambda b, i: (0, 0)),
            vec(),
            pl.BlockSpec((None, tm, d), lambda b, i: (b, i, 0)),
            bvec(), vec(), vec(), bvec(), bvec(),
            pl.BlockSpec((d, LANES), lambda b, i: (0, 0)),
            pl.BlockSpec((N_GROUPS * EXPERTS_PER_GROUP, 1), lambda b, i: (0, 0)),
        ],
        out_specs=(pl.BlockSpec((None, tm, d), lambda b, i: (b, i, 0)),
                   pl.BlockSpec((tm, dp), lambda b, i: (b * nt + i, 0)),
                   pl.BlockSpec((8, tm), lambda b, i: (0, b * nt + i)),
                   pl.BlockSpec((BUCKET_ROWS, LANES), lambda b, i: (0, 0))),
        scratch_shapes=[pltpu.VMEM((BUCKET_ROWS, LANES), F32)],
        compiler_params=_cparams("arbitrary", "arbitrary"),
        name="proj_ln_route",
    )(a, w, bias, h, gate, lng, lnb, sc, sh, rw_pad, rb)


def _wait_tile(src_tile, dst_tile, sem):
    pltpu.make_async_copy(src_tile, dst_tile, sem).wait()


def _scatter_kernel(dest_ref, xp_ref, init_ref, xs_ref, sem):
    del init_ref
    tm = xp_ref.shape[0]
    base = pl.program_id(0) * tm

    def issue(i, c):
        for p in range(DMA_QUEUES):
            r = i * DMA_QUEUES + p
            pltpu.make_async_copy(xp_ref.at[pl.ds(r, 1)], xs_ref.at[pl.ds(dest_ref[base + r], 1)],
                                  sem).start(priority=p)
        return c

    lax.fori_loop(0, tm // DMA_QUEUES, issue, 0, unroll=4)
    _wait_tile(xp_ref, xs_ref.at[pl.ds(0, tm)], sem)


def _scatter_call(dest, xp, rows_out):
    t, dp = xp.shape
    tm = min(TOKEN_TILE, t)
    init = jnp.zeros((rows_out, dp), U32)
    return pl.pallas_call(
        _scatter_kernel,
        out_shape=jax.ShapeDtypeStruct((rows_out, dp), U32),
        grid_spec=pltpu.PrefetchScalarGridSpec(
            num_scalar_prefetch=1,
            grid=(t // tm,),
            in_specs=[pl.BlockSpec((tm, dp), lambda i, dest: (i, 0)),
                      pl.BlockSpec(memory_space=pl.ANY)],
            out_specs=pl.BlockSpec(memory_space=pl.ANY),
            scratch_shapes=[pltpu.SemaphoreType.DMA],
        ),
        input_output_aliases={2: 0},
        compiler_params=_cparams("arbitrary"),
        name="moe_scatter",
    )(dest, xp, init)


def _moe_kernel(ex_ref, valid_ref, xs_ref, w1_ref, w3_ref, w2_ref, y_ref, acc_ref):
    del ex_ref
    i = pl.program_id(0)
    s = pl.program_id(1)
    half = xs_ref.shape[1] - LANES
    ok = valid_ref[i] > 0

    @pl.when((i == 0) & (s == 0))
    def _():
        acc_ref[...] = jnp.zeros_like(acc_ref)

    @pl.when(ok)
    def _():
        w = xs_ref[:, :half]
        hi = lax.bitcast_convert_type(w & jnp.uint32(0xFFFF0000), F32).astype(BF16)
        lo = lax.bitcast_convert_type(w << 16, F32).astype(BF16)
        x = jnp.concatenate([hi, lo], axis=1)
        gates = lax.bitcast_convert_type(xs_ref[:, half:], F32)
        which = (s + i) % 2
        g = jnp.where(which == 0, gates[:, 0:1], gates[:, 1:2])
        a = jnp.dot(x, w1_ref[...], preferred_element_type=F32)
        b = jnp.dot(x, w3_ref[...], preferred_element_type=F32)
        hid = (_silu(a) * b * g).astype(BF16)
        y = jnp.dot(hid, w2_ref[...].astype(BF16), preferred_element_type=F32)
        total = jnp.where(s == 0, 0.0, acc_ref[...]) + y
        acc_ref[...] = total
        y_ref[...] = _pack_cols(total)

    @pl.when(jnp.logical_not(ok) & (s == 0))
    def _():
        y_ref[...] = jnp.zeros_like(y_ref)


def _moe_call(ex, valid, xs, w1, w3, w2, layer):
    rows, dp = xs.shape
    _, n_e, d, ff = w1.shape
    tm = min(TOKEN_TILE, rows)
    ntiles = rows // tm
    return pl.pallas_call(
        _moe_kernel,
        out_shape=jax.ShapeDtypeStruct((rows, d // 2), U32),
        grid_spec=pltpu.PrefetchScalarGridSpec(
            num_scalar_prefetch=2,
            grid=(ntiles, 2),
            in_specs=[
                pl.BlockSpec((tm, dp), lambda i, s, ex, va: (i, 0)),
                pl.BlockSpec((None, None, d, ff), lambda i, s, ex, va: (layer, ex[2 * i + s], 0, 0)),
                pl.BlockSpec((None, None, d, ff), lambda i, s, ex, va: (layer, ex[2 * i + s], 0, 0)),
                pl.BlockSpec((None, None, ff, d), lambda i, s, ex, va: (layer, ex[2 * i + s], 0, 0)),
            ],
            out_specs=pl.BlockSpec((tm, d // 2), lambda i, s, ex, va: (i, 0)),
            scratch_shapes=[pltpu.VMEM((tm, d), F32)],
        ),
        compiler_params=_cparams("arbitrary", "arbitrary"),
        name="moe_experts",
    )(ex, valid, xs, w1, w3, w2)


GATHER_CHUNKS = 8


def _gather_ln_kernel(dest_ref, ys_ref, h_ref, gate_ref, lng_ref, lnb_ref, o_ref, buf, sem, *, alpha, nt):
    tm = h_ref.shape[0]
    step = pl.program_id(0) * nt + pl.program_id(1)
    last = pl.num_programs(0) * nt - 1
    slot = step % 2

    def row_copy(tile, sl, r):
        return pltpu.make_async_copy(ys_ref.at[pl.ds(dest_ref[tile * tm + r], 1)], buf.at[sl, pl.ds(r, 1)],
                                     sem.at[sl])

    def wait_tile(sl):
        _wait_tile(ys_ref.at[pl.ds(0, tm)], buf.at[sl], sem.at[sl])

    @pl.when(step == 0)
    def _():
        def issue(i, c):
            for p in range(DMA_QUEUES):
                row_copy(0, 0, i * DMA_QUEUES + p).start(priority=p)
            return c

        lax.fori_loop(0, tm // DMA_QUEUES, issue, 0, unroll=4)

    wait_tile(slot)
    nxt = jnp.minimum(step + 1, last)
    rows = tm // GATHER_CHUNKS
    for c in range(GATHER_CHUNKS):
        for r in range(c * rows, (c + 1) * rows):
            row_copy(nxt, 1 - slot, r).start(priority=r % DMA_QUEUES)
        sl = slice(c * rows, (c + 1) * rows)
        f = _unpack_cols(buf[slot, sl])
        o_ref[sl] = _layer_norm(alpha * h_ref[sl] + gate_ref[...] * f, lng_ref[...], lnb_ref[...])

    @pl.when(step == last)
    def _():
        wait_tile(1 - slot)


def _gather_ln_call(dest, ys, h, gate, lng, lnb, alpha):
    bsz, s, d = h.shape
    tm = min(TOKEN_TILE, s)
    nt = s // tm
    kern = functools.partial(_gather_ln_kernel, alpha=alpha, nt=nt)
    return pl.pallas_call(
        kern,
        out_shape=jax.ShapeDtypeStruct((bsz, s, d), F32),
        grid_spec=pltpu.PrefetchScalarGridSpec(
            num_scalar_prefetch=1,
            grid=(bsz, nt),
            in_specs=[
                pl.BlockSpec(memory_space=pl.ANY),
                pl.BlockSpec((None, tm, d), lambda b, i, dest: (b, i, 0)),
                pl.BlockSpec((None, 1, d), lambda b, i, dest: (b, 0, 0)),
                pl.BlockSpec((1, d), lambda b, i, dest: (0, 0)),
                pl.BlockSpec((1, d), lambda b, i, dest: (0, 0)),
            ],
            out_specs=pl.BlockSpec((None, tm, d), lambda b, i, dest: (b, i, 0)),
            scratch_shapes=[pltpu.VMEM((2, tm, d // 2), U32), pltpu.SemaphoreType.DMA((2,))],
        ),
        compiler_params=_cparams("arbitrary", "arbitrary"),
        name="moe_gather_ln",
    )(dest, ys, h, gate, lng, lnb)


_PAIR_LO = (0, 0, 0, 1, 1, 2)
_PAIR_HI = (1, 2, 3, 2, 3, 3)


def _moe_layer(xp, meta, cnt, h1, gate2, lng, lnb, w1, w3, w2, layer, alpha):
    t = xp.shape[0]
    tm = min(TOKEN_TILE, t)
    ntiles = t // tm + N_BUCKETS
    counts = cnt[:N_BUCKETS, 0].astype(I32)
    tiles_b = (counts + tm - 1) // tm
    tile_end = jnp.cumsum(tiles_b)
    offs = (tile_end - tiles_b) * tm
    bucket = meta[4].astype(I32)
    dest = offs[bucket] + meta[5].astype(I32)

    tile = jnp.arange(ntiles, dtype=I32)
    valid = (tile < tile_end[-1]).astype(I32)
    tb = jnp.minimum(jnp.sum((tile[:, None] >= tile_end[None, :]).astype(I32), axis=1), N_BUCKETS - 1)
    lo = jnp.asarray(_PAIR_LO, I32)[tb % N_PAIRS] + EXPERTS_PER_GROUP * (tb // N_PAIRS)
    hi = jnp.asarray(_PAIR_HI, I32)[tb % N_PAIRS] + EXPERTS_PER_GROUP * (tb // N_PAIRS)
    odd = (tile % 2) == 1
    ex = jnp.stack([jnp.where(odd, hi, lo), jnp.where(odd, lo, hi)], axis=1).reshape(-1)

    xs = _scatter_call(dest, xp, ntiles * tm)
    ys = _moe_call(ex, valid, xs, w1, w3, w2, layer)
    return _gather_ln_call(dest, ys, h1, gate2, lng, lnb, alpha)


def _hy_in_kernel(xm_ref, xp_ref, xn_ref, sc_ref, sh_ref, w_ref, b_ref, cw_ref, cb_ref, x0_ref, z_ref, *, nt, tn):
    i = pl.program_id(1)
    tm, d = xm_ref.shape
    sc, sh = 1.0 + sc_ref[...], sh_ref[...]
    u = jnp.concatenate([xp_ref[...] * sc + sh, xm_ref[...] * sc + sh, xn_ref[...] * sc + sh], axis=0).astype(BF16)
    rows = tm + 16
    rid = lax.broadcasted_iota(I32, (rows, 1), 0)
    keep = ((rid >= 8) | (i > 0)) & ((rid < tm + 8) | (i < nt - 1))

    def conv(sec, j):
        col = sec * d + j * tn
        p = jnp.dot(u, w_ref[:, col:col + tn], preferred_element_type=F32) + b_ref[:, col:col + tn]
        p = jnp.where(keep, p, 0.0)
        cw = cw_ref[:, col:col + tn]
        out = (cw[0:1] * pltpu.roll(p, 1, 0) + cw[1:2] * p + cw[2:3] * pltpu.roll(p, rows - 1, 0)
               + cb_ref[:, col:col + tn])
        return out[8:8 + tm]

    nch = d // tn
    for j in range(nch // 2):
        x0_ref[:, j * tn:(j + 1) * tn] = _pack_pair(conv(0, j), conv(0, j + nch // 2))
        z_ref[:, j * tn:(j + 1) * tn] = _pack_pair(conv(1, j) * conv(2, j),
                                                   conv(1, j + nch // 2) * conv(2, j + nch // 2))


def _hy_in_call(h, sc, sh, w, b, cw, cb):
    bsz, s, d = h.shape
    tm = min(TOKEN_TILE, s)
    nt = s // tm
    hb = tm // 8
    tn = 1024
    kern = functools.partial(_hy_in_kernel, nt=nt, tn=tn)
    return pl.pallas_call(
        kern,
        out_shape=(jax.ShapeDtypeStruct((bsz, s, d // 2), U32), jax.ShapeDtypeStruct((bsz, s, d // 2), U32)),
        grid=(bsz, nt),
        in_specs=[
            pl.BlockSpec((None, tm, d), lambda b, i: (b, i, 0)),
            pl.BlockSpec((None, 8, d), lambda b, i: (b, jnp.maximum(i * hb - 1, 0), 0)),
            pl.BlockSpec((None, 8, d), lambda b, i: (b, jnp.minimum((i + 1) * hb, nt * hb - 1), 0)),
            pl.BlockSpec((None, 1, d), lambda b, i: (b, 0, 0)),
            pl.BlockSpec((None, 1, d), lambda b, i: (b, 0, 0)),
            _resident((d, 3 * d), lambda b, i: (0, 0)),
            pl.BlockSpec((1, 3 * d), lambda b, i: (0, 0)),
            pl.BlockSpec((3, 3 * d), lambda b, i: (0, 0)),
            pl.BlockSpec((1, 3 * d), lambda b, i: (0, 0)),
        ],
        out_specs=(pl.BlockSpec((None, tm, d // 2), lambda b, i: (b, i, 0)),
                   pl.BlockSpec((None, tm, d // 2), lambda b, i: (b, i, 0))),
        compiler_params=_cparams("arbitrary", "arbitrary"),
        name="hyena_in",
    )(h, h, h, sc, sh, w, b, cw, cb)


def _filter_kernel(w1_ref, b1_ref, w2_ref, b2_ref, w3_ref, b3_ref, fr_ref, w4_ref, dl_ref, f_ref, l1_ref,
                   *, seq, tn):
    i = pl.program_id(0)

    @pl.when(i == 0)
    def _():
        l1_ref[...] = jnp.zeros_like(l1_ref)

    n_lane = i * tn + lax.broadcasted_iota(I32, (1, tn), 1)
    m_lane = jnp.where(n_lane < seq, n_lane, 2 * seq - n_lane).astype(F32)
    t_lane = m_lane / (seq - 1.0)
    wl = (2.0 * math.pi) * m_lane / float(seq)
    band = lax.broadcasted_iota(I32, (HY_BANDS, 1), 0).astype(F32)
    fb = 1e-4 + band * ((HY_BANDS - 1 - 1e-4) / (HY_BANDS - 1))
    ang = fb * wl
    z = jnp.concatenate([t_lane, jnp.cos(ang), -jnp.sin(ang),
                         jnp.zeros((7, tn), F32)], axis=0)
    fr = fr_ref[...]

    def layer(w_ref, b_ref, x):
        pre = lax.dot_general(w_ref[...], x, (((0,), (0,)), ((), ())), preferred_element_type=F32,
                              precision=HIGHEST)
        return jnp.sin(fr * (pre + b_ref[...]))

    hdn = layer(w1_ref, b1_ref, z)
    hdn = layer(w2_ref, b2_ref, hdn)
    hdn = layer(w3_ref, b3_ref, hdn)
    h = lax.dot_general(hdn.astype(BF16), w4_ref[...].astype(BF16), (((0,), (0,)), ((), ())),
                        preferred_element_type=F32)
    n_col = i * tn + lax.broadcasted_iota(I32, (tn, 1), 0)
    m_col = jnp.where(n_col < seq, n_col, 2 * seq - n_col).astype(F32)
    h = h * jnp.exp(-(m_col / (seq - 1.0)) * dl_ref[...])
    h = jnp.where(n_col == seq, 0.0, h)
    l1_ref[...] += jnp.sum(jnp.abs(h), axis=0, keepdims=True)
    f_ref[...] = _pack_cols(h)


def _filter_call(fw1, fb1, fw2, fb2, fw3, fb3, freq, fw4, seq):
    width = fw2.shape[0]
    d = fw4.shape[1] // 2
    tn = min(512, seq)
    steps = 2 * seq // tn
    max_decay = math.log(HY_TARGET) / HY_FAST_DECAY
    min_decay = math.log(HY_TARGET) / HY_SLOW_DECAY
    deltas = jnp.abs(jnp.linspace(min_decay, max_decay, d, dtype=F32)).reshape(1, d)
    w1p = jnp.concatenate([fw1, jnp.zeros((7, width), F32)], axis=0)
    col = lambda a: a.reshape(width, 1)
    small = lambda shp: pl.BlockSpec(shp, lambda i: (0, 0))
    kern = functools.partial(_filter_kernel, seq=seq, tn=tn)
    return pl.pallas_call(
        kern,
        out_shape=(jax.ShapeDtypeStruct((2 * seq, d // 2), U32), jax.ShapeDtypeStruct((1, d), F32)),
        grid=(steps,),
        in_specs=[small((40, width)), small((width, 1)), small((width, width)), small((width, 1)),
                  small((width, width)), small((width, 1)), small((width, 1)),
                  pl.BlockSpec((width, d), lambda i: (0, (i * tn) // seq)),
                  small((1, d))],
        out_specs=(pl.BlockSpec((tn, d // 2), lambda i: (i, 0)), small((1, d))),
        compiler_params=_cparams("arbitrary"),
        name="hyena_filter",
    )(w1p, col(fb1), fw2, col(fb2), fw3, col(fb3), col(freq), fw4, deltas)


def _dft_tables(n1, n2):
    n = n1 * n2
    k = np.arange(n1)[:, None]
    m = np.arange(n1)[None, :]
    ang1 = -2.0 * np.pi * ((k * m) % n1) / n1
    f1r, f1i = np.cos(ang1), np.sin(ang1)
    hn = n1 // 2
    a_data = np.block([[f1r[:, :hn], -f1i[:, :hn]], [f1i[:, :hn], f1r[:, :hn]]])
    a_filt = np.concatenate([f1r, f1i], axis=0)
    a_inv = np.block([[f1r.T[:hn], f1i.T[:hn]], [-f1i.T[:hn], f1r.T[:hn]]]) / n
    k2 = np.arange(n2)[:, None]
    m2 = np.arange(n2)[None, :]
    ang2 = -2.0 * np.pi * ((k2 * m2) % n2) / n2
    f2r, f2i = np.cos(ang2), np.sin(ang2)
    b_fwd = np.block([[f2r, -f2i], [f2i, f2r]])
    b_inv = np.block([[f2r, f2i], [-f2i, f2r]])
    angt = -2.0 * np.pi * ((np.arange(n2)[:, None] * np.arange(n1)[None, :]) % n) / n
    tw = (np.cos(angt), np.sin(angt))
    perm = (np.arange(n1 // FFT_K1_GROUP)[None, :] * FFT_K1_GROUP + np.arange(FFT_K1_GROUP)[:, None]).reshape(-1)
    perm2 = np.concatenate([perm, perm + n1])
    as_bf16 = lambda a: jnp.asarray(a, F32).astype(BF16)
    return dict(a_data=as_bf16(a_data[perm2]), a_filt=as_bf16(a_filt[perm2]), a_inv=as_bf16(a_inv[:, perm2]),
                b_fwd=as_bf16(b_fwd), b_inv=as_bf16(b_inv),
                tw_n2=tuple(jnp.asarray(t[:, perm], F32).reshape(n2, n1, 1) for t in tw),
                tw_k1=tuple(jnp.asarray(t.T.copy(), F32).reshape(n1, n2, 1) for t in tw))


FFT_K1_GROUP = 4


def _pack_pair(hi, lo):
    hb = lax.bitcast_convert_type(hi.astype(BF16).astype(F32), U32)
    lb = lax.bitcast_convert_type(lo.astype(BF16).astype(F32), U32)
    return (hb & jnp.uint32(0xFFFF0000)) | (lb >> 16)


def _unpack_pair(w):
    hi = lax.bitcast_convert_type(w & jnp.uint32(0xFFFF0000), F32)
    lo = lax.bitcast_convert_type(w << 16, F32)
    return hi, lo


def _fetch_step(view, buf, sem):
    k = pl.program_id(0)
    slot = k % 2

    def copy(step, sl):
        return pltpu.make_async_copy(view(step), buf.at[sl], sem.at[sl])

    @pl.when(k == 0)
    def _():
        copy(0, 0).start()

    @pl.when(k + 1 < pl.num_programs(0))
    def _():
        copy(k + 1, 1 - slot).start()

    copy(k, slot).wait()
    return buf.at[slot]


def _store_step(view, buf, sem, fill):
    k = pl.program_id(0)
    slot = k % 2

    def copy(step, sl):
        return pltpu.make_async_copy(buf.at[sl], view(step), sem.at[sl])

    @pl.when(k >= 2)
    def _():
        copy(k - 2, slot).wait()

    fill(buf.at[slot])
    copy(k, slot).start()

    @pl.when(k == pl.num_programs(0) - 1)
    def _():
        copy(k, slot).wait()

        @pl.when(k >= 1)
        def _():
            copy(k - 1, 1 - slot).wait()


def _unpack_cols(w):
    hi, lo = _unpack_pair(w)
    return jnp.concatenate([hi, lo], axis=1)


def _pack_cols(x):
    half = x.shape[1] // 2
    return _pack_pair(x[:, :half], x[:, half:])


def _fft_a_kernel(x_hbm, a_ref, twr_ref, twi_ref, y_ref, xbuf, sem):
    n1 = a_ref.shape[0] // 2
    x_ref = _fetch_step(lambda j: x_hbm.at[:, :, j, :], xbuf, sem)
    rhs = jnp.concatenate([_unpack_cols(x_ref[0]), _unpack_cols(x_ref[1])], axis=0).astype(BF16)
    y = jnp.dot(a_ref[...], rhs, preferred_element_type=F32)
    yr, yi = y[:n1], y[n1:]
    tr, ti = twr_ref[...], twi_ref[...]
    w = _pack_pair(yr * tr - yi * ti, yr * ti + yi * tr)
    q = n1 // FFT_K1_GROUP
    d = w.shape[1]
    for kk in range(FFT_K1_GROUP):
        y_ref[:, kk * d:(kk + 1) * d] = w[kk * q:(kk + 1) * q]


def _fft_a_call(x4, a_mat, tw_n2):
    _, hn, n2, half = x4.shape
    d = 2 * half
    n1 = 2 * hn
    q = n1 // FFT_K1_GROUP
    return pl.pallas_call(
        _fft_a_kernel,
        out_shape=jax.ShapeDtypeStruct((n2, q, FFT_K1_GROUP * d), U32),
        grid=(n2,),
        in_specs=[
            pl.BlockSpec(memory_space=pl.ANY),
            pl.BlockSpec((2 * n1, n1), lambda j: (0, 0)),
            pl.BlockSpec((None, n1, 1), lambda j: (j, 0, 0)),
            pl.BlockSpec((None, n1, 1), lambda j: (j, 0, 0)),
        ],
        out_specs=pl.BlockSpec((None, q, FFT_K1_GROUP * d), lambda j: (j, 0, 0)),
        scratch_shapes=[pltpu.VMEM((2, 2, hn, half), U32), pltpu.SemaphoreType.DMA((2,))],
        compiler_params=_cparams("arbitrary"),
        name="fft_stage_a",
    )(x4, a_mat, tw_n2[0], tw_n2[1])


def _load_k1(y_ref, kk):
    d = y_ref.shape[1] // FFT_K1_GROUP
    yr, yi = _unpack_pair(y_ref[:, kk * d:(kk + 1) * d])
    return jnp.concatenate([yr, yi], axis=0).astype(BF16)


def _fft_b_kernel(y_hbm, b_ref, h_ref, ybuf, sem):
    y_ref = _fetch_step(lambda kb: y_hbm.at[:, kb, :], ybuf, sem)
    n2 = y_ref.shape[0]
    for kk in range(FFT_K1_GROUP):
        x = jnp.dot(b_ref[...], _load_k1(y_ref, kk), preferred_element_type=F32)
        h_ref[kk] = _pack_pair(x[:n2], x[n2:])


def _fft_b_call(y, b_fwd):
    n2, q, gd = y.shape
    d = gd // FFT_K1_GROUP
    return pl.pallas_call(
        _fft_b_kernel,
        out_shape=jax.ShapeDtypeStruct((q * FFT_K1_GROUP, n2, d), U32),
        grid=(q,),
        in_specs=[pl.BlockSpec(memory_space=pl.ANY),
                  pl.BlockSpec((2 * n2, 2 * n2), lambda k: (0, 0))],
        out_specs=pl.BlockSpec((FFT_K1_GROUP, n2, d), lambda k: (k, 0, 0)),
        scratch_shapes=[pltpu.VMEM((2, n2, gd), U32), pltpu.SemaphoreType.DMA((2,))],
        compiler_params=_cparams("arbitrary"),
        name="fft_filter_b",
    )(y, b_fwd)


def _fft_bc_kernel(y_hbm, h_ref, bf_ref, bi_ref, twr_ref, twi_ref, g_hbm, ybuf, gbuf, sem_in, sem_out):
    y_ref = _fetch_step(lambda kb: y_hbm.at[:, kb, :], ybuf, sem_in)
    n2 = y_ref.shape[0]
    d = y_ref.shape[1] // FFT_K1_GROUP

    def fill(g_ref):
        for kk in range(FFT_K1_GROUP):
            x = jnp.dot(bf_ref[...], _load_k1(y_ref, kk), preferred_element_type=F32)
            xr, xi = x[:n2], x[n2:]
            hr, hi = _unpack_pair(h_ref[kk])
            z = jnp.concatenate([xr * hr - xi * hi, xr * hi + xi * hr], axis=0).astype(BF16)
            g = jnp.dot(bi_ref[...], z, preferred_element_type=F32)
            gr, gi = g[:n2], g[n2:]
            tr, ti = twr_ref[kk], twi_ref[kk]
            g_ref[:, kk * d:(kk + 1) * d] = _pack_pair(gr * tr + gi * ti, gi * tr - gr * ti)

    _store_step(lambda kb: g_hbm.at[:, kb, :], gbuf, sem_out, fill)


def _fft_bc_call(y, hspec, b_fwd, b_inv, tw_k1):
    n2, q, gd = y.shape
    d = gd // FFT_K1_GROUP
    mat = lambda: pl.BlockSpec((2 * n2, 2 * n2), lambda k: (0, 0))
    tw = lambda: pl.BlockSpec((FFT_K1_GROUP, n2, 1), lambda k: (k, 0, 0))
    return pl.pallas_call(
        _fft_bc_kernel,
        out_shape=jax.ShapeDtypeStruct((n2, q, gd), U32),
        grid=(q,),
        in_specs=[pl.BlockSpec(memory_space=pl.ANY),
                  pl.BlockSpec((FFT_K1_GROUP, n2, d), lambda k: (k, 0, 0)), mat(), mat(), tw(), tw()],
        out_specs=pl.BlockSpec(memory_space=pl.ANY),
        scratch_shapes=[pltpu.VMEM((2, n2, gd), U32), pltpu.VMEM((2, n2, gd), U32),
                        pltpu.SemaphoreType.DMA((2,)), pltpu.SemaphoreType.DMA((2,))],
        compiler_params=_cparams("arbitrary"),
        name="fft_stage_bc",
    )(y, hspec, b_fwd, b_inv, tw_k1[0], tw_k1[1])


def _fft_d_kernel(g_ref, a_ref, x0_hbm, z_hbm, l1_ref, fb_ref, o_hbm, x0buf, zbuf, obuf, sem_x, sem_z, sem_o):
    hn = a_ref.shape[0] // 2
    d = g_ref.shape[1] // FFT_K1_GROUP
    x0_ref = _fetch_step(lambda j: x0_hbm.at[:, :, j, :], x0buf, sem_x)
    z_ref = _fetch_step(lambda j: z_hbm.at[:, :, j, :], zbuf, sem_z)
    parts = [_unpack_pair(g_ref[:, kk * d:(kk + 1) * d]) for kk in range(FFT_K1_GROUP)]
    rhs = jnp.concatenate([p[0] for p in parts] + [p[1] for p in parts], axis=0).astype(BF16)
    y = jnp.dot(a_ref[...], rhs, preferred_element_type=F32)
    inv_l1 = 1.0 / l1_ref[...]
    fb = fb_ref[...]

    def fill(o_ref):
        for b in range(2):
            conv = y[b * hn:(b + 1) * hn] * inv_l1
            o_ref[b] = _pack_cols(_unpack_cols(x0_ref[b]) * (conv + fb * _unpack_cols(z_ref[b])))

    _store_step(lambda j: o_hbm.at[:, :, j, :], obuf, sem_o, fill)


def _fft_d_call(g, a_inv, x0_4, z_4, l1, fbias):
    n2, q, gd = g.shape
    d = gd // FFT_K1_GROUP
    n1 = q * FFT_K1_GROUP
    hn = n1 // 2
    half = d // 2
    tok_buf = lambda: pltpu.VMEM((2, 2, hn, half), U32)
    return pl.pallas_call(
        _fft_d_kernel,
        out_shape=jax.ShapeDtypeStruct((2, hn, n2, half), U32),
        grid=(n2,),
        in_specs=[pl.BlockSpec((None, q, gd), lambda j: (j, 0, 0)),
                  pl.BlockSpec((n1, 2 * n1), lambda j: (0, 0)),
                  pl.BlockSpec(memory_space=pl.ANY), pl.BlockSpec(memory_space=pl.ANY),
                  pl.BlockSpec((1, d), lambda j: (0, 0)),
                  pl.BlockSpec((1, d), lambda j: (0, 0))],
        out_specs=pl.BlockSpec(memory_space=pl.ANY),
        scratch_shapes=[tok_buf(), tok_buf(), tok_buf(),
                        pltpu.SemaphoreType.DMA((2,)), pltpu.SemaphoreType.DMA((2,)), pltpu.SemaphoreType.DMA((2,))],
        compiler_params=_cparams("arbitrary"),
        name="fft_stage_d",
    )(g, a_inv, x0_4, z_4, l1, fbias)


def _hyena_conv(x0p, zp, filtp, l1, fbias):
    bsz, seq, half = zp.shape
    assert bsz == 2, "the two batch rows are packed as one complex signal"
    n2 = FFT_N2
    n1 = 2 * seq // n2
    tabs = _dft_tables(n1, n2)
    hn = n1 // 2
    view = lambda a: a.reshape(2, hn, n2, half)
    hspec = _fft_b_call(_fft_a_call(view(filtp), tabs["a_filt"], tabs["tw_n2"]), tabs["b_fwd"])
    y = _fft_a_call(view(zp), tabs["a_data"], tabs["tw_n2"])
    g = _fft_bc_call(y, hspec, tabs["b_fwd"], tabs["b_inv"], tabs["tw_k1"])
    out = _fft_d_call(g, tabs["a_inv"], view(x0p), view(zp), l1, fbias)
    return out.reshape(bsz, seq, half)


def _rope_tables(seq, head_dim):
    axis_dim = head_dim // 2
    rows = seq // GRID_W
    inv = ROPE_BASE ** (-jnp.arange(0, axis_dim, 2, dtype=F32) / axis_dim)
    row = jnp.repeat(jnp.arange(rows, dtype=F32), GRID_W)[:, None] * inv
    col = jnp.tile(jnp.arange(GRID_W, dtype=F32), rows)[:, None] * inv
    quarter = axis_dim // 2
    cos = jnp.concatenate([jnp.cos(row), jnp.cos(row), jnp.cos(col), jnp.cos(col)], axis=1)
    sin = jnp.concatenate([jnp.sin(row), jnp.sin(row), jnp.sin(col), jnp.sin(col)], axis=1)
    second = (np.arange(head_dim) % axis_dim) >= quarter
    reps = LANES // head_dim
    cos = jnp.tile(cos, (1, reps))
    sin = jnp.tile(sin, (1, reps))
    second = jnp.asarray(np.tile(second, reps))[None, :]
    return cos, jnp.where(second, sin, 0.0), jnp.where(second, 0.0, -sin)


def kernel(x, c, ctx, c_ctx, ada_w, ada_b, attn_w_in, attn_b_in, attn_sink, attn_w_out, hy_w_in, hy_b_in, hy_conv_w, hy_conv_b, hy_f_w1, hy_f_b1, hy_f_w2, hy_f_b2, hy_f_w3, hy_f_b3, hy_f_freq, hy_f_w4, hy_f_bias, hy_w_out, hy_b_out, ln1_g, ln1_b, ln2_g, ln2_b, router_w, router_b, moe_w1, moe_w3, moe_w2):
    bsz, seq, d = x.shape
    depth = ada_w.shape[0]
    assert depth == 2 and attn_w_in.shape[0] == 1 and hy_w_in.shape[0] == 1
    alpha = (2 * depth) ** 0.25
    n_heads = attn_sink.shape[1]
    attn_dim = attn_w_out.shape[1]
    head_dim = attn_dim // n_heads
    kv_dim = (attn_w_in.shape[2] - attn_dim) // 2
    n_kv = kv_dim // head_dim
    group = n_heads // n_kv
    assert head_dim * 2 == LANES and group % 2 == 0
    n_exp = router_w.shape[1]
    assert n_exp == N_GROUPS * EXPERTS_PER_GROUP

    cond = jnp.concatenate([c, c_ctx[None, :], jnp.zeros((8 - bsz - 1, d), F32)], axis=0)
    mods = _ada_call(cond, ada_w, ada_b).reshape(depth, 8, 6, d)
    mod = lambda layer, k: mods[layer, :bsz, k].reshape(bsz, 1, d)
    cmod = lambda layer, k: mods[layer, bsz, k].reshape(1, d)
    row = lambda v: v.reshape(1, -1)

    rw_pad = jnp.concatenate([router_w, jnp.zeros((d, LANES - n_exp), F32)], axis=1).astype(BF16)
    rb = router_b.reshape(n_exp, 1)
    w1b, w3b, w2b = moe_w1.astype(BF16), moe_w3.astype(BF16), moe_w2

    w_ext = attn_w_in[0].astype(BF16)
    b_ext = row(attn_b_in[0])
    w_kv, b_kv = w_ext[:, attn_dim:], b_ext[:, attn_dim:]
    n_kd = 2 * kv_dim
    cos_t, sa_t, sb_t = _rope_tables(seq, head_dim)

    q, kd, vd = _qkv_call(x, mod(0, 1), mod(0, 0), w_ext, b_ext, cos_t, sa_t, sb_t, attn_dim, n_kd, head_dim)
    kxd, vxd = _ctx_kv_call(ctx, cmod(0, 1), cmod(0, 0), w_kv, b_kv, n_kd)
    att = _attn_call(attn_sink[0], q, kd, vd, kxd, vxd, n_kv, group)
    h1, xp, meta, cnt = _proj_ln_call(att, attn_w_out[0].astype(BF16), jnp.zeros((1, d), F32), x, mod(0, 2),
                                      row(ln1_g[0]), row(ln1_b[0]), mod(0, 4), mod(0, 3), rw_pad, rb, alpha)
    h = _moe_layer(xp, meta, cnt, h1, mod(0, 5), row(ln2_g[0]), row(ln2_b[0]), w1b, w3b, w2b, 0, alpha)

    x0, z = _hy_in_call(h, mod(1, 1), mod(1, 0), hy_w_in[0].astype(BF16), row(hy_b_in[0]), hy_conv_w[0],
                        row(hy_conv_b[0]))
    filt, l1 = _filter_call(hy_f_w1[0], hy_f_b1[0], hy_f_w2[0], hy_f_b2[0], hy_f_w3[0], hy_f_b3[0],
                            hy_f_freq[0], hy_f_w4[0], seq)
    yh = _hyena_conv(x0, z, filt, l1, row(hy_f_bias[0]))
    h1, xp, meta, cnt = _proj_ln_call(yh, hy_w_out[0].astype(BF16), row(hy_b_out[0]), h, mod(1, 2),
                                      row(ln1_g[1]), row(ln1_b[1]), mod(1, 4), mod(1, 3), rw_pad, rb, alpha)
    return _moe_layer(xp, meta, cnt, h1, mod(1, 5), row(ln2_g[1]), row(ln2_b[1]), w1b, w3b, w2b, 1, alpha)
```

```python
import functools
import math

import numpy as np
import jax
import jax.numpy as jnp
from jax import lax
from jax.experimental import pallas as pl
from jax.experimental.pallas import tpu as pltpu

F32 = jnp.float32
BF16 = jnp.bfloat16
I32 = jnp.int32
U32 = jnp.uint32
HIGHEST = lax.Precision.HIGHEST

LANES = 128
V7X_VMEM_LIMIT_BYTES = 56 * 1024 * 1024

GRID_W = 64
BLOCK = 128
ROPE_BASE = 10000.0
NEG_INF = -1e30
HY_BANDS = 16
HY_FAST_DECAY = 0.3
HY_SLOW_DECAY = 1.5
HY_TARGET = 1e-2
N_GROUPS = 4
EXPERTS_PER_GROUP = 4
N_PAIRS = 6
N_BUCKETS = N_GROUPS * N_PAIRS
BUCKET_ROWS = 32
LN_EPS = 1e-5
FFT_N2 = 128

TOKEN_TILE = 512
PROJ_CHUNKS = 4
DMA_QUEUES = 2


def _cparams(*sem):
    return pltpu.CompilerParams(dimension_semantics=sem, vmem_limit_bytes=V7X_VMEM_LIMIT_BYTES)


def _resident(block_shape, index_map):
    return pl.BlockSpec(block_shape, index_map, pipeline_mode=pl.Buffered(1))


def _silu(x):
    return x * jax.nn.sigmoid(x)


def _ada_kernel(c_ref, w_ref, b_ref, o_ref):
    c = _silu(c_ref[...])
    o_ref[...] = jnp.dot(c, w_ref[...], preferred_element_type=F32, precision=HIGHEST) + b_ref[...]


def _ada_call(cond, ada_w, ada_b):
    depth, d, n6 = ada_w.shape
    tn = 1024
    rows = cond.shape[0]
    return pl.pallas_call(
        _ada_kernel,
        out_shape=jax.ShapeDtypeStruct((depth, rows, n6), F32),
        grid=(depth, n6 // tn),
        in_specs=[
            pl.BlockSpec((rows, d), lambda l, j: (0, 0)),
            pl.BlockSpec((None, d, tn), lambda l, j: (l, 0, j)),
            pl.BlockSpec((None, 1, tn), lambda l, j: (l, 0, j)),
        ],
        out_specs=pl.BlockSpec((None, rows, tn), lambda l, j: (l, 0, j)),
        compiler_params=_cparams("arbitrary", "arbitrary"),
        name="ada_mod",
    )(cond, ada_w, ada_b.reshape(depth, 1, n6))


def _qkv_kernel(x_ref, sc_ref, sh_ref, w_ref, b_ref, cos_ref, sa_ref, sb_ref, q_ref, k_ref, v_ref, *, scale):
    u = (x_ref[...] * (1.0 + sc_ref[...]) + sh_ref[...]).astype(BF16)
    p = jnp.dot(u, w_ref[...], preferred_element_type=F32) + b_ref[...]
    cos, sa, sb = cos_ref[...], sa_ref[...], sb_ref[...]
    nq = q_ref.shape[1]
    nk = k_ref.shape[1]

    def rope(xc):
        return xc * cos + pltpu.roll(xc, 16, 1) * sa + pltpu.roll(xc, LANES - 16, 1) * sb

    for c in range(nq // LANES):
        q_ref[:, c * LANES:(c + 1) * LANES] = (rope(p[:, c * LANES:(c + 1) * LANES]) * scale).astype(BF16)
    nkv = nk // 2
    for c in range(nkv // LANES):
        _store_dup_heads(k_ref, c, rope(p[:, nq + c * LANES:nq + (c + 1) * LANES]))
        _store_dup_heads(v_ref, c, p[:, nq + nkv + c * LANES:nq + nkv + (c + 1) * LANES])


def _store_dup_heads(ref, c, pair):
    lo = lax.broadcasted_iota(I32, (1, LANES), 1) < (LANES // 2)
    swapped = pltpu.roll(pair, LANES // 2, 1)
    ref[:, (2 * c) * LANES:(2 * c + 1) * LANES] = jnp.where(lo, pair, swapped).astype(ref.dtype)
    ref[:, (2 * c + 1) * LANES:(2 * c + 2) * LANES] = jnp.where(lo, swapped, pair).astype(ref.dtype)


def _qkv_call(x, sc, sh, w_ext, b_ext, cos_t, sa_t, sb_t, n_q, n_kd, head_dim):
    bsz, s, d = x.shape
    tm = min(TOKEN_TILE, s)
    n_out = w_ext.shape[1]
    kern = functools.partial(_qkv_kernel, scale=head_dim ** -0.5)
    return pl.pallas_call(
        kern,
        out_shape=(jax.ShapeDtypeStruct((bsz, s, n_q), BF16),
                   jax.ShapeDtypeStruct((bsz, s, n_kd), BF16),
                   jax.ShapeDtypeStruct((bsz, s, n_kd), BF16)),
        grid=(bsz, s // tm),
        in_specs=[
            pl.BlockSpec((None, tm, d), lambda b, i: (b, i, 0)),
            pl.BlockSpec((None, 1, d), lambda b, i: (b, 0, 0)),
            pl.BlockSpec((None, 1, d), lambda b, i: (b, 0, 0)),
            _resident((d, n_out), lambda b, i: (0, 0)),
            pl.BlockSpec((1, n_out), lambda b, i: (0, 0)),
            pl.BlockSpec((tm, LANES), lambda b, i: (i, 0)),
            pl.BlockSpec((tm, LANES), lambda b, i: (i, 0)),
            pl.BlockSpec((tm, LANES), lambda b, i: (i, 0)),
        ],
        out_specs=(pl.BlockSpec((None, tm, n_q), lambda b, i: (b, i, 0)),
                   pl.BlockSpec((None, tm, n_kd), lambda b, i: (b, i, 0)),
                   pl.BlockSpec((None, tm, n_kd), lambda b, i: (b, i, 0))),
        compiler_params=_cparams("arbitrary", "arbitrary"),
        name="attn_qkv",
    )(x, sc, sh, w_ext, b_ext, cos_t, sa_t, sb_t)


def _ctx_kv_kernel(x_ref, sc_ref, sh_ref, w_ref, b_ref, k_ref, v_ref):
    u = (x_ref[...] * (1.0 + sc_ref[...]) + sh_ref[...]).astype(BF16)
    p = jnp.dot(u, w_ref[...], preferred_element_type=F32) + b_ref[...]
    nkv = k_ref.shape[1] // 2
    for c in range(nkv // LANES):
        _store_dup_heads(k_ref, c, p[:, c * LANES:(c + 1) * LANES])
        _store_dup_heads(v_ref, c, p[:, nkv + c * LANES:nkv + (c + 1) * LANES])


def _ctx_kv_call(ctx, csc, csh, w_kv, b_kv, n_kd):
    bsz, c, d = ctx.shape
    return pl.pallas_call(
        _ctx_kv_kernel,
        out_shape=(jax.ShapeDtypeStruct((bsz, c, n_kd), BF16), jax.ShapeDtypeStruct((bsz, c, n_kd), BF16)),
        grid=(bsz,),
        in_specs=[
            pl.BlockSpec((None, c, d), lambda b: (b, 0, 0)),
            pl.BlockSpec((1, d), lambda b: (0, 0)),
            pl.BlockSpec((1, d), lambda b: (0, 0)),
            pl.BlockSpec((d, n_kd), lambda b: (0, 0)),
            pl.BlockSpec((1, n_kd), lambda b: (0, 0)),
        ],
        out_specs=(pl.BlockSpec((None, c, n_kd), lambda b: (b, 0, 0)),
                   pl.BlockSpec((None, c, n_kd), lambda b: (b, 0, 0))),
        compiler_params=_cparams("arbitrary"),
        name="attn_ctx_kv",
    )(ctx, csc, csh, w_kv, b_kv)


def _attn_kernel(sink_ref, q_ref, kp_ref, kc_ref, kn_ref, vp_ref, vc_ref, vn_ref, kx_ref, vx_ref, o_ref,
                 *, n_kv, group, nb):
    n = pl.program_id(1)
    r = lax.broadcasted_iota(I32, (BLOCK, BLOCK), 0)
    j = lax.broadcasted_iota(I32, (BLOCK, BLOCK), 1)
    prev_ok = (j >= r) & (n > 0)
    next_ok = (j <= r) & (n < nb - 1)
    lo = lax.broadcasted_iota(I32, (1, LANES), 1) < (LANES // 2)
    pairs = group // 2
    for kh in range(n_kv):
        sl = slice(kh * LANES, (kh + 1) * LANES)
        kcat = jnp.concatenate([kp_ref[:, sl], kc_ref[:, sl], kn_ref[:, sl], kx_ref[:, sl]], axis=0)
        vcat = jnp.concatenate([vp_ref[:, sl], vc_ref[:, sl], vn_ref[:, sl], vx_ref[:, sl]], axis=0)
        nkeys = kcat.shape[0]
        parts = []
        for pp in range(pairs):
            q2 = q_ref[:, (kh * pairs + pp) * LANES:(kh * pairs + pp + 1) * LANES]
            zq = jnp.zeros_like(q2)
            parts += [jnp.where(lo, q2, zq), jnp.where(lo, zq, q2)]
        qs = jnp.concatenate(parts, axis=0)
        s_all = lax.dot_general(qs, kcat, (((1,), (1,)), ((), ())), preferred_element_type=F32)
        e_parts, rdens = [], []
        for g in range(group):
            s = s_all[g * BLOCK:(g + 1) * BLOCK]
            s = jnp.concatenate([
                jnp.where(prev_ok, s[:, :BLOCK], NEG_INF),
                s[:, BLOCK:2 * BLOCK],
                jnp.where(next_ok, s[:, 2 * BLOCK:3 * BLOCK], NEG_INF),
                s[:, 3 * BLOCK:]], axis=1)
            sk = sink_ref[kh * group + g]
            m = jnp.maximum(jnp.max(s, axis=1, keepdims=True), sk)
            e = jnp.exp(s - m)
            rdens.append(1.0 / (jnp.sum(e, axis=1, keepdims=True) + jnp.exp(sk - m)))
            e_parts.append(e.astype(BF16))
        o = jnp.dot(jnp.concatenate(e_parts, axis=0), vcat, preferred_element_type=F32)
        for pp in range(pairs):
            p = kh * pairs + pp
            o_lo = o[(2 * pp) * BLOCK:(2 * pp + 1) * BLOCK] * rdens[2 * pp]
            o_hi = o[(2 * pp + 1) * BLOCK:(2 * pp + 2) * BLOCK] * rdens[2 * pp + 1]
            o_ref[:, p * LANES:(p + 1) * LANES] = jnp.where(lo, o_lo, o_hi).astype(BF16)


def _attn_call(sink, q, kd, vd, kxd, vxd, n_kv, group):
    bsz, s, n_q = q.shape
    n_kd = kd.shape[2]
    c = kxd.shape[1]
    nb = s // BLOCK
    kern = functools.partial(_attn_kernel, n_kv=n_kv, group=group, nb=nb)
    prev = lambda b, n: (b, jnp.maximum(n - 1, 0), 0)
    cur = lambda b, n: (b, n, 0)
    nxt = lambda b, n: (b, jnp.minimum(n + 1, nb - 1), 0)
    kv = lambda im: pl.BlockSpec((None, BLOCK, n_kd), im)
    return pl.pallas_call(
        kern,
        out_shape=jax.ShapeDtypeStruct((bsz, s, n_q), BF16),
        grid=(bsz, nb),
        in_specs=[
            pl.BlockSpec(memory_space=pltpu.SMEM),
            pl.BlockSpec((None, BLOCK, n_q), cur),
            kv(prev), kv(cur), kv(nxt), kv(prev), kv(cur), kv(nxt),
            pl.BlockSpec((None, c, n_kd), lambda b, n: (b, 0, 0)),
            pl.BlockSpec((None, c, n_kd), lambda b, n: (b, 0, 0)),
        ],
        out_specs=pl.BlockSpec((None, BLOCK, n_q), cur),
        compiler_params=_cparams("arbitrary", "arbitrary"),
        name="window_attn",
    )(sink, q, kd, kd, kd, vd, vd, vd, kxd, vxd)


def _layer_norm(r, g, b):
    mu = jnp.mean(r, axis=-1, keepdims=True)
    xc = r - mu
    var = jnp.mean(xc * xc, axis=-1, keepdims=True)
    return xc * lax.rsqrt(var + LN_EPS) * g + b


def _route(logits_t, rb):
    s = jax.nn.sigmoid(logits_t)
    sel = s + rb
    n_e = N_GROUPS * EXPERTS_PER_GROUP
    sel_r = [sel[e:e + 1, :] for e in range(n_e)]
    s_r = [s[e:e + 1, :] for e in range(n_e)]
    gscore = []
    for g in range(N_GROUPS):
        a, b, c, d = sel_r[4 * g:4 * g + 4]
        m1, n1, m2, n2 = jnp.maximum(a, b), jnp.minimum(a, b), jnp.maximum(c, d), jnp.minimum(c, d)
        gscore.append(jnp.maximum(m1, m2) + jnp.maximum(jnp.minimum(m1, m2), jnp.maximum(n1, n2)))
    best, gi = gscore[0], jnp.zeros_like(gscore[0], dtype=I32)
    for g in range(1, N_GROUPS):
        upd = gscore[g] > best
        gi = jnp.where(upd, g, gi)
        best = jnp.where(upd, gscore[g], best)

    def pick(rows, i):
        out = rows[i]
        for g in range(1, N_GROUPS):
            out = jnp.where(gi == g, rows[4 * g + i], out)
        return out

    v = [pick(sel_r, i) for i in range(EXPERTS_PER_GROUP)]
    sv = [pick(s_r, i) for i in range(EXPERTS_PER_GROUP)]

    def argmax4(vals):
        bv, bi = vals[0], jnp.zeros_like(gi)
        for i in range(1, EXPERTS_PER_GROUP):
            upd = vals[i] > bv
            bi = jnp.where(upd, i, bi)
            bv = jnp.where(upd, vals[i], bv)
        return bi

    def take4(vals, idx):
        out = vals[0]
        for i in range(1, EXPERTS_PER_GROUP):
            out = jnp.where(idx == i, vals[i], out)
        return out

    i1 = argmax4(v)
    i2 = argmax4([jnp.where(i1 == i, -jnp.inf, v[i]) for i in range(EXPERTS_PER_GROUP)])
    s1, s2 = take4(sv, i1), take4(sv, i2)
    tot = s1 + s2
    g1, g2 = s1 / tot, s2 / tot
    first_lo = i1 < i2
    i_lo, i_hi = jnp.minimum(i1, i2), jnp.maximum(i1, i2)
    g_lo, g_hi = jnp.where(first_lo, g1, g2), jnp.where(first_lo, g2, g1)
    pair = jnp.where(i_lo == 0, i_hi - 1, jnp.where(i_lo == 1, i_hi + 1, N_PAIRS - 1))
    bucket = gi * N_PAIRS + pair
    return 4 * gi + i_lo, 4 * gi + i_hi, g_lo, g_hi, bucket


def _proj_ln_kernel(a_ref, w_ref, bias_ref, h_ref, gate_ref, lng_ref, lnb_ref, sc_ref, sh_ref, rw_ref, rb_ref,
                    h1_ref, xp_ref, meta_ref, cnt_ref, run_ref, *, alpha, a_packed):
    first = (pl.program_id(0) == 0) & (pl.program_id(1) == 0)

    @pl.when(first)
    def _():
        run_ref[...] = jnp.zeros_like(run_ref)

    tm, d = h_ref.shape
    half = d // 2
    chunk = tm // PROJ_CHUNKS
    logit_parts = []
    ys = []
    for c in range(PROJ_CHUNKS):
        rs = slice(c * chunk, (c + 1) * chunk)
        a = _unpack_cols(a_ref[rs]).astype(BF16) if a_packed else a_ref[rs]
        ys.append(jnp.dot(a, w_ref[...], preferred_element_type=F32) + bias_ref[...])
    for c in range(PROJ_CHUNKS):
        rs = slice(c * chunk, (c + 1) * chunk)
        y = ys[c]
        h1 = _layer_norm(alpha * h_ref[rs] + gate_ref[...] * y, lng_ref[...], lnb_ref[...])
        h1_ref[rs] = h1
        tb = (h1 * (1.0 + sc_ref[...]) + sh_ref[...]).astype(BF16)
        bits = lax.bitcast_convert_type(tb.astype(F32), U32)
        xp_ref[rs, :half] = (bits[:, :half] & jnp.uint32(0xFFFF0000)) | (bits[:, half:] >> 16)
        logits = jnp.dot(tb, rw_ref[...], preferred_element_type=F32)
        logit_parts.append(jnp.transpose(logits)[:N_GROUPS * EXPERTS_PER_GROUP, :])
    logits_t = jnp.concatenate(logit_parts, axis=1)
    e_lo, e_hi, g_lo, g_hi, bucket = _route(logits_t, rb_ref[...])

    rows = lax.broadcasted_iota(I32, (BUCKET_ROWS, tm), 0)
    onehot = (rows == bucket).astype(F32)
    tri = (lax.broadcasted_iota(I32, (tm, tm), 0) <= lax.broadcasted_iota(I32, (tm, tm), 1)).astype(BF16)
    cum = jnp.dot(onehot.astype(BF16), tri, preferred_element_type=F32)
    run = run_ref[:, 0:1]
    rank = jnp.sum(onehot * (cum - 1.0 + run), axis=0, keepdims=True)
    new_run = run + cum[:, tm - 1:tm]
    run_ref[...] = jnp.broadcast_to(new_run, run_ref.shape)
    cnt_ref[...] = jnp.broadcast_to(new_run, cnt_ref.shape)

    mrow = lax.broadcasted_iota(I32, (8, tm), 0)
    meta = jnp.where(mrow == 0, e_lo.astype(F32), 0.0)
    meta = jnp.where(mrow == 1, e_hi.astype(F32), meta)
    meta = jnp.where(mrow == 2, g_lo, meta)
    meta = jnp.where(mrow == 3, g_hi, meta)
    meta = jnp.where(mrow == 4, bucket.astype(F32), meta)
    meta = jnp.where(mrow == 5, rank, meta)
    meta_ref[...] = meta

    grow = lax.broadcasted_iota(I32, (LANES, tm), 0)
    gates_t = jnp.where(grow == 0, g_lo, jnp.where(grow == 1, g_hi, 0.0))
    xp_ref[:, half:] = lax.bitcast_convert_type(jnp.transpose(gates_t), U32)


def _proj_ln_call(a, w, bias, h, gate, lng, lnb, sc, sh, rw_pad, rb, alpha):
    bsz, s, d = h.shape
    a_packed = a.dtype == U32
    da = a.shape[2]
    dm = w.shape[0]
    tm = min(TOKEN_TILE, s)
    t = bsz * s
    nt = s // tm
    dp = d // 2 + LANES
    kern = functools.partial(_proj_ln_kernel, alpha=alpha, a_packed=a_packed)
    vec = lambda: pl.BlockSpec((1, d), lambda b, i: (0, 0))
    bvec = lambda: pl.BlockSpec((None, 1, d), lambda b, i: (b, 0, 0))
    return pl.pallas_call(
        kern,
        out_shape=(jax.ShapeDtypeStruct((bsz, s, d), F32),
                   jax.ShapeDtypeStruct((t, dp), U32),
                   jax.ShapeDtypeStruct((8, t), F32),
                   jax.ShapeDtypeStruct((BUCKET_ROWS, LANES), F32)),
        grid=(bsz, nt),
        in_specs=[
            pl.BlockSpec((None, tm, da), lambda b, i: (b, i, 0)),
            _resident((dm, d), lambda b, i: (0, 0)),
            vec(),
            pl.BlockSpec((None, tm, d), lambda b, i: (b, i, 0)),
            bvec(), vec(), vec(), bvec(), bvec(),
            pl.BlockSpec((d, LANES), lambda b, i: (0, 0)),
            pl.BlockSpec((N_GROUPS * EXPERTS_PER_GROUP, 1), lambda b, i: (0, 0)),
        ],
        out_specs=(pl.BlockSpec((None, tm, d), lambda b, i: (b, i, 0)),
                   pl.BlockSpec((tm, dp), lambda b, i: (b * nt + i, 0)),
                   pl.BlockSpec((8, tm), lambda b, i: (0, b * nt + i)),
                   pl.BlockSpec((BUCKET_ROWS, LANES), lambda b, i: (0, 0))),
        scratch_shapes=[pltpu.VMEM((BUCKET_ROWS, LANES), F32)],
        compiler_params=_cparams("arbitrary", "arbitrary"),
        name="proj_ln_route",
    )(a, w, bias, h, gate, lng, lnb, sc, sh, rw_pad, rb)


def _wait_tile(src_tile, dst_tile, sem):
    pltpu.make_async_copy(src_tile, dst_tile, sem).wait()


def _scatter_kernel(dest_ref, xp_ref, init_ref, xs_ref, sem):
    del init_ref
    tm = xp_ref.shape[0]
    base = pl.program_id(0) * tm

    def issue(i, c):
        for p in range(DMA_QUEUES):
            r = i * DMA_QUEUES + p
            pltpu.make_async_copy(xp_ref.at[pl.ds(r, 1)], xs_ref.at[pl.ds(dest_ref[base + r], 1)],
                                  sem).start(priority=p)
        return c

    lax.fori_loop(0, tm // DMA_QUEUES, issue, 0, unroll=4)
    _wait_tile(xp_ref, xs_ref.at[pl.ds(0, tm)], sem)


def _scatter_call(dest, xp, rows_out):
    t, dp = xp.shape
    tm = min(TOKEN_TILE, t)
    init = jnp.zeros((rows_out, dp), U32)
    return pl.pallas_call(
        _scatter_kernel,
        out_shape=jax.ShapeDtypeStruct((rows_out, dp), U32),
        grid_spec=pltpu.PrefetchScalarGridSpec(
            num_scalar_prefetch=1,
            grid=(t // tm,),
            in_specs=[pl.BlockSpec((tm, dp), lambda i, dest: (i, 0)),
                      pl.BlockSpec(memory_space=pl.ANY)],
            out_specs=pl.BlockSpec(memory_space=pl.ANY),
            scratch_shapes=[pltpu.SemaphoreType.DMA],
        ),
        input_output_aliases={2: 0},
        compiler_params=_cparams("arbitrary"),
        name="moe_scatter",
    )(dest, xp, init)


def _moe_kernel(ex_ref, valid_ref, xs_ref, w1_ref, w3_ref, w2_ref, y_ref, acc_ref):
    del ex_ref
    i = pl.program_id(0)
    s = pl.program_id(1)
    half = xs_ref.shape[1] - LANES
    ok = valid_ref[i] > 0

    @pl.when((i == 0) & (s == 0))
    def _():
        acc_ref[...] = jnp.zeros_like(acc_ref)

    @pl.when(ok)
    def _():
        w = xs_ref[:, :half]
        hi = lax.bitcast_convert_type(w & jnp.uint32(0xFFFF0000), F32).astype(BF16)
        lo = lax.bitcast_convert_type(w << 16, F32).astype(BF16)
        x = jnp.concatenate([hi, lo], axis=1)
        gates = lax.bitcast_convert_type(xs_ref[:, half:], F32)
        which = (s + i) % 2
        g = jnp.where(which == 0, gates[:, 0:1], gates[:, 1:2])
        ffh = w1_ref.shape[1] // 2
        y = None
        for c in range(2):
            fs = slice(c * ffh, (c + 1) * ffh)
            a = jnp.dot(x, w1_ref[:, fs], preferred_element_type=F32)
            b = jnp.dot(x, w3_ref[:, fs].astype(BF16), preferred_element_type=F32)
            hid = (_silu(a) * b * g).astype(BF16)
            part = jnp.dot(hid, w2_ref[fs, :].astype(BF16), preferred_element_type=F32)
            y = part if y is None else y + part
        total = jnp.where(s == 0, 0.0, acc_ref[...]) + y
        acc_ref[...] = total
        y_ref[...] = _pack_cols(total)

    @pl.when(jnp.logical_not(ok) & (s == 0))
    def _():
        y_ref[...] = jnp.zeros_like(y_ref)


def _moe_call(ex, valid, xs, w1, w3, w2, layer):
    rows, dp = xs.shape
    _, n_e, d, ff = w1.shape
    tm = min(TOKEN_TILE, rows)
    ntiles = rows // tm
    return pl.pallas_call(
        _moe_kernel,
        out_shape=jax.ShapeDtypeStruct((rows, d // 2), U32),
        grid_spec=pltpu.PrefetchScalarGridSpec(
            num_scalar_prefetch=2,
            grid=(ntiles, 2),
            in_specs=[
                pl.BlockSpec((tm, dp), lambda i, s, ex, va: (i, 0)),
                pl.BlockSpec((None, None, d, ff), lambda i, s, ex, va: (layer, ex[2 * i + s], 0, 0)),
                pl.BlockSpec((None, None, d, ff), lambda i, s, ex, va: (layer, ex[2 * i + s], 0, 0)),
                pl.BlockSpec((None, None, ff, d), lambda i, s, ex, va: (layer, ex[2 * i + s], 0, 0)),
            ],
            out_specs=pl.BlockSpec((tm, d // 2), lambda i, s, ex, va: (i, 0)),
            scratch_shapes=[pltpu.VMEM((tm, d), F32)],
        ),
        compiler_params=_cparams("arbitrary", "arbitrary"),
        name="moe_experts",
    )(ex, valid, xs, w1, w3, w2)


GATHER_CHUNKS = 8


def _gather_ln_kernel(dest_ref, ys_ref, h_ref, gate_ref, lng_ref, lnb_ref, o_ref, buf, sem, *, alpha, nt):
    tm = h_ref.shape[0]
    step = pl.program_id(0) * nt + pl.program_id(1)
    last = pl.num_programs(0) * nt - 1
    slot = step % 2

    def row_copy(tile, sl, r):
        return pltpu.make_async_copy(ys_ref.at[pl.ds(dest_ref[tile * tm + r], 1)], buf.at[sl, pl.ds(r, 1)],
                                     sem.at[sl])

    def wait_tile(sl):
        _wait_tile(ys_ref.at[pl.ds(0, tm)], buf.at[sl], sem.at[sl])

    @pl.when(step == 0)
    def _():
        def issue(i, c):
            for p in range(DMA_QUEUES):
                row_copy(0, 0, i * DMA_QUEUES + p).start(priority=p)
            return c

        lax.fori_loop(0, tm // DMA_QUEUES, issue, 0, unroll=4)

    wait_tile(slot)
    nxt = jnp.minimum(step + 1, last)
    rows = tm // GATHER_CHUNKS
    for c in range(GATHER_CHUNKS):
        for r in range(c * rows, (c + 1) * rows):
            row_copy(nxt, 1 - slot, r).start(priority=r % DMA_QUEUES)
        sl = slice(c * rows, (c + 1) * rows)
        f = _unpack_cols(buf[slot, sl])
        o_ref[sl] = _layer_norm(alpha * h_ref[sl] + gate_ref[...] * f, lng_ref[...], lnb_ref[...])

    @pl.when(step == last)
    def _():
        wait_tile(1 - slot)


def _gather_ln_call(dest, ys, h, gate, lng, lnb, alpha):
    bsz, s, d = h.shape
    tm = min(TOKEN_TILE, s)
    nt = s // tm
    kern = functools.partial(_gather_ln_kernel, alpha=alpha, nt=nt)
    return pl.pallas_call(
        kern,
        out_shape=jax.ShapeDtypeStruct((bsz, s, d), F32),
        grid_spec=pltpu.PrefetchScalarGridSpec(
            num_scalar_prefetch=1,
            grid=(bsz, nt),
            in_specs=[
                pl.BlockSpec(memory_space=pl.ANY),
                pl.BlockSpec((None, tm, d), lambda b, i, dest: (b, i, 0)),
                pl.BlockSpec((None, 1, d), lambda b, i, dest: (b, 0, 0)),
                pl.BlockSpec((1, d), lambda b, i, dest: (0, 0)),
                pl.BlockSpec((1, d), lambda b, i, dest: (0, 0)),
            ],
            out_specs=pl.BlockSpec((None, tm, d), lambda b, i, dest: (b, i, 0)),
            scratch_shapes=[pltpu.VMEM((2, tm, d // 2), U32), pltpu.SemaphoreType.DMA((2,))],
        ),
        compiler_params=_cparams("arbitrary", "arbitrary"),
        name="moe_gather_ln",
    )(dest, ys, h, gate, lng, lnb)


_PAIR_LO = (0, 0, 0, 1, 1, 2)
_PAIR_HI = (1, 2, 3, 2, 3, 3)


def _moe_layer(xp, meta, cnt, h1, gate2, lng, lnb, w1, w3, w2, layer, alpha):
    t = xp.shape[0]
    tm = min(TOKEN_TILE, t)
    ntiles = t // tm + N_BUCKETS
    counts = cnt[:N_BUCKETS, 0].astype(I32)
    tiles_b = (counts + tm - 1) // tm
    tile_end = jnp.cumsum(tiles_b)
    offs = (tile_end - tiles_b) * tm
    bucket = meta[4].astype(I32)
    dest = offs[bucket] + meta[5].astype(I32)

    tile = jnp.arange(ntiles, dtype=I32)
    valid = (tile < tile_end[-1]).astype(I32)
    tb = jnp.minimum(jnp.sum((tile[:, None] >= tile_end[None, :]).astype(I32), axis=1), N_BUCKETS - 1)
    lo = jnp.asarray(_PAIR_LO, I32)[tb % N_PAIRS] + EXPERTS_PER_GROUP * (tb // N_PAIRS)
    hi = jnp.asarray(_PAIR_HI, I32)[tb % N_PAIRS] + EXPERTS_PER_GROUP * (tb // N_PAIRS)
    odd = (tile % 2) == 1
    ex = jnp.stack([jnp.where(odd, hi, lo), jnp.where(odd, lo, hi)], axis=1).reshape(-1)

    xs = _scatter_call(dest, xp, ntiles * tm)
    ys = _moe_call(ex, valid, xs, w1, w3, w2, layer)
    return _gather_ln_call(dest, ys, h1, gate2, lng, lnb, alpha)


def _hy_in_kernel(xm_ref, xp_ref, xn_ref, sc_ref, sh_ref, w_ref, b_ref, cw_ref, cb_ref, x0_ref, z_ref, *, nt, tn):
    i = pl.program_id(1)
    tm, d = xm_ref.shape
    sc, sh = 1.0 + sc_ref[...], sh_ref[...]
    u = jnp.concatenate([xp_ref[...] * sc + sh, xm_ref[...] * sc + sh, xn_ref[...] * sc + sh], axis=0).astype(BF16)
    rows = tm + 16
    rid = lax.broadcasted_iota(I32, (rows, 1), 0)
    keep = ((rid >= 8) | (i > 0)) & ((rid < tm + 8) | (i < nt - 1))

    def conv(sec, j):
        col = sec * d + j * tn
        p = jnp.dot(u, w_ref[:, col:col + tn], preferred_element_type=F32) + b_ref[:, col:col + tn]
        p = jnp.where(keep, p, 0.0)
        cw = cw_ref[:, col:col + tn]
        out = (cw[0:1] * pltpu.roll(p, 1, 0) + cw[1:2] * p + cw[2:3] * pltpu.roll(p, rows - 1, 0)
               + cb_ref[:, col:col + tn])
        return out[8:8 + tm]

    nch = d // tn
    for j in range(nch // 2):
        x0_ref[:, j * tn:(j + 1) * tn] = _pack_pair(conv(0, j), conv(0, j + nch // 2))
        z_ref[:, j * tn:(j + 1) * tn] = _pack_pair(conv(1, j) * conv(2, j),
                                                   conv(1, j + nch // 2) * conv(2, j + nch // 2))


def _hy_in_call(h, sc, sh, w, b, cw, cb):
    bsz, s, d = h.shape
    tm = min(TOKEN_TILE, s)
    nt = s // tm
    hb = tm // 8
    tn = 1024
    kern = functools.partial(_hy_in_kernel, nt=nt, tn=tn)
    return pl.pallas_call(
        kern,
        out_shape=(jax.ShapeDtypeStruct((bsz, s, d // 2), U32), jax.ShapeDtypeStruct((bsz, s, d // 2), U32)),
        grid=(bsz, nt),
        in_specs=[
            pl.BlockSpec((None, tm, d), lambda b, i: (b, i, 0)),
            pl.BlockSpec((None, 8, d), lambda b, i: (b, jnp.maximum(i * hb - 1, 0), 0)),
            pl.BlockSpec((None, 8, d), lambda b, i: (b, jnp.minimum((i + 1) * hb, nt * hb - 1), 0)),
            pl.BlockSpec((None, 1, d), lambda b, i: (b, 0, 0)),
            pl.BlockSpec((None, 1, d), lambda b, i: (b, 0, 0)),
            _resident((d, 3 * d), lambda b, i: (0, 0)),
            pl.BlockSpec((1, 3 * d), lambda b, i: (0, 0)),
            pl.BlockSpec((3, 3 * d), lambda b, i: (0, 0)),
            pl.BlockSpec((1, 3 * d), lambda b, i: (0, 0)),
        ],
        out_specs=(pl.BlockSpec((None, tm, d // 2), lambda b, i: (b, i, 0)),
                   pl.BlockSpec((None, tm, d // 2), lambda b, i: (b, i, 0))),
        compiler_params=_cparams("arbitrary", "arbitrary"),
        name="hyena_in",
    )(h, h, h, sc, sh, w, b, cw, cb)


def _filter_kernel(w1_ref, b1_ref, w2_ref, b2_ref, w3_ref, b3_ref, fr_ref, w4_ref, dl_ref, f_ref, l1_ref,
                   *, seq, tn):
    i = pl.program_id(0)

    @pl.when(i == 0)
    def _():
        l1_ref[...] = jnp.zeros_like(l1_ref)

    n_lane = i * tn + lax.broadcasted_iota(I32, (1, tn), 1)
    m_lane = jnp.where(n_lane < seq, n_lane, 2 * seq - n_lane).astype(F32)
    t_lane = m_lane / (seq - 1.0)
    wl = (2.0 * math.pi) * m_lane / float(seq)
    band = lax.broadcasted_iota(I32, (HY_BANDS, 1), 0).astype(F32)
    fb = 1e-4 + band * ((HY_BANDS - 1 - 1e-4) / (HY_BANDS - 1))
    ang = fb * wl
    z = jnp.concatenate([t_lane, jnp.cos(ang), -jnp.sin(ang),
                         jnp.zeros((7, tn), F32)], axis=0)
    fr = fr_ref[...]

    def layer(w_ref, b_ref, x):
        pre = lax.dot_general(w_ref[...], x, (((0,), (0,)), ((), ())), preferred_element_type=F32,
                              precision=HIGHEST)
        return jnp.sin(fr * (pre + b_ref[...]))

    hdn = layer(w1_ref, b1_ref, z)
    hdn = layer(w2_ref, b2_ref, hdn)
    hdn = layer(w3_ref, b3_ref, hdn)
    h = lax.dot_general(hdn.astype(BF16), w4_ref[...].astype(BF16), (((0,), (0,)), ((), ())),
                        preferred_element_type=F32)
    n_col = i * tn + lax.broadcasted_iota(I32, (tn, 1), 0)
    m_col = jnp.where(n_col < seq, n_col, 2 * seq - n_col).astype(F32)
    h = h * jnp.exp(-(m_col / (seq - 1.0)) * dl_ref[...])
    h = jnp.where(n_col == seq, 0.0, h)
    l1_ref[...] += jnp.sum(jnp.abs(h), axis=0, keepdims=True)
    f_ref[...] = _pack_cols(h)


def _filter_call(fw1, fb1, fw2, fb2, fw3, fb3, freq, fw4, seq):
    width = fw2.shape[0]
    d = fw4.shape[1] // 2
    tn = min(512, seq)
    steps = 2 * seq // tn
    max_decay = math.log(HY_TARGET) / HY_FAST_DECAY
    min_decay = math.log(HY_TARGET) / HY_SLOW_DECAY
    deltas = jnp.abs(jnp.linspace(min_decay, max_decay, d, dtype=F32)).reshape(1, d)
    w1p = jnp.concatenate([fw1, jnp.zeros((7, width), F32)], axis=0)
    col = lambda a: a.reshape(width, 1)
    small = lambda shp: pl.BlockSpec(shp, lambda i: (0, 0))
    kern = functools.partial(_filter_kernel, seq=seq, tn=tn)
    return pl.pallas_call(
        kern,
        out_shape=(jax.ShapeDtypeStruct((2 * seq, d // 2), U32), jax.ShapeDtypeStruct((1, d), F32)),
        grid=(steps,),
        in_specs=[small((40, width)), small((width, 1)), small((width, width)), small((width, 1)),
                  small((width, width)), small((width, 1)), small((width, 1)),
                  pl.BlockSpec((width, d), lambda i: (0, (i * tn) // seq)),
                  small((1, d))],
        out_specs=(pl.BlockSpec((tn, d // 2), lambda i: (i, 0)), small((1, d))),
        compiler_params=_cparams("arbitrary"),
        name="hyena_filter",
    )(w1p, col(fb1), fw2, col(fb2), fw3, col(fb3), col(freq), fw4, deltas)


def _dft_tables(n1, n2):
    n = n1 * n2
    k = np.arange(n1)[:, None]
    m = np.arange(n1)[None, :]
    ang1 = -2.0 * np.pi * ((k * m) % n1) / n1
    f1r, f1i = np.cos(ang1), np.sin(ang1)
    hn = n1 // 2
    a_data = np.block([[f1r[:, :hn], -f1i[:, :hn]], [f1i[:, :hn], f1r[:, :hn]]])
    a_filt = np.concatenate([f1r, f1i], axis=0)
    a_inv = np.block([[f1r.T[:hn], f1i.T[:hn]], [-f1i.T[:hn], f1r.T[:hn]]]) / n
    k2 = np.arange(n2)[:, None]
    m2 = np.arange(n2)[None, :]
    ang2 = -2.0 * np.pi * ((k2 * m2) % n2) / n2
    f2r, f2i = np.cos(ang2), np.sin(ang2)
    b_fwd = np.block([[f2r, -f2i], [f2i, f2r]])
    b_inv = np.block([[f2r, f2i], [-f2i, f2r]])
    angt = -2.0 * np.pi * ((np.arange(n2)[:, None] * np.arange(n1)[None, :]) % n) / n
    tw = (np.cos(angt), np.sin(angt))
    perm = (np.arange(n1 // FFT_K1_GROUP)[None, :] * FFT_K1_GROUP + np.arange(FFT_K1_GROUP)[:, None]).reshape(-1)
    perm2 = np.concatenate([perm, perm + n1])
    as_bf16 = lambda a: jnp.asarray(a, F32).astype(BF16)
    return dict(a_data=as_bf16(a_data[perm2]), a_filt=as_bf16(a_filt[perm2]), a_inv=as_bf16(a_inv[:, perm2]),
                b_fwd=as_bf16(b_fwd), b_inv=as_bf16(b_inv),
                tw_n2=tuple(jnp.asarray(t[:, perm], F32).reshape(n2, n1, 1) for t in tw),
                tw_k1=tuple(jnp.asarray(t.T.copy(), F32).reshape(n1, n2, 1) for t in tw))


FFT_K1_GROUP = 4


def _pack_pair(hi, lo):
    hb = lax.bitcast_convert_type(hi.astype(BF16).astype(F32), U32)
    lb = lax.bitcast_convert_type(lo.astype(BF16).astype(F32), U32)
    return (hb & jnp.uint32(0xFFFF0000)) | (lb >> 16)


def _unpack_pair(w):
    hi = lax.bitcast_convert_type(w & jnp.uint32(0xFFFF0000), F32)
    lo = lax.bitcast_convert_type(w << 16, F32)
    return hi, lo


def _fetch_step(view, buf, sem):
    k = pl.program_id(0)
    slot = k % 2

    def copy(step, sl):
        return pltpu.make_async_copy(view(step), buf.at[sl], sem.at[sl])

    @pl.when(k == 0)
    def _():
        copy(0, 0).start()

    @pl.when(k + 1 < pl.num_programs(0))
    def _():
        copy(k + 1, 1 - slot).start()

    copy(k, slot).wait()
    return buf.at[slot]


def _store_step(view, buf, sem, fill):
    k = pl.program_id(0)
    slot = k % 2

    def copy(step, sl):
        return pltpu.make_async_copy(buf.at[sl], view(step), sem.at[sl])

    @pl.when(k >= 2)
    def _():
        copy(k - 2, slot).wait()

    fill(buf.at[slot])
    copy(k, slot).start()

    @pl.when(k == pl.num_programs(0) - 1)
    def _():
        copy(k, slot).wait()

        @pl.when(k >= 1)
        def _():
            copy(k - 1, 1 - slot).wait()


def _unpack_cols(w):
    hi, lo = _unpack_pair(w)
    return jnp.concatenate([hi, lo], axis=1)


def _pack_cols(x):
    half = x.shape[1] // 2
    return _pack_pair(x[:, :half], x[:, half:])


def _fft_a_kernel(x_hbm, a_ref, twr_ref, twi_ref, y_ref, xbuf, sem):
    n1 = a_ref.shape[0] // 2
    x_ref = _fetch_step(lambda j: x_hbm.at[:, :, j, :], xbuf, sem)
    rhs = jnp.concatenate([_unpack_cols(x_ref[0]), _unpack_cols(x_ref[1])], axis=0).astype(BF16)
    y = jnp.dot(a_ref[...], rhs, preferred_element_type=F32)
    yr, yi = y[:n1], y[n1:]
    tr, ti = twr_ref[...], twi_ref[...]
    w = _pack_pair(yr * tr - yi * ti, yr * ti + yi * tr)
    q = n1 // FFT_K1_GROUP
    d = w.shape[1]
    for kk in range(FFT_K1_GROUP):
        y_ref[:, kk * d:(kk + 1) * d] = w[kk * q:(kk + 1) * q]


def _fft_a_call(x4, a_mat, tw_n2):
    _, hn, n2, half = x4.shape
    d = 2 * half
    n1 = 2 * hn
    q = n1 // FFT_K1_GROUP
    return pl.pallas_call(
        _fft_a_kernel,
        out_shape=jax.ShapeDtypeStruct((n2, q, FFT_K1_GROUP * d), U32),
        grid=(n2,),
        in_specs=[
            pl.BlockSpec(memory_space=pl.ANY),
            pl.BlockSpec((2 * n1, n1), lambda j: (0, 0)),
            pl.BlockSpec((None, n1, 1), lambda j: (j, 0, 0)),
            pl.BlockSpec((None, n1, 1), lambda j: (j, 0, 0)),
        ],
        out_specs=pl.BlockSpec((None, q, FFT_K1_GROUP * d), lambda j: (j, 0, 0)),
        scratch_shapes=[pltpu.VMEM((2, 2, hn, half), U32), pltpu.SemaphoreType.DMA((2,))],
        compiler_params=_cparams("arbitrary"),
        name="fft_stage_a",
    )(x4, a_mat, tw_n2[0], tw_n2[1])


def _load_k1(y_ref, kk):
    d = y_ref.shape[1] // FFT_K1_GROUP
    yr, yi = _unpack_pair(y_ref[:, kk * d:(kk + 1) * d])
    return jnp.concatenate([yr, yi], axis=0).astype(BF16)


def _fft_b_kernel(y_hbm, b_ref, h_ref, ybuf, sem):
    y_ref = _fetch_step(lambda kb: y_hbm.at[:, kb, :], ybuf, sem)
    n2 = y_ref.shape[0]
    for kk in range(FFT_K1_GROUP):
        x = jnp.dot(b_ref[...], _load_k1(y_ref, kk), preferred_element_type=F32)
        h_ref[kk] = _pack_pair(x[:n2], x[n2:])


def _fft_b_call(y, b_fwd):
    n2, q, gd = y.shape
    d = gd // FFT_K1_GROUP
    return pl.pallas_call(
        _fft_b_kernel,
        out_shape=jax.ShapeDtypeStruct((q * FFT_K1_GROUP, n2, d), U32),
        grid=(q,),
        in_specs=[pl.BlockSpec(memory_space=pl.ANY),
                  pl.BlockSpec((2 * n2, 2 * n2), lambda k: (0, 0))],
        out_specs=pl.BlockSpec((FFT_K1_GROUP, n2, d), lambda k: (k, 0, 0)),
        scratch_shapes=[pltpu.VMEM((2, n2, gd), U32), pltpu.SemaphoreType.DMA((2,))],
        compiler_params=_cparams("arbitrary"),
        name="fft_filter_b",
    )(y, b_fwd)


def _fft_bc_kernel(y_hbm, h_ref, bf_ref, bi_ref, twr_ref, twi_ref, g_hbm, ybuf, gbuf, sem_in, sem_out):
    y_ref = _fetch_step(lambda kb: y_hbm.at[:, kb, :], ybuf, sem_in)
    n2 = y_ref.shape[0]
    d = y_ref.shape[1] // FFT_K1_GROUP

    def fill(g_ref):
        for kk in range(FFT_K1_GROUP):
            x = jnp.dot(bf_ref[...], _load_k1(y_ref, kk), preferred_element_type=F32)
            xr, xi = x[:n2], x[n2:]
            hr, hi = _unpack_pair(h_ref[kk])
            z = jnp.concatenate([xr * hr - xi * hi, xr * hi + xi * hr], axis=0).astype(BF16)
            g = jnp.dot(bi_ref[...], z, preferred_element_type=F32)
            gr, gi = g[:n2], g[n2:]
            tr, ti = twr_ref[kk], twi_ref[kk]
            g_ref[:, kk * d:(kk + 1) * d] = _pack_pair(gr * tr + gi * ti, gi * tr - gr * ti)

    _store_step(lambda kb: g_hbm.at[:, kb, :], gbuf, sem_out, fill)


def _fft_bc_call(y, hspec, b_fwd, b_inv, tw_k1):
    n2, q, gd = y.shape
    d = gd // FFT_K1_GROUP
    mat = lambda: pl.BlockSpec((2 * n2, 2 * n2), lambda k: (0, 0))
    tw = lambda: pl.BlockSpec((FFT_K1_GROUP, n2, 1), lambda k: (k, 0, 0))
    return pl.pallas_call(
        _fft_bc_kernel,
        out_shape=jax.ShapeDtypeStruct((n2, q, gd), U32),
        grid=(q,),
        in_specs=[pl.BlockSpec(memory_space=pl.ANY),
                  pl.BlockSpec((FFT_K1_GROUP, n2, d), lambda k: (k, 0, 0)), mat(), mat(), tw(), tw()],
        out_specs=pl.BlockSpec(memory_space=pl.ANY),
        scratch_shapes=[pltpu.VMEM((2, n2, gd), U32), pltpu.VMEM((2, n2, gd), U32),
                        pltpu.SemaphoreType.DMA((2,)), pltpu.SemaphoreType.DMA((2,))],
        compiler_params=_cparams("arbitrary"),
        name="fft_stage_bc",
    )(y, hspec, b_fwd, b_inv, tw_k1[0], tw_k1[1])


def _fft_d_kernel(g_ref, a_ref, x0_hbm, z_hbm, l1_ref, fb_ref, o_hbm, x0buf, zbuf, obuf, sem_x, sem_z, sem_o):
    hn = a_ref.shape[0] // 2
    d = g_ref.shape[1] // FFT_K1_GROUP
    x0_ref = _fetch_step(lambda j: x0_hbm.at[:, :, j, :], x0buf, sem_x)
    z_ref = _fetch_step(lambda j: z_hbm.at[:, :, j, :], zbuf, sem_z)
    parts = [_unpack_pair(g_ref[:, kk * d:(kk + 1) * d]) for kk in range(FFT_K1_GROUP)]
    rhs = jnp.concatenate([p[0] for p in parts] + [p[1] for p in parts], axis=0).astype(BF16)
    y = jnp.dot(a_ref[...], rhs, preferred_element_type=F32)
    inv_l1 = 1.0 / l1_ref[...]
    fb = fb_ref[...]

    def fill(o_ref):
        for b in range(2):
            conv = y[b * hn:(b + 1) * hn] * inv_l1
            o_ref[b] = _pack_cols(_unpack_cols(x0_ref[b]) * (conv + fb * _unpack_cols(z_ref[b])))

    _store_step(lambda j: o_hbm.at[:, :, j, :], obuf, sem_o, fill)


def _fft_d_call(g, a_inv, x0_4, z_4, l1, fbias):
    n2, q, gd = g.shape
    d = gd // FFT_K1_GROUP
    n1 = q * FFT_K1_GROUP
    hn = n1 // 2
    half = d // 2
    tok_buf = lambda: pltpu.VMEM((2, 2, hn, half), U32)
    return pl.pallas_call(
        _fft_d_kernel,
        out_shape=jax.ShapeDtypeStruct((2, hn, n2, half), U32),
        grid=(n2,),
        in_specs=[pl.BlockSpec((None, q, gd), lambda j: (j, 0, 0)),
                  pl.BlockSpec((n1, 2 * n1), lambda j: (0, 0)),
                  pl.BlockSpec(memory_space=pl.ANY), pl.BlockSpec(memory_space=pl.ANY),
                  pl.BlockSpec((1, d), lambda j: (0, 0)),
                  pl.BlockSpec((1, d), lambda j: (0, 0))],
        out_specs=pl.BlockSpec(memory_space=pl.ANY),
        scratch_shapes=[tok_buf(), tok_buf(), tok_buf(),
                        pltpu.SemaphoreType.DMA((2,)), pltpu.SemaphoreType.DMA((2,)), pltpu.SemaphoreType.DMA((2,))],
        compiler_params=_cparams("arbitrary"),
        name="fft_stage_d",
    )(g, a_inv, x0_4, z_4, l1, fbias)


def _hyena_conv(x0p, zp, filtp, l1, fbias):
    bsz, seq, half = zp.shape
    assert bsz == 2, "the two batch rows are packed as one complex signal"
    n2 = FFT_N2
    n1 = 2 * seq // n2
    tabs = _dft_tables(n1, n2)
    hn = n1 // 2
    view = lambda a: a.reshape(2, hn, n2, half)
    hspec = _fft_b_call(_fft_a_call(view(filtp), tabs["a_filt"], tabs["tw_n2"]), tabs["b_fwd"])
    y = _fft_a_call(view(zp), tabs["a_data"], tabs["tw_n2"])
    g = _fft_bc_call(y, hspec, tabs["b_fwd"], tabs["b_inv"], tabs["tw_k1"])
    out = _fft_d_call(g, tabs["a_inv"], view(x0p), view(zp), l1, fbias)
    return out.reshape(bsz, seq, half)


def _rope_tables(seq, head_dim):
    axis_dim = head_dim // 2
    rows = seq // GRID_W
    inv = ROPE_BASE ** (-jnp.arange(0, axis_dim, 2, dtype=F32) / axis_dim)
    row = jnp.repeat(jnp.arange(rows, dtype=F32), GRID_W)[:, None] * inv
    col = jnp.tile(jnp.arange(GRID_W, dtype=F32), rows)[:, None] * inv
    quarter = axis_dim // 2
    cos = jnp.concatenate([jnp.cos(row), jnp.cos(row), jnp.cos(col), jnp.cos(col)], axis=1)
    sin = jnp.concatenate([jnp.sin(row), jnp.sin(row), jnp.sin(col), jnp.sin(col)], axis=1)
    second = (np.arange(head_dim) % axis_dim) >= quarter
    reps = LANES // head_dim
    cos = jnp.tile(cos, (1, reps))
    sin = jnp.tile(sin, (1, reps))
    second = jnp.asarray(np.tile(second, reps))[None, :]
    return cos, jnp.where(second, sin, 0.0), jnp.where(second, 0.0, -sin)


def kernel(x, c, ctx, c_ctx, ada_w, ada_b, attn_w_in, attn_b_in, attn_sink, attn_w_out, hy_w_in, hy_b_in, hy_conv_w, hy_conv_b, hy_f_w1, hy_f_b1, hy_f_w2, hy_f_b2, hy_f_w3, hy_f_b3, hy_f_freq, hy_f_w4, hy_f_bias, hy_w_out, hy_b_out, ln1_g, ln1_b, ln2_g, ln2_b, router_w, router_b, moe_w1, moe_w3, moe_w2):
    bsz, seq, d = x.shape
    depth = ada_w.shape[0]
    assert depth == 2 and attn_w_in.shape[0] == 1 and hy_w_in.shape[0] == 1
    alpha = (2 * depth) ** 0.25
    n_heads = attn_sink.shape[1]
    attn_dim = attn_w_out.shape[1]
    head_dim = attn_dim // n_heads
    kv_dim = (attn_w_in.shape[2] - attn_dim) // 2
    n_kv = kv_dim // head_dim
    group = n_heads // n_kv
    assert head_dim * 2 == LANES and group % 2 == 0
    n_exp = router_w.shape[1]
    assert n_exp == N_GROUPS * EXPERTS_PER_GROUP

    cond = jnp.concatenate([c, c_ctx[None, :], jnp.zeros((8 - bsz - 1, d), F32)], axis=0)
    mods = _ada_call(cond, ada_w, ada_b).reshape(depth, 8, 6, d)
    mod = lambda layer, k: mods[layer, :bsz, k].reshape(bsz, 1, d)
    cmod = lambda layer, k: mods[layer, bsz, k].reshape(1, d)
    row = lambda v: v.reshape(1, -1)

    rw_pad = jnp.concatenate([router_w, jnp.zeros((d, LANES - n_exp), F32)], axis=1).astype(BF16)
    rb = router_b.reshape(n_exp, 1)
    w1b, w3b, w2b = moe_w1.astype(BF16), moe_w3, moe_w2

    w_ext = attn_w_in[0].astype(BF16)
    b_ext = row(attn_b_in[0])
    w_kv, b_kv = w_ext[:, attn_dim:], b_ext[:, attn_dim:]
    n_kd = 2 * kv_dim
    cos_t, sa_t, sb_t = _rope_tables(seq, head_dim)

    q, kd, vd = _qkv_call(x, mod(0, 1), mod(0, 0), w_ext, b_ext, cos_t, sa_t, sb_t, attn_dim, n_kd, head_dim)
    kxd, vxd = _ctx_kv_call(ctx, cmod(0, 1), cmod(0, 0), w_kv, b_kv, n_kd)
    att = _attn_call(attn_sink[0], q, kd, vd, kxd, vxd, n_kv, group)
    h1, xp, meta, cnt = _proj_ln_call(att, attn_w_out[0].astype(BF16), jnp.zeros((1, d), F32), x, mod(0, 2),
                                      row(ln1_g[0]), row(ln1_b[0]), mod(0, 4), mod(0, 3), rw_pad, rb, alpha)
    h = _moe_layer(xp, meta, cnt, h1, mod(0, 5), row(ln2_g[0]), row(ln2_b[0]), w1b, w3b, w2b, 0, alpha)

    x0, z = _hy_in_call(h, mod(1, 1), mod(1, 0), hy_w_in[0].astype(BF16), row(hy_b_in[0]), hy_conv_w[0],
                        row(hy_conv_b[0]))
    filt, l1 = _filter_call(hy_f_w1[0], hy_f_b1[0], hy_f_w2[0], hy_f_b2[0], hy_f_w3[0], hy_f_b3[0],
                            hy_f_freq[0], hy_f_w4[0], seq)
    yh = _hyena_conv(x0, z, filt, l1, row(hy_f_bias[0]))
    h1, xp, meta, cnt = _proj_ln_call(yh, hy_w_out[0].astype(BF16), row(hy_b_out[0]), h, mod(1, 2),
                                      row(ln1_g[1]), row(ln1_b[1]), mod(1, 4), mod(1, 3), rw_pad, rb, alpha)
    return _moe_layer(xp, meta, cnt, h1, mod(1, 5), row(ln2_g[1]), row(ln2_b[1]), w1b, w3b, w2b, 1, alpha)
```
